```python
import math
import jax, jax.numpy as jnp
from jax import lax
import numpy as np

D_MODEL = 1024
BATCH = 4
SEQ = 8192
DEPTH = 4

CTX_LEN = 256
GRID_W = 64
W_CONV = 512
W_HYENA = 512
HYENA_ORDER = 2
HYENA_EMB = 33
HYENA_BANDS = (HYENA_EMB - 1) // 2
HYENA_HID = 64
HYENA_FAST_DECAY = 0.3
HYENA_SLOW_DECAY = 1.5
HYENA_TARGET = 1e-2
N_HEADS = 8
QK_NOPE = 64
QK_ROPE = 32
V_HEAD = 64
Q_LORA = 384
KV_LORA = 256
W_MLA = N_HEADS * V_HEAD
ROPE_BASE = 10000.0
Q_BLOCK = 128
N_BRANCH = 3
EPS = 1e-6
IN_SPLITS = (W_CONV, W_CONV, W_CONV, W_CONV, 3 * W_HYENA, W_HYENA, Q_LORA, KV_LORA, QK_ROPE, W_MLA, N_BRANCH * D_MODEL)
N_IN = 4 * W_CONV + 4 * W_HYENA + Q_LORA + KV_LORA + QK_ROPE + W_MLA + N_BRANCH * D_MODEL

kernel_name = 'hybrid_conv_hyena_mla_prefix_trunk'


def rmsnorm(x, g):
    xf = x.astype(jnp.float32)
    y = xf * lax.rsqrt(jnp.mean(xf * xf, axis=-1, keepdims=True) + EPS)
    return (y * g.astype(jnp.float32)).astype(x.dtype)


def split_cols(p):
    outs = []
    o = 0
    for n in IN_SPLITS:
        outs.append(p[..., o:o + n])
        o += n
    return outs


def dwconv3(u, w, b):
    up = jnp.pad(u, ((0, 0), (1, 1), (0, 0)))
    return up[:, :-2] * w[0] + up[:, 1:-1] * w[1] + up[:, 2:] * w[2] + b


def axial_rope_tables(L):
    rows = L // GRID_W
    row = jnp.broadcast_to(jnp.arange(rows, dtype=jnp.float32)[:, None], (rows, GRID_W)).reshape(L)
    col = jnp.broadcast_to(jnp.arange(GRID_W, dtype=jnp.float32)[None, :], (rows, GRID_W)).reshape(L)
    n_f = QK_ROPE // 4
    inv = ROPE_BASE ** (-jnp.arange(n_f, dtype=jnp.float32) / n_f)
    ang = jnp.concatenate([row[:, None] * inv, col[:, None] * inv], axis=-1)
    return jnp.cos(ang), jnp.sin(ang)


def apply_rope(x, cos, sin):
    half = QK_ROPE // 2
    x1, x2 = x[..., :half], x[..., half:]
    c = cos[:, None, :].astype(x.dtype)
    s = sin[:, None, :].astype(x.dtype)
    return jnp.concatenate([x1 * c - x2 * s, x1 * s + x2 * c], axis=-1)


def mla_qkv(cq, ckv, kr, lp, rope):
    B, L = cq.shape[:2]
    q = (rmsnorm(cq, lp['mla_q_norm']) @ lp['mla_w_uq']).reshape(B, L, N_HEADS, QK_NOPE + QK_ROPE)
    qn, qr = q[..., :QK_NOPE], q[..., QK_NOPE:]
    kv = (rmsnorm(ckv, lp['mla_kv_norm']) @ lp['mla_w_ukv']).reshape(B, L, N_HEADS, QK_NOPE + V_HEAD)
    kn, v = kv[..., :QK_NOPE], kv[..., QK_NOPE:]
    kr = kr[:, :, None, :]
    if rope is not None:
        qr = apply_rope(qr, rope[0], rope[1])
        kr = apply_rope(kr, rope[0], rope[1])
    return qn, qr, kn, kr[:, :, 0], v


def attend(qn, qr, kn, kr, v):
    scale = (QK_NOPE + QK_ROPE) ** -0.5
    s = jnp.einsum('bqhd,bkhd->bhqk', qn, kn) + jnp.einsum('bqhr,bkr->bhqk', qr, kr)
    p = jax.nn.softmax(s.astype(jnp.float32) * scale, axis=-1).astype(v.dtype)
    return jnp.einsum('bhqk,bkhd->bqhd', p, v)


def attend_blocked(qn, qr, kn, kr, v):
    B, L = qn.shape[:2]
    nb = L // Q_BLOCK

    def blk(a):
        return a.reshape(B, nb, Q_BLOCK, *a.shape[2:]).swapaxes(0, 1)

    out = lax.map(lambda qs: attend(qs[0], qs[1], kn, kr, v), (blk(qn), blk(qr)))
    return out.swapaxes(0, 1).reshape(B, L, W_MLA)


def hyena_filter_fft(L, lp):
    dt = lp['hy_w1'].dtype
    t = jnp.linspace(0.0, 1.0, L, dtype=jnp.float32)[:, None]
    w = 2.0 * math.pi * jnp.arange(L, dtype=jnp.float32)[:, None] / L
    f = jnp.linspace(1e-4, HYENA_BANDS - 1, HYENA_BANDS, dtype=jnp.float32)[None, :]
    z = jnp.concatenate([t, jnp.cos(f * w), -jnp.sin(f * w)], axis=-1).astype(dt)
    hid = jnp.sin(lp['hy_freq'] * (z @ lp['hy_w1'] + lp['hy_b1']))
    hid = jnp.sin(lp['hy_freq'] * (hid @ lp['hy_w2'] + lp['hy_b2']))
    h = (hid @ lp['hy_w3']).astype(jnp.float32).reshape(L, 2, HYENA_ORDER, W_HYENA)
    max_decay = math.log(HYENA_TARGET) / HYENA_FAST_DECAY
    min_decay = math.log(HYENA_TARGET) / HYENA_SLOW_DECAY
    deltas = jnp.abs(jnp.linspace(min_decay, max_decay, W_HYENA, dtype=jnp.float32))
    h = h * jnp.exp(-t * deltas)[:, None, None, :]
    h = h / jnp.sum(jnp.abs(h), axis=(0, 1), keepdims=True)
    fwd = h[:, 0]
    bwd = h[1:, 1][::-1]
    k_full = jnp.concatenate([fwd, jnp.zeros((1, HYENA_ORDER, W_HYENA), jnp.float32), bwd], axis=0)
    return jnp.fft.rfft(k_full, axis=0)


def fftconv(u, kf, skip):
    L = u.shape[1]
    uf32 = u.astype(jnp.float32)
    uf = jnp.fft.rfft(uf32, n=2 * L, axis=1)
    y = jnp.fft.irfft(uf * kf[None], n=2 * L, axis=1)[:, :L]
    return (y + uf32 * skip.astype(jnp.float32)).astype(u.dtype)


def hyena_mix(hproj, lp, kf):
    u = dwconv3(hproj, lp['hy_conv_w'], lp['hy_conv_b'])
    v, x1, x2 = u[..., :W_HYENA], u[..., W_HYENA:2 * W_HYENA], u[..., 2 * W_HYENA:]
    z = x1 * fftconv(v, kf[:, 0], lp['hy_skip'][0])
    z = x2 * fftconv(z, kf[:, 1], lp['hy_skip'][1])
    return z


def merged_branches(parts, att, kf, lp):
    xin, gb, gc, za, hproj, zh, _cq, _ckv, _kr, zm, gates = parts
    ya = (jax.nn.silu(za) * (gb * dwconv3(gc * xin, lp['sc_conv_w'], lp['sc_conv_b']))) @ lp['sc_out']
    yh = (jax.nn.silu(zh) * hyena_mix(hproj, lp, kf)) @ lp['hy_out']
    ym = (jax.nn.silu(zm) * att) @ lp['mla_out']
    g = jax.nn.sigmoid(gates.reshape(*gates.shape[:-1], N_BRANCH, D_MODEL))
    y = g[..., 0, :] * ya + g[..., 1, :] * yh + g[..., 2, :] * ym
    return y @ lp['w_o']


def modulated_norm(x, g, mod):
    shift, scale, gate = mod[..., :D_MODEL], mod[..., D_MODEL:2 * D_MODEL], mod[..., 2 * D_MODEL:]
    return rmsnorm(x, g) * (1 + scale) + shift, gate


def setup_inputs(seed: int = 0) -> dict:
    key = jax.random.key(seed)
    ks = iter(jax.random.split(key, 32))

    def nrm(shape, s):
        return jax.random.normal(next(ks), shape, jnp.float32) * s

    D = D_MODEL
    return {
        'x': nrm((BATCH, SEQ, D), 1.0),
        'c': nrm((BATCH, D), 1.0),
        'ctx': nrm((BATCH, CTX_LEN, D), 1.0),
        'c_ctx': nrm((D,), 1.0),
        'ada_w': nrm((DEPTH, D, 3 * D), 0.5 * D ** -0.5),
        'ada_b': nrm((DEPTH, 3 * D), 0.02),
        'norm_g': 1.0 + nrm((DEPTH, D), 0.02),
        'w_in': nrm((DEPTH, D, N_IN), D ** -0.5),
        'b_in': nrm((DEPTH, N_IN), 0.02),
        'sc_conv_w': nrm((DEPTH, 3, W_CONV), 3 ** -0.5),
        'sc_conv_b': nrm((DEPTH, W_CONV), 0.02),
        'sc_out': nrm((DEPTH, W_CONV, D), W_CONV ** -0.5),
        'hy_conv_w': nrm((DEPTH, 3, 3 * W_HYENA), 3 ** -0.5),
        'hy_conv_b': nrm((DEPTH, 3 * W_HYENA), 0.02),
        'hy_w1': nrm((DEPTH, HYENA_EMB, HYENA_HID), HYENA_EMB ** -0.5),
        'hy_b1': nrm((DEPTH, HYENA_HID), 0.02),
        'hy_w2': nrm((DEPTH, HYENA_HID, HYENA_HID), HYENA_HID ** -0.5),
        'hy_b2': nrm((DEPTH, HYENA_HID), 0.02),
        'hy_w3': nrm((DEPTH, HYENA_HID, 2 * HYENA_ORDER * W_HYENA), HYENA_HID ** -0.5),
        'hy_freq': 1.0 + nrm((DEPTH, HYENA_HID), 0.1),
        'hy_skip': nrm((DEPTH, HYENA_ORDER, W_HYENA), 0.2),
        'hy_out': nrm((DEPTH, W_HYENA, D), W_HYENA ** -0.5),
        'mla_q_norm': 1.0 + nrm((DEPTH, Q_LORA), 0.02),
        'mla_w_uq': nrm((DEPTH, Q_LORA, N_HEADS * (QK_NOPE + QK_ROPE)), Q_LORA ** -0.5),
        'mla_kv_norm': 1.0 + nrm((DEPTH, KV_LORA), 0.02),
        'mla_w_ukv': nrm((DEPTH, KV_LORA, N_HEADS * (QK_NOPE + V_HEAD)), KV_LORA ** -0.5),
        'mla_out': nrm((DEPTH, W_MLA, D), W_MLA ** -0.5),
        'w_o': nrm((DEPTH, D, D), D ** -0.5),
        'final_g': 1.0 + nrm((D,), 0.02),
    }


def reference(x, c, ctx, c_ctx, ada_w, ada_b, norm_g, w_in, b_in, sc_conv_w, sc_conv_b, sc_out,
              hy_conv_w, hy_conv_b, hy_w1, hy_b1, hy_w2, hy_b2, hy_w3, hy_freq, hy_skip, hy_out,
              mla_q_norm, mla_w_uq, mla_kv_norm, mla_w_ukv, mla_out, w_o, final_g):
    B, L = x.shape[:2]
    Lc = ctx.shape[1]
    rope = axial_rope_tables(L)
    x_lat, x_ctx = x, ctx
    for i in range(DEPTH):
        lp = {
            'sc_conv_w': sc_conv_w[i], 'sc_conv_b': sc_conv_b[i], 'sc_out': sc_out[i],
            'hy_conv_w': hy_conv_w[i], 'hy_conv_b': hy_conv_b[i], 'hy_w1': hy_w1[i], 'hy_b1': hy_b1[i],
            'hy_w2': hy_w2[i], 'hy_b2': hy_b2[i], 'hy_w3': hy_w3[i], 'hy_freq': hy_freq[i],
            'hy_skip': hy_skip[i], 'hy_out': hy_out[i],
            'mla_q_norm': mla_q_norm[i], 'mla_w_uq': mla_w_uq[i], 'mla_kv_norm': mla_kv_norm[i],
            'mla_w_ukv': mla_w_ukv[i], 'mla_out': mla_out[i], 'w_o': w_o[i],
        }
        last = i == DEPTH - 1
        mod_l = (jax.nn.silu(c) @ ada_w[i] + ada_b[i])[:, None, :]
        mod_c = (jax.nn.silu(c_ctx) @ ada_w[i] + ada_b[i])[None, None, :]
        h_l, gate_l = modulated_norm(x_lat, norm_g[i], mod_l)
        h_c, gate_c = modulated_norm(x_ctx, norm_g[i], mod_c)
        parts_l = split_cols(h_l @ w_in[i] + b_in[i])
        parts_c = split_cols(h_c @ w_in[i] + b_in[i])
        qn_c, qr_c, kn_c, kr_c, v_c = mla_qkv(parts_c[6], parts_c[7], parts_c[8], lp, None)
        qn_l, qr_l, kn_l, kr_l, v_l = mla_qkv(parts_l[6], parts_l[7], parts_l[8], lp, rope)
        att_l = attend_blocked(qn_l, qr_l,
                               jnp.concatenate([kn_c, kn_l], axis=1),
                               jnp.concatenate([kr_c, kr_l], axis=1),
                               jnp.concatenate([v_c, v_l], axis=1))
        kf_l = hyena_filter_fft(L, lp)
        new_lat = x_lat + gate_l * merged_branches(parts_l, att_l, kf_l, lp)
        if not last:
            att_c = attend(qn_c, qr_c, kn_c, kr_c, v_c).reshape(B, Lc, W_MLA)
            kf_c = hyena_filter_fft(Lc, lp)
            x_ctx = x_ctx + gate_c * merged_branches(parts_c, att_c, kf_c, lp)
        x_lat = new_lat
    return rmsnorm(x_lat, final_g)
```

```python
import functools
import math

import jax
import jax.numpy as jnp
from jax import lax
from jax.experimental import pallas as pl
from jax.experimental.pallas import tpu as pltpu

F32 = jnp.float32
BF16 = jnp.bfloat16

D_MODEL = 1024
DEPTH = 4
GRID_W = 64
W_CONV = 512
W_HYENA = 512
HYENA_ORDER = 2
HYENA_EMB = 33
HYENA_BANDS = (HYENA_EMB - 1) // 2
HYENA_HID = 64
HYENA_FAST_DECAY = 0.3
HYENA_SLOW_DECAY = 1.5
HYENA_TARGET = 1e-2
N_HEADS = 8
QK_NOPE = 64
QK_ROPE = 32
V_HEAD = 64
Q_LORA = 384
KV_LORA = 256
W_MLA = N_HEADS * V_HEAD
ROPE_BASE = 10000.0
N_BRANCH = 3
EPS = 1e-6

O_XIN, O_GB, O_GC, O_ZA = 0, 512, 1024, 1536
O_HPROJ, O_ZH = 2048, 3584
O_CQ, O_CKV, O_KR, O_ZM, O_GATES = 4096, 4480, 4736, 4768, 5280
N_IN = 8352

LANE = 128
SUBLANE = 8
HEAD_PAD = 128
N2 = 128
VMEM_LIMIT = 56 * 1024 * 1024


def _cparams(sem):
    return pltpu.CompilerParams(dimension_semantics=sem, vmem_limit_bytes=VMEM_LIMIT)


def _ada_kernel(c_ref, w_ref, b_ref, o_ref):
    c = c_ref[...]
    s = c * jax.nn.sigmoid(c)
    o_ref[...] = jnp.dot(s, w_ref[...], preferred_element_type=F32,
                         precision=lax.Precision.HIGHEST) + b_ref[...]


def _ada_mods(cc, ada_w, ada_b):
    depth = ada_w.shape[0]
    d = cc.shape[1]
    return pl.pallas_call(
        _ada_kernel,
        grid=(depth, 3),
        in_specs=[pl.BlockSpec((8, d), lambda l, j: (0, 0)),
                  pl.BlockSpec((None, d, d), lambda l, j: (l, 0, j)),
                  pl.BlockSpec((None, 1, d), lambda l, j: (l, 0, j))],
        out_specs=pl.BlockSpec((None, 8, d), lambda l, j: (l, 0, j)),
        out_shape=jax.ShapeDtypeStruct((depth, 8, 3 * d), F32),
        compiler_params=_cparams(("parallel", "parallel")),
        name="ada_mods",
    )(cc, ada_w, ada_b.reshape(depth, 1, 3 * d))


def _mod_norm(x, g, scale, shift):
    y = x * lax.rsqrt(jnp.mean(x * x, axis=-1, keepdims=True) + EPS)
    return (y * g) * (1.0 + scale) + shift


def _rms(x, g):
    return x * lax.rsqrt(jnp.mean(x * x, axis=-1, keepdims=True) + EPS) * g


def _silu(x):
    return x * jax.nn.sigmoid(x)


def _dwconv3_ext(u_ext, w, b, first, last, tm):
    n = tm + 2 * SUBLANE
    prev = pltpu.roll(u_ext, 1, 0)[SUBLANE:SUBLANE + tm]
    nxt = pltpu.roll(u_ext, n - 1, 0)[SUBLANE:SUBLANE + tm]
    row = lax.broadcasted_iota(jnp.int32, (tm, 1), 0)
    prev = jnp.where(jnp.logical_and(first, row == 0), 0.0, prev)
    nxt = jnp.where(jnp.logical_and(last, row == tm - 1), 0.0, nxt)
    return prev * w[0:1] + u_ext[SUBLANE:SUBLANE + tm] * w[1:2] + nxt * w[2:3] + b


def _proj_ext(xm_ref, xp_ref, xn_ref, g_ref, sc_ref, sh_ref, w_ref, b_ref):
    x_ext = jnp.concatenate([xp_ref[...], xm_ref[...], xn_ref[...]], axis=0)
    h = _mod_norm(x_ext, g_ref[...], sc_ref[...], sh_ref[...])
    return jnp.dot(h.astype(BF16), w_ref[...], preferred_element_type=F32) + b_ref[...]


def _kp1_kernel(xm_ref, xp_ref, xn_ref, g_ref, sc_ref, sh_ref, w_ref, b_ref, cw_ref, cb_ref,
                a_ref, sm_ref, *, tm, tiles_per_seq):
    i = pl.program_id(0)
    first = (i % tiles_per_seq) == 0
    last = (i % tiles_per_seq) == tiles_per_seq - 1
    p = _proj_ext(xm_ref, xp_ref, xn_ref, g_ref, sc_ref, sh_ref, w_ref, b_ref)
    prod = p[:, 2 * W_CONV:3 * W_CONV] * p[:, 0:W_CONV]
    conv = _dwconv3_ext(prod, cw_ref[...], cb_ref[...], first, last, tm)
    pm = p[SUBLANE:SUBLANE + tm]
    a = _silu(pm[:, 3 * W_CONV:4 * W_CONV]) * (pm[:, W_CONV:2 * W_CONV] * conv)
    a_ref[...] = a.astype(a_ref.dtype)
    sm_ref[...] = _silu(pm[:, 4 * W_CONV:4 * W_CONV + W_MLA]).astype(sm_ref.dtype)


def _kp2_kernel(xm_ref, xp_ref, xn_ref, g_ref, sc_ref, sh_ref, w_ref, b_ref, cw_ref, cb_ref,
                v_ref, x1_ref, m2_ref, *, tm, tiles_per_seq):
    i = pl.program_id(0)
    first = (i % tiles_per_seq) == 0
    last = (i % tiles_per_seq) == tiles_per_seq - 1
    p = _proj_ext(xm_ref, xp_ref, xn_ref, g_ref, sc_ref, sh_ref, w_ref, b_ref)
    u = _dwconv3_ext(p[:, 0:3 * W_HYENA], cw_ref[...], cb_ref[...], first, last, tm)
    zh = p[SUBLANE:SUBLANE + tm, 3 * W_HYENA:4 * W_HYENA]
    v_ref[...] = u[:, 0:W_HYENA].astype(v_ref.dtype)
    x1_ref[...] = u[:, W_HYENA:2 * W_HYENA].astype(x1_ref.dtype)
    m2_ref[...] = (_silu(zh) * u[:, 2 * W_HYENA:3 * W_HYENA]).astype(m2_ref.dtype)


def _kp3_kernel(xm_ref, g_ref, sc_ref, sh_ref, w_ref, b_ref, qg_ref, kvg_ref, wq_ref, wqs_ref,
                wk_ref, wv_ref, e2_ref, cosq_ref, sinq_ref, cosk_ref, sink_ref,
                q_ref, k_ref, v_ref):
    h = _mod_norm(xm_ref[...], g_ref[...], sc_ref[...], sh_ref[...])
    p = jnp.dot(h.astype(BF16), w_ref[...], preferred_element_type=F32) + b_ref[...]
    cq = _rms(p[:, 0:Q_LORA], qg_ref[...]).astype(BF16)
    ckv = _rms(p[:, Q_LORA:Q_LORA + KV_LORA], kvg_ref[...]).astype(BF16)
    kr = p[:, 640:640 + QK_ROPE]
    krs = p[:, 768:768 + QK_ROPE]
    qa = jnp.dot(cq, wq_ref[...], preferred_element_type=F32)
    qb = jnp.dot(cq, wqs_ref[...], preferred_element_type=F32)
    q_ref[...] = (qa * cosq_ref[...] + qb * sinq_ref[...]).astype(q_ref.dtype)
    kr_rot = (kr * cosk_ref[...] + krs * sink_ref[...]).astype(BF16)
    kn = jnp.dot(ckv, wk_ref[...], preferred_element_type=F32)
    krp = jnp.dot(kr_rot, e2_ref[...], preferred_element_type=F32)
    k_ref[...] = (kn + krp).astype(k_ref.dtype)
    v_ref[...] = jnp.dot(ckv, wv_ref[...], preferred_element_type=F32).astype(v_ref.dtype)


def _mod_spec(mod, tiles_per_seq):
    d = mod.shape[-1]
    if mod.shape[0] == 1:
        return pl.BlockSpec((None, 1, d), lambda i: (0, 0, 0))
    return pl.BlockSpec((None, 1, d), lambda i: (i // tiles_per_seq, 0, 0))


def _full_spec(a):
    nd = a.ndim
    return pl.BlockSpec(a.shape, lambda i: (0,) * nd)


def _halo_specs(n_rows, tm, d):
    r = tm // SUBLANE
    nb = n_rows // SUBLANE
    return [pl.BlockSpec((tm, d), lambda i: (i, 0)),
            pl.BlockSpec((SUBLANE, d), lambda i: (jnp.maximum(i * r - 1, 0), 0)),
            pl.BlockSpec((SUBLANE, d), lambda i: (jnp.minimum((i + 1) * r, nb - 1), 0))]


def _kp_conv_call(kern, name, x2d, seq_len, mods, g, w, b, cw, cb, out_widths):
    n_rows, d = x2d.shape
    tm = min(512, seq_len)
    tiles_per_seq = seq_len // tm
    scale, shift = mods
    g2 = g.reshape(1, 1, d)
    in_specs = _halo_specs(n_rows, tm, d) + [
        _mod_spec(g2, tiles_per_seq), _mod_spec(scale, tiles_per_seq), _mod_spec(shift, tiles_per_seq),
        _full_spec(w), _full_spec(b), _full_spec(cw), _full_spec(cb)]
    return pl.pallas_call(
        functools.partial(kern, tm=tm, tiles_per_seq=tiles_per_seq),
        grid=(n_rows // tm,),
        in_specs=in_specs,
        out_specs=[pl.BlockSpec((tm, wd), lambda i: (i, 0)) for wd in out_widths],
        out_shape=[jax.ShapeDtypeStruct((n_rows, wd), BF16) for wd in out_widths],
        compiler_params=_cparams(("parallel",)),
        name=name,
    )(x2d, x2d, x2d, g2, scale, shift, w, b, cw, cb)


def _kp3_call(x2d, seq_len, mods, g, lw, tabs):
    n_rows, d = x2d.shape
    tm = min(512, seq_len)
    tiles_per_seq = seq_len // tm
    scale, shift = mods
    g2 = g.reshape(1, 1, d)
    cosq, sinq, cosk, sink = tabs
    consts = [lw['w3'], lw['b3'], lw['qg'], lw['kvg'], lw['wq'], lw['wqs'], lw['wk'], lw['wv'], lw['e2']]

    def tab_spec(t):
        return pl.BlockSpec((tm, t.shape[1]), lambda i: (i % tiles_per_seq, 0))

    in_specs = [pl.BlockSpec((tm, d), lambda i: (i, 0)),
                _mod_spec(g2, tiles_per_seq), _mod_spec(scale, tiles_per_seq), _mod_spec(shift, tiles_per_seq)]
    in_specs += [_full_spec(a) for a in consts]
    in_specs += [tab_spec(t) for t in (cosq, sinq, cosk, sink)]
    hq = N_HEADS * HEAD_PAD
    return pl.pallas_call(
        _kp3_kernel,
        grid=(n_rows // tm,),
        in_specs=in_specs,
        out_specs=[pl.BlockSpec((tm, hq), lambda i: (i, 0)),
                   pl.BlockSpec((tm, hq), lambda i: (i, 0)),
                   pl.BlockSpec((tm, W_MLA), lambda i: (i, 0))],
        out_shape=[jax.ShapeDtypeStruct((n_rows, hq), BF16),
                   jax.ShapeDtypeStruct((n_rows, hq), BF16),
                   jax.ShapeDtypeStruct((n_rows, W_MLA), BF16)],
        compiler_params=_cparams(("parallel",)),
        name="kp3_mla",
    )(x2d, g2, scale, shift, *consts, cosq, sinq, cosk, sink)


def _attn_kernel(q_ref, k_ref, v_ref, o_ref, m_ref, l_ref, acc_ref):
    ki = pl.program_id(3)

    @pl.when(ki == 0)
    def _():
        m_ref[...] = jnp.full(m_ref.shape, -jnp.inf, F32)
        l_ref[...] = jnp.zeros(l_ref.shape, F32)
        acc_ref[...] = jnp.zeros(acc_ref.shape, F32)

    v = v_ref[...]
    low = lax.broadcasted_iota(jnp.int32, (1, 2 * V_HEAD), 1) < V_HEAD
    alphas, pvs = [], []
    for j in range(2):
        q = q_ref[:, j * HEAD_PAD:(j + 1) * HEAD_PAD]
        k = k_ref[:, j * HEAD_PAD:(j + 1) * HEAD_PAD]
        s = lax.dot_general(q, k, (((1,), (1,)), ((), ())), preferred_element_type=F32)
        m_prev = m_ref[j]
        m_new = jnp.maximum(m_prev, jnp.max(s, axis=-1, keepdims=True))
        alpha = jnp.exp(m_prev - m_new)
        p = jnp.exp(s - m_new)
        l_ref[j] = alpha * l_ref[j] + jnp.sum(p, axis=-1, keepdims=True)
        m_ref[j] = m_new
        alphas.append(alpha)
        pvs.append(jnp.dot(p.astype(BF16), v, preferred_element_type=F32))
    acc_ref[...] = acc_ref[...] * jnp.where(low, alphas[0], alphas[1]) + jnp.where(low, pvs[0], pvs[1])

    @pl.when(ki == pl.num_programs(3) - 1)
    def _():
        o_ref[...] = (acc_ref[...] / jnp.where(low, l_ref[0], l_ref[1])).astype(o_ref.dtype)


def _kv_tile(lk):
    best = LANE
    for t in range(LANE, min(lk, 1536) + 1, LANE):
        if lk % t == 0:
            best = t
    return best


def _attention(q, k, v):
    b, lq, _ = q.shape
    lk = k.shape[1]
    tq = min(512, lq)
    tk = _kv_tile(lk)
    grid = (b, N_HEADS // 2, lq // tq, lk // tk)
    return pl.pallas_call(
        _attn_kernel,
        grid=grid,
        in_specs=[pl.BlockSpec((None, tq, 2 * HEAD_PAD), lambda bi, h, qi, ki: (bi, qi, h)),
                  pl.BlockSpec((None, tk, 2 * HEAD_PAD), lambda bi, h, qi, ki: (bi, ki, h)),
                  pl.BlockSpec((None, tk, 2 * V_HEAD), lambda bi, h, qi, ki: (bi, ki, h))],
        out_specs=pl.BlockSpec((None, tq, 2 * V_HEAD), lambda bi, h, qi, ki: (bi, qi, h)),
        out_shape=jax.ShapeDtypeStruct((b, lq, W_MLA), BF16),
        scratch_shapes=[pltpu.VMEM((2, tq, 1), F32), pltpu.VMEM((2, tq, 1), F32),
                        pltpu.VMEM((tq, 2 * V_HEAD), F32)],
        compiler_params=_cparams(("parallel", "parallel", "parallel", "arbitrary")),
        name="mla_attention",
    )(q, k, v)


def _filt_kernel(z_ref, w1_ref, b1_ref, w2_ref, b2_ref, fr_ref, w3_ref, dl_ref, k_ref, s_ref,
                 *, tmf, seq_len):
    i = pl.program_id(1)
    hp = lax.Precision.HIGHEST
    z = z_ref[...]
    fr = fr_ref[...]
    hid = jnp.sin(fr * (jnp.dot(z, w1_ref[...], preferred_element_type=F32, precision=hp) + b1_ref[...]))
    hid = jnp.sin(fr * (jnp.dot(hid, w2_ref[...], preferred_element_type=F32, precision=hp) + b2_ref[...]))
    h = jnp.dot(hid, w3_ref[...], preferred_element_type=F32, precision=hp)
    t = z[:, 0:1]
    h = h * jnp.exp(-t * dl_ref[...])

    @pl.when(i == 0)
    def _():
        s_ref[...] = jnp.zeros(s_ref.shape, F32)

    s_ref[...] += jnp.sum(jnp.abs(h), axis=0, keepdims=True)
    row = i * tmf + lax.broadcasted_iota(jnp.int32, (tmf, 1), 0)
    k_ref[...] = jnp.where(row == seq_len, 0.0, h)


def _filter_tables(seq_len):
    t = jnp.linspace(0.0, 1.0, seq_len, dtype=F32)[:, None]
    w = 2.0 * math.pi * jnp.arange(seq_len, dtype=F32)[:, None] / seq_len
    f = jnp.linspace(1e-4, HYENA_BANDS - 1, HYENA_BANDS, dtype=F32)[None, :]
    z = jnp.concatenate([t, jnp.cos(f * w), -jnp.sin(f * w)], axis=-1)
    idx = jnp.concatenate([jnp.arange(seq_len), jnp.zeros((1,), jnp.int32),
                           seq_len - jnp.arange(1, seq_len)])
    zz = z[idx]
    return jnp.pad(zz, ((0, 0), (0, LANE - HYENA_EMB)))


def _hyena_deltas():
    max_decay = math.log(HYENA_TARGET) / HYENA_FAST_DECAY
    min_decay = math.log(HYENA_TARGET) / HYENA_SLOW_DECAY
    deltas = jnp.abs(jnp.linspace(min_decay, max_decay, W_HYENA, dtype=F32))
    return jnp.tile(deltas, HYENA_ORDER)[None, :]


def _filter_gen(seq_len, n_layers, hy_w1, hy_b1, hy_w2, hy_b2, hy_w3, hy_freq):
    zz = _filter_tables(seq_len)
    tmf = min(512, seq_len)
    tiles_half = seq_len // tmf
    wc = HYENA_ORDER * W_HYENA
    w1p = jnp.pad(hy_w1[:n_layers], ((0, 0), (0, LANE - HYENA_EMB), (0, 0)))
    w3r = hy_w3[:n_layers].reshape(n_layers, HYENA_HID, 2, wc).transpose(0, 2, 1, 3)
    r1 = lambda a: a[:n_layers].reshape(n_layers, 1, HYENA_HID)
    return pl.pallas_call(
        functools.partial(_filt_kernel, tmf=tmf, seq_len=seq_len),
        grid=(n_layers, 2 * tiles_half),
        in_specs=[pl.BlockSpec((tmf, LANE), lambda l, i: (i, 0)),
                  pl.BlockSpec((None, LANE, HYENA_HID), lambda l, i: (l, 0, 0)),
                  pl.BlockSpec((None, 1, HYENA_HID), lambda l, i: (l, 0, 0)),
                  pl.BlockSpec((None, HYENA_HID, HYENA_HID), lambda l, i: (l, 0, 0)),
                  pl.BlockSpec((None, 1, HYENA_HID), lambda l, i: (l, 0, 0)),
                  pl.BlockSpec((None, 1, HYENA_HID), lambda l, i: (l, 0, 0)),
                  pl.BlockSpec((None, None, HYENA_HID, wc), lambda l, i: (l, i // tiles_half, 0, 0)),
                  pl.BlockSpec((1, wc), lambda l, i: (0, 0))],
        out_specs=[pl.BlockSpec((tmf, wc), lambda l, i: (i, l)),
                   pl.BlockSpec((None, 1, wc), lambda l, i: (l, 0, 0))],
        out_shape=[jax.ShapeDtypeStruct((2 * seq_len, n_layers * wc), F32),
                   jax.ShapeDtypeStruct((n_layers, 1, wc), F32)],
        compiler_params=_cparams(("parallel", "arbitrary")),
        name="hyena_filter",
    )(zz, w1p, r1(hy_b1), hy_w2[:n_layers], r1(hy_b2), r1(hy_freq), w3r, _hyena_deltas())


def _angles(a, b, n):
    m = (a * b) % n
    return m.astype(F32) * (2.0 * math.pi / n)


def _dft_tables(nh):
    n1 = 2 * nh
    n = n1 * N2
    k1 = jnp.arange(n1, dtype=jnp.int32)
    th = _angles(k1[:, None], jnp.arange(nh, dtype=jnp.int32)[None, :], n1)
    c, s = jnp.cos(th), jnp.sin(th)
    w1_pair = jnp.concatenate([jnp.concatenate([c, s], 1), jnp.concatenate([-s, c], 1)], 0)
    thf = _angles(k1[:, None], k1[None, :], n1)
    w1_real = jnp.concatenate([jnp.cos(thf), -jnp.sin(thf)], 0)
    g1 = jnp.concatenate([jnp.concatenate([c.T, -s.T], 1), jnp.concatenate([s.T, c.T], 1)], 0) / n
    n2 = jnp.arange(N2, dtype=jnp.int32)
    kk = k1[:, None, None] + n1 * n2[None, :, None]
    ph = _angles(kk, n2[None, None, :], n)
    cp, sp = jnp.cos(ph), jnp.sin(ph)
    f2 = jnp.concatenate([jnp.concatenate([cp, sp], 2), jnp.concatenate([-sp, cp], 2)], 1)
    cpt, spt = cp.transpose(0, 2, 1), sp.transpose(0, 2, 1)
    g2 = jnp.concatenate([jnp.concatenate([cpt, -spt], 2), jnp.concatenate([spt, cpt], 2)], 1)
    return dict(w1_pair=w1_pair.astype(BF16), w1_real=w1_real.astype(BF16), g1=g1.astype(BF16),
                f2=f2.astype(BF16), g2=g2.astype(BF16))


def _strided_rows(ref, start, size):
    parts = [ref[cc, pl.ds(start, size, stride=2 * SUBLANE), :] for cc in range(ref.shape[0])]
    return parts[0] if len(parts) == 1 else jnp.concatenate(parts, axis=1)


def _store_chunks(ref, row0, val):
    for cc in range(ref.shape[0]):
        ref[cc, pl.ds(row0, SUBLANE), :] = val[:, cc * LANE:(cc + 1) * LANE]


def _stage_a_kernel(*refs, n2t, n1, has_inv, has_fwd, has_div):
    it = iter(refs)
    y_ref = next(it) if has_inv else None
    g1_ref = next(it) if has_inv else None
    u_ref = next(it)
    mul_ref = next(it) if has_inv else None
    skip_ref = next(it) if has_inv else None
    div_ref = next(it) if has_div else None
    w1_ref = next(it) if has_fwd else None
    e_ref = next(it) if has_inv else None
    a_ref = next(it) if has_fwd else None
    for j in range(n2t):
        u = u_ref[j]
        if has_inv:
            nb, jl = j // SUBLANE, j % SUBLANE
            yre = _strided_rows(y_ref.at[nb], jl, n1)
            yim = _strided_rows(y_ref.at[nb], SUBLANE + jl, n1)
            rhs = jnp.concatenate([yre, yim], axis=0).astype(BF16)
            yy = jnp.dot(g1_ref[...], rhs, preferred_element_type=F32)
            uf = u.astype(F32)
            e = mul_ref[j].astype(F32) * (yy + uf * skip_ref[...])
            e_ref[j] = e.astype(e_ref.dtype)
            src = e.astype(BF16)
        elif has_div:
            src = (u / div_ref[...]).astype(BF16)
        else:
            src = u.astype(BF16)
        if has_fwd:
            a = jnp.dot(w1_ref[...], src, preferred_element_type=F32)
            for ri in range(2):
                for kt in range(n1 // SUBLANE):
                    _store_chunks(a_ref.at[kt], (j * 2 + ri) * SUBLANE,
                                  a[ri * n1 + kt * SUBLANE: ri * n1 + (kt + 1) * SUBLANE])


def _stage_a(tabs, nh, u, *, y=None, mul=None, skip=None, div=None, w1=None, want_fwd=True,
             ct=256, n2t=16, e_dtype=BF16, c_off=0):
    p, _, rows, c = u.shape
    n1 = 2 * nh
    has_inv = y is not None
    has_div = div is not None
    ct = min(ct, c)
    nbk = n2t // SUBLANE
    grid = (p, c // ct, N2 // n2t)
    args, specs = [], []
    if has_inv:
        args += [y, tabs['g1']]
        specs += [pl.BlockSpec((None, nbk, ct // LANE, n1 * 2 * SUBLANE, LANE),
                               lambda pi, ci, ni: (pi, ni, ci, 0, 0)),
                  pl.BlockSpec(tabs['g1'].shape, lambda pi, ci, ni: (0, 0))]
    args.append(u)
    specs.append(pl.BlockSpec((None, n2t, rows, ct), lambda pi, ci, ni: (pi, ni, 0, ci)))
    if has_inv:
        args += [mul, skip]
        specs += [pl.BlockSpec((None, n2t, rows, ct), lambda pi, ci, ni: (pi, ni, 0, ci)),
                  pl.BlockSpec((1, ct), lambda pi, ci, ni: (0, ci + c_off // ct))]
    if has_div:
        args.append(div)
        specs.append(pl.BlockSpec((1, ct), lambda pi, ci, ni: (0, ci)))
    if want_fwd:
        args.append(w1)
        specs.append(pl.BlockSpec(w1.shape, lambda pi, ci, ni: (0, 0)))
    out_shape, out_specs = [], []
    if has_inv:
        out_shape.append(jax.ShapeDtypeStruct(u.shape, e_dtype))
        out_specs.append(pl.BlockSpec((None, n2t, rows, ct), lambda pi, ci, ni: (pi, ni, 0, ci)))
    if want_fwd:
        out_shape.append(jax.ShapeDtypeStruct((p, n1 // SUBLANE, c // LANE, N2 * 2 * SUBLANE, LANE), F32))
        out_specs.append(pl.BlockSpec((None, n1 // SUBLANE, ct // LANE, n2t * 2 * SUBLANE, LANE),
                                      lambda pi, ci, ni: (pi, 0, ci, ni, 0)))
    outs = pl.pallas_call(
        functools.partial(_stage_a_kernel, n2t=n2t, n1=n1, has_inv=has_inv, has_fwd=want_fwd,
                          has_div=has_div),
        grid=grid, in_specs=specs, out_specs=out_specs, out_shape=out_shape,
        compiler_params=_cparams(("parallel", "parallel", "parallel")),
        name="hyena_stage_a" + ("_inv" if has_inv else "") + ("_fwd" if want_fwd else ""),
    )(*args)
    return outs


def _stage_b_kernel(*refs, filt_only):
    if filt_only:
        a_ref, f2_ref, o_ref = refs
    else:
        a_ref, f2_ref, g2_ref, kf_ref, o_ref = refs
    for kl in range(SUBLANE):
        are = _strided_rows(a_ref, kl, N2)
        aim = _strided_rows(a_ref, SUBLANE + kl, N2)
        rhs = jnp.concatenate([are, aim], axis=0).astype(BF16)
        t = jnp.dot(f2_ref[kl], rhs, preferred_element_type=F32)
        if filt_only:
            o_ref[kl] = t
            continue
        kf = kf_ref[kl]
        tre, tim = t[:N2], t[N2:]
        kre, kim = kf[:N2], kf[N2:]
        z = jnp.concatenate([tre * kre - tim * kim, tre * kim + tim * kre], axis=0).astype(BF16)
        y = jnp.dot(g2_ref[kl], z, preferred_element_type=F32)
        for ri in range(2):
            for nt in range(N2 // SUBLANE):
                _store_chunks(o_ref.at[nt], (kl * 2 + ri) * SUBLANE,
                              y[ri * N2 + nt * SUBLANE: ri * N2 + (nt + 1) * SUBLANE])


def _stage_b(tabs, a, kf=None, *, c_off=0, ct=512):
    p, nkt, ncc, _, _ = a.shape
    c = ncc * LANE
    n1 = nkt * SUBLANE
    ct = min(ct, c)
    filt_only = kf is None
    grid = (nkt, c // ct, p)
    a_spec = pl.BlockSpec((None, None, ct // LANE, N2 * 2 * SUBLANE, LANE),
                          lambda kt, ci, pi: (pi, kt, ci, 0, 0))
    tab_spec = pl.BlockSpec((SUBLANE, 2 * N2, 2 * N2), lambda kt, ci, pi: (kt, 0, 0))
    if filt_only:
        return pl.pallas_call(
            functools.partial(_stage_b_kernel, filt_only=True),
            grid=grid, in_specs=[a_spec, tab_spec],
            out_specs=pl.BlockSpec((SUBLANE, 2 * N2, ct), lambda kt, ci, pi: (kt, 0, ci)),
            out_shape=jax.ShapeDtypeStruct((n1, 2 * N2, c), F32),
            compiler_params=_cparams(("parallel", "parallel", "parallel")),
            name="hyena_stage_b_filter",
        )(a, tabs['f2'])
    return pl.pallas_call(
        functools.partial(_stage_b_kernel, filt_only=False),
        grid=grid,
        in_specs=[a_spec, tab_spec, tab_spec,
                  pl.BlockSpec((SUBLANE, 2 * N2, ct), lambda kt, ci, pi: (kt, 0, ci + c_off // ct))],
        out_specs=pl.BlockSpec((None, N2 // SUBLANE, ct // LANE, 2 * SUBLANE * SUBLANE, LANE),
                               lambda kt, ci, pi: (pi, 0, ci, kt, 0)),
        out_shape=jax.ShapeDtypeStruct((p, N2 // SUBLANE, c // LANE, n1 * 2 * SUBLANE, LANE), F32),
        compiler_params=_cparams(("parallel", "parallel", "parallel")),
        name="hyena_stage_b",
    )(a, tabs['f2'], tabs['g2'], kf)


def _permute_seq(a, nh):
    b, _, c = a.shape
    return a.reshape(b // 2, 2, nh, N2, c).transpose(0, 3, 1, 2, 4).reshape(b // 2, N2, 2 * nh, c)


def _unpermute_seq(a, nh):
    p, _, _, c = a.shape
    return a.reshape(p, N2, 2, nh, c).transpose(0, 2, 3, 1, 4).reshape(2 * p, nh * N2, c)


def _filter_spectrum(tabs, nh, k_un, ssum):
    n, call = k_un.shape
    kp = k_un.reshape(1, 2 * nh, N2, call).transpose(0, 2, 1, 3)
    (a,) = _stage_a(tabs, nh, kp, div=ssum.reshape(1, call), w1=tabs['w1_real'], ct=512, n2t=16)
    return _stage_b(tabs, a)


def _hyena_long(tabs, nh, kf, layer, v, x1, m2, skip):
    vp, x1p, m2p = (_permute_seq(t, nh) for t in (v, x1, m2))
    c0 = layer * HYENA_ORDER * W_HYENA
    skip2 = skip.reshape(1, HYENA_ORDER * W_HYENA)
    (a1,) = _stage_a(tabs, nh, vp, w1=tabs['w1_pair'])
    y1 = _stage_b(tabs, a1, kf, c_off=c0)
    z, a2 = _stage_a(tabs, nh, vp, y=y1, mul=x1p, skip=skip2, w1=tabs['w1_pair'], c_off=0)
    y2 = _stage_b(tabs, a2, kf, c_off=c0 + W_HYENA)
    (gp,) = _stage_a(tabs, nh, z, y=y2, mul=m2p, skip=skip2, want_fwd=False, c_off=W_HYENA)
    return _unpermute_seq(gp, nh)


def _ctx_conv_kernel(v_ref, x1_ref, m2_ref, k_ref, s_ref, skip_ref, ff_ref, fk_ref, gi_ref, o_ref,
                     *, n):
    hp = lax.Precision.HIGHEST

    def conv(u, o):
        kfull = k_ref[:, o * W_HYENA:(o + 1) * W_HYENA] / s_ref[:, o * W_HYENA:(o + 1) * W_HYENA]
        kf = jnp.dot(fk_ref[...], kfull, preferred_element_type=F32, precision=hp)
        uf = jnp.dot(ff_ref[...], u, preferred_element_type=F32, precision=hp)
        ure, uim, kre, kim = uf[:n], uf[n:], kf[:n], kf[n:]
        z = jnp.concatenate([ure * kre - uim * kim, ure * kim + uim * kre], axis=0)
        y = jnp.dot(gi_ref[...], z, preferred_element_type=F32, precision=hp)
        return y + u * skip_ref[o:o + 1, :]

    v = v_ref[...].astype(F32)
    z1 = x1_ref[...].astype(F32) * conv(v, 0)
    o_ref[...] = (m2_ref[...].astype(F32) * conv(z1, 1)).astype(o_ref.dtype)


def _ctx_tables(lc):
    n = 2 * lc
    k = jnp.arange(n, dtype=jnp.int32)
    ph = _angles(k[:, None], k[None, :], n)
    c, s = jnp.cos(ph), jnp.sin(ph)
    fk = jnp.concatenate([c, -s], axis=0)
    ff = fk[:, :lc]
    gi = jnp.concatenate([c[:lc], -s[:lc]], axis=1) / n
    return ff, fk, gi


def _ctx_hyena(ctabs, k_un, ssum, layer, v, x1, m2, skip):
    b, lc, c = v.shape
    n = 2 * lc
    ff, fk, gi = ctabs
    wc = HYENA_ORDER * W_HYENA
    tok = pl.BlockSpec((None, lc, c), lambda bi: (bi, 0, 0))
    return pl.pallas_call(
        functools.partial(_ctx_conv_kernel, n=n),
        grid=(b,),
        in_specs=[tok, tok, tok,
                  pl.BlockSpec((n, wc), lambda bi: (0, layer)),
                  pl.BlockSpec((None, 1, wc), lambda bi: (layer, 0, 0)),
                  pl.BlockSpec((HYENA_ORDER, c), lambda bi: (0, 0)),
                  _full_spec(ff), _full_spec(fk), _full_spec(gi)],
        out_specs=tok,
        out_shape=jax.ShapeDtypeStruct((b, lc, c), BF16),
        compiler_params=_cparams(("parallel",)),
        name="ctx_hyena",
    )(v, x1, m2, k_un, ssum, skip, ff, fk, gi)


def _merge_kernel(x_ref, a_ref, gh_ref, att_ref, sm_ref, g_ref, sc_ref, sh_ref, gt_ref,
                  wg_ref, bg_ref, sco_ref, hyo_ref, mlo_ref, wo_ref, fg_ref, o_ref, *, final):
    x = x_ref[...]
    d = x.shape[-1]
    h = _mod_norm(x, g_ref[...], sc_ref[...], sh_ref[...])
    gates = jax.nn.sigmoid(jnp.dot(h.astype(BF16), wg_ref[...], preferred_element_type=F32) + bg_ref[...])
    ya = jnp.dot(a_ref[...], sco_ref[...], preferred_element_type=F32)
    yh = jnp.dot(gh_ref[...], hyo_ref[...], preferred_element_type=F32)
    am = (sm_ref[...].astype(F32) * att_ref[...].astype(F32)).astype(BF16)
    ym = jnp.dot(am, mlo_ref[...], preferred_element_type=F32)
    y = gates[:, 0:d] * ya + gates[:, d:2 * d] * yh + gates[:, 2 * d:3 * d] * ym
    o = jnp.dot(y.astype(BF16), wo_ref[...], preferred_element_type=F32)
    xn = x + gt_ref[...] * o
    if final:
        xn = _rms(xn, fg_ref[...])
    o_ref[...] = xn


def _merge_call(x2d, seq_len, mods, g, lw, a, gh, att, sm, final_g, final):
    n_rows, d = x2d.shape
    tm = min(256, seq_len)
    tiles_per_seq = seq_len // tm
    scale, shift, gate = mods
    g2 = g.reshape(1, 1, d)
    fg = final_g.reshape(1, d)
    tok = lambda wd: pl.BlockSpec((tm, wd), lambda i: (i, 0))
    consts = [lw['wg'], lw['bg'], lw['sc_out'], lw['hy_out'], lw['mla_out'], lw['w_o'], fg]
    in_specs = [tok(d), tok(W_CONV), tok(W_HYENA), tok(W_MLA), tok(W_MLA),
                _mod_spec(g2, tiles_per_seq), _mod_spec(scale, tiles_per_seq),
                _mod_spec(shift, tiles_per_seq), _mod_spec(gate, tiles_per_seq)]
    in_specs += [_full_spec(c) for c in consts]
    return pl.pallas_call(
        functools.partial(_merge_kernel, final=final),
        grid=(n_rows // tm,),
        in_specs=in_specs,
        out_specs=tok(d),
        out_shape=jax.ShapeDtypeStruct((n_rows, d), F32),
        compiler_params=_cparams(("parallel",)),
        name="merge_final" if final else "merge",
    )(x2d, a, gh, att, sm, g2, scale, shift, gate, *consts)


def _rope_swap_cols(w):
    half = QK_ROPE // 2
    return jnp.concatenate([-w[..., half:], w[..., :half]], axis=-1)


def _layer_weights(i, w_in, b_in, mla_q_norm, mla_w_uq, mla_kv_norm, mla_w_ukv, sc_out, hy_out,
                   mla_out, w_o):
    wi, bi = w_in[i], b_in[i]
    d = wi.shape[0]
    lw = {}
    lw['w1'] = jnp.concatenate([wi[:, O_XIN:O_HPROJ], wi[:, O_ZM:O_GATES]], axis=1).astype(BF16)
    lw['b1'] = jnp.concatenate([bi[O_XIN:O_HPROJ], bi[O_ZM:O_GATES]])[None, :]
    lw['w2'] = wi[:, O_HPROJ:O_CQ].astype(BF16)
    lw['b2'] = bi[O_HPROJ:O_CQ][None, :]
    wkr, bkr = wi[:, O_KR:O_ZM], bi[O_KR:O_ZM]
    zw = lambda n: jnp.zeros((d, n), F32)
    zb = lambda n: jnp.zeros((n,), F32)
    lw['w3'] = jnp.concatenate([wi[:, O_CQ:O_KR], wkr, zw(96), _rope_swap_cols(wkr), zw(96)], axis=1).astype(BF16)
    lw['b3'] = jnp.concatenate([bi[O_CQ:O_KR], bkr, zb(96), _rope_swap_cols(bkr), zb(96)])[None, :]
    lw['qg'] = mla_q_norm[i][None, :]
    lw['kvg'] = mla_kv_norm[i][None, :]
    wuq = mla_w_uq[i].reshape(Q_LORA, N_HEADS, QK_NOPE + QK_ROPE)
    pad = jnp.zeros((Q_LORA, N_HEADS, HEAD_PAD - QK_NOPE - QK_ROPE), F32)
    lw['wq'] = jnp.concatenate([wuq, pad], axis=-1).reshape(Q_LORA, N_HEADS * HEAD_PAD).astype(BF16)
    zn = jnp.zeros((Q_LORA, N_HEADS, QK_NOPE), F32)
    lw['wqs'] = jnp.concatenate([zn, _rope_swap_cols(wuq[..., QK_NOPE:]), pad], axis=-1).reshape(
        Q_LORA, N_HEADS * HEAD_PAD).astype(BF16)
    wukv = mla_w_ukv[i].reshape(KV_LORA, N_HEADS, QK_NOPE + V_HEAD)
    padk = jnp.zeros((KV_LORA, N_HEADS, HEAD_PAD - QK_NOPE), F32)
    lw['wk'] = jnp.concatenate([wukv[..., :QK_NOPE], padk], axis=-1).reshape(
        KV_LORA, N_HEADS * HEAD_PAD).astype(BF16)
    lw['wv'] = wukv[..., QK_NOPE:].reshape(KV_LORA, W_MLA).astype(BF16)
    e2 = jnp.zeros((QK_ROPE, N_HEADS, HEAD_PAD), F32).at[:, :, QK_NOPE:QK_NOPE + QK_ROPE].set(
        jnp.eye(QK_ROPE, dtype=F32)[:, None, :])
    lw['e2'] = e2.reshape(QK_ROPE, N_HEADS * HEAD_PAD).astype(BF16)
    lw['wg'] = wi[:, O_GATES:].astype(BF16)
    lw['bg'] = bi[O_GATES:][None, :]
    lw['sc_out'] = sc_out[i].astype(BF16)
    lw['hy_out'] = hy_out[i].astype(BF16)
    lw['mla_out'] = mla_out[i].astype(BF16)
    lw['w_o'] = w_o[i].astype(BF16)
    return lw


def _rope_tables(seq_len, use_rope):
    scale = (QK_NOPE + QK_ROPE) ** -0.5
    if use_rope:
        rows = seq_len // GRID_W
        row = jnp.broadcast_to(jnp.arange(rows, dtype=F32)[:, None], (rows, GRID_W)).reshape(seq_len)
        col = jnp.broadcast_to(jnp.arange(GRID_W, dtype=F32)[None, :], (rows, GRID_W)).reshape(seq_len)
        n_f = QK_ROPE // 4
        inv = ROPE_BASE ** (-jnp.arange(n_f, dtype=F32) / n_f)
        ang = jnp.concatenate([row[:, None] * inv, col[:, None] * inv], axis=-1)
        cos, sin = jnp.cos(ang), jnp.sin(ang)
    else:
        cos = jnp.ones((seq_len, QK_ROPE // 2), F32)
        sin = jnp.zeros((seq_len, QK_ROPE // 2), F32)
    cosk = jnp.concatenate([cos, cos], axis=-1)
    sink = jnp.concatenate([sin, sin], axis=-1)
    ones = jnp.ones((seq_len, QK_NOPE), F32)
    zeros = jnp.zeros((seq_len, QK_NOPE), F32)
    tail = jnp.zeros((seq_len, HEAD_PAD - QK_NOPE - QK_ROPE), F32)
    cosq = jnp.tile(jnp.concatenate([ones, cosk, tail], axis=-1), (1, N_HEADS)) * scale
    sinq = jnp.tile(jnp.concatenate([zeros, sink, tail], axis=-1), (1, N_HEADS)) * scale
    return cosq, sinq, cosk, sink


def kernel(x, c, ctx, c_ctx, ada_w, ada_b, norm_g, w_in, b_in, sc_conv_w, sc_conv_b, sc_out, hy_conv_w,
           hy_conv_b, hy_w1, hy_b1, hy_w2, hy_b2, hy_w3, hy_freq, hy_skip, hy_out, mla_q_norm, mla_w_uq,
           mla_kv_norm, mla_w_ukv, mla_out, w_o, final_g):
    bsz, seq, d = x.shape
    lc = ctx.shape[1]
    depth = ada_w.shape[0]
    nh = seq // N2
    assert d == D_MODEL and bsz % 2 == 0 and nh % SUBLANE == 0 and bsz <= 7

    cc = jnp.zeros((8, d), F32).at[:bsz].set(c).at[bsz].set(c_ctx)
    mods = _ada_mods(cc, ada_w, ada_b)

    tabs = _dft_tables(nh)
    k_un, ssum = _filter_gen(seq, depth, hy_w1, hy_b1, hy_w2, hy_b2, hy_w3, hy_freq)
    kf = _filter_spectrum(tabs, nh, k_un, ssum)
    if depth > 1:
        kc_un, sc_sum = _filter_gen(lc, depth - 1, hy_w1, hy_b1, hy_w2, hy_b2, hy_w3, hy_freq)
        ctabs = _ctx_tables(lc)

    rope_l = _rope_tables(seq, True)
    rope_c = _rope_tables(lc, False)

    x_lat = x.reshape(bsz * seq, d)
    x_ctx = ctx.reshape(bsz * lc, d)
    for i in range(depth):
        last = i == depth - 1
        lw = _layer_weights(i, w_in, b_in, mla_q_norm, mla_w_uq, mla_kv_norm, mla_w_ukv, sc_out,
                            hy_out, mla_out, w_o)
        m = mods[i]
        split = lambda r: tuple(r[:, None, j * d:(j + 1) * d] for j in range(3))
        shift_l, scale_l, gate_l = split(m[:bsz])
        shift_c, scale_c, gate_c = split(m[bsz:bsz + 1])
        g = norm_g[i]
        scw, scb = sc_conv_w[i], sc_conv_b[i][None, :]
        hcw, hcb = hy_conv_w[i], hy_conv_b[i][None, :]

        q_c, k_c, v_c = _kp3_call(x_ctx, lc, (scale_c, shift_c), g, lw, rope_c)
        q_l, k_l, v_l = _kp3_call(x_lat, seq, (scale_l, shift_l), g, lw, rope_l)
        hq = N_HEADS * HEAD_PAD
        k_all = jnp.concatenate([k_c.reshape(bsz, lc, hq), k_l.reshape(bsz, seq, hq)], axis=1)
        v_all = jnp.concatenate([v_c.reshape(bsz, lc, W_MLA), v_l.reshape(bsz, seq, W_MLA)], axis=1)
        att_l = _attention(q_l.reshape(bsz, seq, hq), k_all, v_all).reshape(bsz * seq, W_MLA)

        a_l, sm_l = _kp_conv_call(_kp1_kernel, "kp1_conv", x_lat, seq, (scale_l, shift_l), g,
                                  lw['w1'], lw['b1'], scw, scb, (W_CONV, W_MLA))
        v_h, x1_h, m2_h = _kp_conv_call(_kp2_kernel, "kp2_hyena", x_lat, seq, (scale_l, shift_l), g,
                                        lw['w2'], lw['b2'], hcw, hcb, (W_HYENA,) * 3)
        r3 = lambda t: t.reshape(bsz, seq, W_HYENA)
        gh_l = _hyena_long(tabs, nh, kf, i, r3(v_h), r3(x1_h), r3(m2_h), hy_skip[i])
        new_lat = _merge_call(x_lat, seq, (scale_l, shift_l, gate_l), g, lw, a_l,
                              gh_l.reshape(bsz * seq, W_HYENA), att_l, sm_l, final_g, last)
        if not last:
            att_c = _attention(q_c.reshape(bsz, lc, hq), k_c.reshape(bsz, lc, hq),
                               v_c.reshape(bsz, lc, W_MLA)).reshape(bsz * lc, W_MLA)
            a_c, sm_c = _kp_conv_call(_kp1_kernel, "kp1_conv_ctx", x_ctx, lc, (scale_c, shift_c), g,
                                      lw['w1'], lw['b1'], scw, scb, (W_CONV, W_MLA))
            v_hc, x1_hc, m2_hc = _kp_conv_call(_kp2_kernel, "kp2_hyena_ctx", x_ctx, lc,
                                               (scale_c, shift_c), g, lw['w2'], lw['b2'], hcw, hcb,
                                               (W_HYENA,) * 3)
            rc = lambda t: t.reshape(bsz, lc, W_HYENA)
            gh_c = _ctx_hyena(ctabs, kc_un, sc_sum, i, rc(v_hc), rc(x1_hc), rc(m2_hc), hy_skip[i])
            x_ctx = _merge_call(x_ctx, lc, (scale_c, shift_c, gate_c), g, lw, a_c,
                                gh_c.reshape(bsz * lc, W_HYENA), att_c, sm_c, final_g, False)
        x_lat = new_lat
    return x_lat.reshape(bsz, seq, d)
```

```python
import functools
import math

import jax
import jax.numpy as jnp
from jax import lax
from jax.experimental import pallas as pl
from jax.experimental.pallas import tpu as pltpu

F32 = jnp.float32
BF16 = jnp.bfloat16

D_MODEL = 1024
DEPTH = 4
GRID_W = 64
W_CONV = 512
W_HYENA = 512
HYENA_ORDER = 2
HYENA_EMB = 33
HYENA_BANDS = (HYENA_EMB - 1) // 2
HYENA_HID = 64
HYENA_FAST_DECAY = 0.3
HYENA_SLOW_DECAY = 1.5
HYENA_TARGET = 1e-2
N_HEADS = 8
QK_NOPE = 64
QK_ROPE = 32
V_HEAD = 64
Q_LORA = 384
KV_LORA = 256
W_MLA = N_HEADS * V_HEAD
ROPE_BASE = 10000.0
N_BRANCH = 3
EPS = 1e-6

O_XIN, O_GB, O_GC, O_ZA = 0, 512, 1024, 1536
O_HPROJ, O_ZH = 2048, 3584
O_CQ, O_CKV, O_KR, O_ZM, O_GATES = 4096, 4480, 4736, 4768, 5280
N_IN = 8352

LANE = 128
SUBLANE = 8
HEAD_PAD = 128
N2 = 128
VMEM_LIMIT = 56 * 1024 * 1024


def _cparams(sem):
    return pltpu.CompilerParams(dimension_semantics=sem, vmem_limit_bytes=VMEM_LIMIT)


def _ada_kernel(c_ref, w_ref, b_ref, o_ref):
    c = c_ref[...]
    s = c * jax.nn.sigmoid(c)
    o_ref[...] = jnp.dot(s, w_ref[...], preferred_element_type=F32,
                         precision=lax.Precision.HIGHEST) + b_ref[...]


def _ada_mods(cc, ada_w, ada_b):
    depth = ada_w.shape[0]
    d = cc.shape[1]
    return pl.pallas_call(
        _ada_kernel,
        grid=(depth, 3),
        in_specs=[pl.BlockSpec((8, d), lambda l, j: (0, 0)),
                  pl.BlockSpec((None, d, d), lambda l, j: (l, 0, j)),
                  pl.BlockSpec((None, 1, d), lambda l, j: (l, 0, j))],
        out_specs=pl.BlockSpec((None, 8, d), lambda l, j: (l, 0, j)),
        out_shape=jax.ShapeDtypeStruct((depth, 8, 3 * d), F32),
        compiler_params=_cparams(("parallel", "parallel")),
        name="ada_mods",
    )(cc, ada_w, ada_b.reshape(depth, 1, 3 * d))


def _mod_norm(x, g, scale, shift):
    y = x * lax.rsqrt(jnp.mean(x * x, axis=-1, keepdims=True) + EPS)
    return (y * g) * (1.0 + scale) + shift


def _rms(x, g):
    return x * lax.rsqrt(jnp.mean(x * x, axis=-1, keepdims=True) + EPS) * g


def _silu(x):
    return x * jax.nn.sigmoid(x)


def _dwconv3_ext(u_ext, w, b, first, last, tm):
    n = tm + 2 * SUBLANE
    prev = pltpu.roll(u_ext, 1, 0)[SUBLANE:SUBLANE + tm]
    nxt = pltpu.roll(u_ext, n - 1, 0)[SUBLANE:SUBLANE + tm]
    row = lax.broadcasted_iota(jnp.int32, (tm, 1), 0)
    prev = jnp.where(jnp.logical_and(first, row == 0), 0.0, prev)
    nxt = jnp.where(jnp.logical_and(last, row == tm - 1), 0.0, nxt)
    return prev * w[0:1] + u_ext[SUBLANE:SUBLANE + tm] * w[1:2] + nxt * w[2:3] + b


def _proj_ext(xm_ref, xp_ref, xn_ref, g_ref, sc_ref, sh_ref, w_ref, b_ref):
    x_ext = jnp.concatenate([xp_ref[...], xm_ref[...], xn_ref[...]], axis=0)
    h = _mod_norm(x_ext, g_ref[...], sc_ref[...], sh_ref[...])
    return jnp.dot(h.astype(BF16), w_ref[...], preferred_element_type=F32) + b_ref[...]


def _kp1_kernel(xm_ref, xp_ref, xn_ref, g_ref, sc_ref, sh_ref, w_ref, b_ref, cw_ref, cb_ref,
                a_ref, sm_ref, *, tm, tiles_per_seq):
    i = pl.program_id(0)
    first = (i % tiles_per_seq) == 0
    last = (i % tiles_per_seq) == tiles_per_seq - 1
    p = _proj_ext(xm_ref, xp_ref, xn_ref, g_ref, sc_ref, sh_ref, w_ref, b_ref)
    prod = p[:, 2 * W_CONV:3 * W_CONV] * p[:, 0:W_CONV]
    conv = _dwconv3_ext(prod, cw_ref[...], cb_ref[...], first, last, tm)
    pm = p[SUBLANE:SUBLANE + tm]
    a = _silu(pm[:, 3 * W_CONV:4 * W_CONV]) * (pm[:, W_CONV:2 * W_CONV] * conv)
    a_ref[...] = a.astype(a_ref.dtype)
    sm_ref[...] = _silu(pm[:, 4 * W_CONV:4 * W_CONV + W_MLA]).astype(sm_ref.dtype)


def _kp2_kernel(xm_ref, xp_ref, xn_ref, g_ref, sc_ref, sh_ref, w_ref, b_ref, cw_ref, cb_ref,
                v_ref, x1_ref, m2_ref, *, tm, tiles_per_seq):
    i = pl.program_id(0)
    first = (i % tiles_per_seq) == 0
    last = (i % tiles_per_seq) == tiles_per_seq - 1
    p = _proj_ext(xm_ref, xp_ref, xn_ref, g_ref, sc_ref, sh_ref, w_ref, b_ref)
    u = _dwconv3_ext(p[:, 0:3 * W_HYENA], cw_ref[...], cb_ref[...], first, last, tm)
    zh = p[SUBLANE:SUBLANE + tm, 3 * W_HYENA:4 * W_HYENA]
    v_ref[...] = u[:, 0:W_HYENA].astype(v_ref.dtype)
    x1_ref[...] = u[:, W_HYENA:2 * W_HYENA].astype(x1_ref.dtype)
    m2_ref[...] = (_silu(zh) * u[:, 2 * W_HYENA:3 * W_HYENA]).astype(m2_ref.dtype)


def _kp3_kernel(xm_ref, g_ref, sc_ref, sh_ref, w_ref, b_ref, qg_ref, kvg_ref, wq_ref, wqs_ref,
                wk_ref, wv_ref, vb_ref, e2_ref, cosq_ref, sinq_ref, cosk_ref, sink_ref,
                q_ref, k_ref, v_ref):
    h = _mod_norm(xm_ref[...], g_ref[...], sc_ref[...], sh_ref[...])
    p = jnp.dot(h.astype(BF16), w_ref[...], preferred_element_type=F32) + b_ref[...]
    cq = _rms(p[:, 0:Q_LORA], qg_ref[...]).astype(BF16)
    ckv = _rms(p[:, Q_LORA:Q_LORA + KV_LORA], kvg_ref[...]).astype(BF16)
    kr = p[:, 640:640 + QK_ROPE]
    krs = p[:, 768:768 + QK_ROPE]
    qa = jnp.dot(cq, wq_ref[...], preferred_element_type=F32)
    qb = jnp.dot(cq, wqs_ref[...], preferred_element_type=F32)
    q_ref[...] = (qa * cosq_ref[...] + qb * sinq_ref[...]).astype(q_ref.dtype)
    kr_rot = (kr * cosk_ref[...] + krs * sink_ref[...]).astype(BF16)
    kn = jnp.dot(ckv, wk_ref[...], preferred_element_type=F32)
    krp = jnp.dot(kr_rot, e2_ref[...], preferred_element_type=F32)
    k_ref[...] = (kn + krp).astype(k_ref.dtype)
    v_ref[...] = (jnp.dot(ckv, wv_ref[...], preferred_element_type=F32) + vb_ref[...]).astype(v_ref.dtype)


def _mod_spec(mod, tiles_per_seq):
    d = mod.shape[-1]
    if mod.shape[0] == 1:
        return pl.BlockSpec((None, 1, d), lambda i: (0, 0, 0))
    return pl.BlockSpec((None, 1, d), lambda i: (i // tiles_per_seq, 0, 0))


def _full_spec(a):
    nd = a.ndim
    return pl.BlockSpec(a.shape, lambda i: (0,) * nd)


def _halo_specs(n_rows, tm, d):
    r = tm // SUBLANE
    nb = n_rows // SUBLANE
    return [pl.BlockSpec((tm, d), lambda i: (i, 0)),
            pl.BlockSpec((SUBLANE, d), lambda i: (jnp.maximum(i * r - 1, 0), 0)),
            pl.BlockSpec((SUBLANE, d), lambda i: (jnp.minimum((i + 1) * r, nb - 1), 0))]


def _kp_conv_call(kern, name, x2d, seq_len, mods, g, w, b, cw, cb, out_widths):
    n_rows, d = x2d.shape
    tm = min(512, seq_len)
    tiles_per_seq = seq_len // tm
    scale, shift = mods
    g2 = g.reshape(1, 1, d)
    in_specs = _halo_specs(n_rows, tm, d) + [
        _mod_spec(g2, tiles_per_seq), _mod_spec(scale, tiles_per_seq), _mod_spec(shift, tiles_per_seq),
        _full_spec(w), _full_spec(b), _full_spec(cw), _full_spec(cb)]
    return pl.pallas_call(
        functools.partial(kern, tm=tm, tiles_per_seq=tiles_per_seq),
        grid=(n_rows // tm,),
        in_specs=in_specs,
        out_specs=[pl.BlockSpec((tm, wd), lambda i: (i, 0)) for wd in out_widths],
        out_shape=[jax.ShapeDtypeStruct((n_rows, wd), BF16) for wd in out_widths],
        compiler_params=_cparams(("parallel",)),
        name=name,
    )(x2d, x2d, x2d, g2, scale, shift, w, b, cw, cb)


def _kp3_call(x2d, seq_len, mods, g, lw, tabs):
    n_rows, d = x2d.shape
    tm = min(512, seq_len)
    tiles_per_seq = seq_len // tm
    scale, shift = mods
    g2 = g.reshape(1, 1, d)
    cosq, sinq, cosk, sink = tabs
    consts = [lw['w3'], lw['b3'], lw['qg'], lw['kvg'], lw['wq'], lw['wqs'], lw['wk'], lw['wv'], lw['vb'],
              lw['e2']]

    def tab_spec(t):
        return pl.BlockSpec((tm, t.shape[1]), lambda i: (i % tiles_per_seq, 0))

    in_specs = [pl.BlockSpec((tm, d), lambda i: (i, 0)),
                _mod_spec(g2, tiles_per_seq), _mod_spec(scale, tiles_per_seq), _mod_spec(shift, tiles_per_seq)]
    in_specs += [_full_spec(a) for a in consts]
    in_specs += [tab_spec(t) for t in (cosq, sinq, cosk, sink)]
    hq = N_HEADS * HEAD_PAD
    return pl.pallas_call(
        _kp3_kernel,
        grid=(n_rows // tm,),
        in_specs=in_specs,
        out_specs=[pl.BlockSpec((tm, hq), lambda i: (i, 0)),
                   pl.BlockSpec((tm, hq), lambda i: (i, 0)),
                   pl.BlockSpec((tm, hq), lambda i: (i, 0))],
        out_shape=[jax.ShapeDtypeStruct((n_rows, hq), BF16),
                   jax.ShapeDtypeStruct((n_rows, hq), BF16),
                   jax.ShapeDtypeStruct((n_rows, hq), BF16)],
        compiler_params=_cparams(("parallel",)),
        name="kp3_mla",
    )(x2d, g2, scale, shift, *consts, cosq, sinq, cosk, sink)


ATTN_CHAINS = 4


def _attn_kernel(q_ref, k_ref, v_ref, o_ref, s_ref, p_ref, m_ref, acc_ref):
    h = pl.program_id(2)
    ki = pl.program_id(3)

    @pl.when(ki == 0)
    def _():
        m_ref[...] = jnp.full(m_ref.shape, -jnp.inf, F32)
        acc_ref[...] = jnp.zeros(acc_ref.shape, F32)

    hq = q_ref.shape[0] // ATTN_CHAINS
    halves = [pl.ds(c * hq, hq) for c in range(ATTN_CHAINS)]
    for rows in halves:
        s_ref[rows, :] = lax.dot_general(q_ref[rows, :], k_ref[...], (((1,), (1,)), ((), ())),
                                         preferred_element_type=F32)
    alphas = []
    for rows in halves:
        m_prev = m_ref[rows, :]
        m_new = jnp.maximum(m_prev, jnp.max(s_ref[rows, :], axis=-1, keepdims=True))
        alphas.append(jnp.exp2(m_prev - m_new))
        m_ref[rows, :] = m_new
        p_ref[rows, :] = jnp.exp2((s_ref[rows, :] - m_new).astype(BF16))
    for rows, alpha in zip(halves, alphas):
        acc_ref[rows, :] = acc_ref[rows, :] * alpha + jnp.dot(p_ref[rows, :], v_ref[...],
                                                              preferred_element_type=F32)

    @pl.when(ki == pl.num_programs(3) - 1)
    def _():
        acc = acc_ref[...]
        o = (acc[:, 0:V_HEAD] / acc[:, V_HEAD:V_HEAD + 1]).astype(o_ref.dtype)

        @pl.when(h % 2 == 0)
        def _():
            o_ref[:, 0:V_HEAD] = o

        @pl.when(h % 2 == 1)
        def _():
            o_ref[:, V_HEAD:2 * V_HEAD] = o


def _kv_tile(lk):
    best = LANE
    for t in range(LANE, min(lk, 1536) + 1, LANE):
        if lk % t == 0:
            best = t
    return best


def _attention(q, k, v):
    b, lq, _ = q.shape
    lk = k.shape[1]
    tq = min(1024, lq)
    tk = _kv_tile(lk)
    grid = (b, lq // tq, N_HEADS, lk // tk)
    return pl.pallas_call(
        _attn_kernel,
        grid=grid,
        in_specs=[pl.BlockSpec((None, tq, HEAD_PAD), lambda bi, qi, h, ki: (bi, qi, h)),
                  pl.BlockSpec((None, tk, HEAD_PAD), lambda bi, qi, h, ki: (bi, ki, h)),
                  pl.BlockSpec((None, tk, HEAD_PAD), lambda bi, qi, h, ki: (bi, ki, h))],
        out_specs=pl.BlockSpec((None, tq, 2 * V_HEAD), lambda bi, qi, h, ki: (bi, qi, h // 2)),
        out_shape=jax.ShapeDtypeStruct((b, lq, W_MLA), BF16),
        scratch_shapes=[pltpu.VMEM((tq, tk), F32), pltpu.VMEM((tq, tk), BF16),
                        pltpu.VMEM((tq, 1), F32), pltpu.VMEM((tq, HEAD_PAD), F32)],
        compiler_params=_cparams(("parallel", "parallel", "arbitrary", "arbitrary")),
        name="mla_attention",
    )(q, k, v)


def _filt_kernel(z_ref, w1_ref, b1_ref, w2_ref, b2_ref, fr_ref, w3_ref, dl_ref, k_ref, s_ref,
                 *, tmf, seq_len):
    i = pl.program_id(1)
    hp = lax.Precision.HIGHEST
    z = z_ref[...]
    fr = fr_ref[...]
    hid = jnp.sin(fr * (jnp.dot(z, w1_ref[...], preferred_element_type=F32, precision=hp) + b1_ref[...]))
    hid = jnp.sin(fr * (jnp.dot(hid, w2_ref[...], preferred_element_type=F32, precision=hp) + b2_ref[...]))
    h = jnp.dot(hid, w3_ref[...], preferred_element_type=F32, precision=hp)
    t = z[:, 0:1]
    h = h * jnp.exp(-t * dl_ref[...])

    @pl.when(i == 0)
    def _():
        s_ref[...] = jnp.zeros(s_ref.shape, F32)

    s_ref[...] += jnp.sum(jnp.abs(h), axis=0, keepdims=True)
    row = i * tmf + lax.broadcasted_iota(jnp.int32, (tmf, 1), 0)
    k_ref[...] = jnp.where(row == seq_len, 0.0, h)


def _filter_tables(seq_len):
    t = jnp.linspace(0.0, 1.0, seq_len, dtype=F32)[:, None]
    w = 2.0 * math.pi * jnp.arange(seq_len, dtype=F32)[:, None] / seq_len
    f = jnp.linspace(1e-4, HYENA_BANDS - 1, HYENA_BANDS, dtype=F32)[None, :]
    z = jnp.concatenate([t, jnp.cos(f * w), -jnp.sin(f * w)], axis=-1)
    idx = jnp.concatenate([jnp.arange(seq_len), jnp.zeros((1,), jnp.int32),
                           seq_len - jnp.arange(1, seq_len)])
    zz = z[idx]
    return jnp.pad(zz, ((0, 0), (0, LANE - HYENA_EMB)))


def _hyena_deltas():
    max_decay = math.log(HYENA_TARGET) / HYENA_FAST_DECAY
    min_decay = math.log(HYENA_TARGET) / HYENA_SLOW_DECAY
    deltas = jnp.abs(jnp.linspace(min_decay, max_decay, W_HYENA, dtype=F32))
    return jnp.tile(deltas, HYENA_ORDER)[None, :]


def _filter_gen(seq_len, n_layers, hy_w1, hy_b1, hy_w2, hy_b2, hy_w3, hy_freq):
    zz = _filter_tables(seq_len)
    tmf = min(512, seq_len)
    tiles_half = seq_len // tmf
    wc = HYENA_ORDER * W_HYENA
    w1p = jnp.pad(hy_w1[:n_layers], ((0, 0), (0, LANE - HYENA_EMB), (0, 0)))
    w3r = hy_w3[:n_layers].reshape(n_layers, HYENA_HID, 2, wc).transpose(0, 2, 1, 3)
    r1 = lambda a: a[:n_layers].reshape(n_layers, 1, HYENA_HID)
    return pl.pallas_call(
        functools.partial(_filt_kernel, tmf=tmf, seq_len=seq_len),
        grid=(n_layers, 2 * tiles_half),
        in_specs=[pl.BlockSpec((tmf, LANE), lambda l, i: (i, 0)),
                  pl.BlockSpec((None, LANE, HYENA_HID), lambda l, i: (l, 0, 0)),
                  pl.BlockSpec((None, 1, HYENA_HID), lambda l, i: (l, 0, 0)),
                  pl.BlockSpec((None, HYENA_HID, HYENA_HID), lambda l, i: (l, 0, 0)),
                  pl.BlockSpec((None, 1, HYENA_HID), lambda l, i: (l, 0, 0)),
                  pl.BlockSpec((None, 1, HYENA_HID), lambda l, i: (l, 0, 0)),
                  pl.BlockSpec((None, None, HYENA_HID, wc), lambda l, i: (l, i // tiles_half, 0, 0)),
                  pl.BlockSpec((1, wc), lambda l, i: (0, 0))],
        out_specs=[pl.BlockSpec((tmf, wc), lambda l, i: (i, l)),
                   pl.BlockSpec((None, 1, wc), lambda l, i: (l, 0, 0))],
        out_shape=[jax.ShapeDtypeStruct((2 * seq_len, n_layers * wc), F32),
                   jax.ShapeDtypeStruct((n_layers, 1, wc), F32)],
        compiler_params=_cparams(("parallel", "arbitrary")),
        name="hyena_filter",
    )(zz, w1p, r1(hy_b1), hy_w2[:n_layers], r1(hy_b2), r1(hy_freq), w3r, _hyena_deltas())


def _angles(a, b, n):
    m = (a * b) % n
    return m.astype(F32) * (2.0 * math.pi / n)


def _dft_tables(nh):
    n1 = 2 * nh
    n = n1 * N2
    k1 = jnp.arange(n1, dtype=jnp.int32)
    th = _angles(k1[:, None], jnp.arange(nh, dtype=jnp.int32)[None, :], n1)
    c, s = jnp.cos(th), jnp.sin(th)
    w1_pair = jnp.concatenate([jnp.concatenate([c, s], 1), jnp.concatenate([-s, c], 1)], 0)
    thf = _angles(k1[:, None], k1[None, :], n1)
    w1_real = jnp.concatenate([jnp.cos(thf), -jnp.sin(thf)], 0)
    g1 = jnp.concatenate([jnp.concatenate([c.T, -s.T], 1), jnp.concatenate([s.T, c.T], 1)], 0) / n
    n2 = jnp.arange(N2, dtype=jnp.int32)
    kk = k1[:, None, None] + n1 * n2[None, :, None]
    ph = _angles(kk, n2[None, None, :], n)
    cp, sp = jnp.cos(ph), jnp.sin(ph)
    f2 = jnp.concatenate([jnp.concatenate([cp, sp], 2), jnp.concatenate([-sp, cp], 2)], 1)
    cpt, spt = cp.transpose(0, 2, 1), sp.transpose(0, 2, 1)
    g2 = jnp.concatenate([jnp.concatenate([cpt, -spt], 2), jnp.concatenate([spt, cpt], 2)], 1)
    return dict(w1_pair=w1_pair.astype(BF16), w1_real=w1_real.astype(BF16), g1=g1.astype(BF16),
                f2=f2.astype(BF16), g2=g2.astype(BF16))


def _strided_rows(ref, start, size):
    parts = [ref[cc, pl.ds(start, size, stride=2 * SUBLANE), :] for cc in range(ref.shape[0])]
    return parts[0] if len(parts) == 1 else jnp.concatenate(parts, axis=1)


def _store_chunks(ref, row0, val):
    for cc in range(ref.shape[0]):
        ref[cc, pl.ds(row0, SUBLANE), :] = val[:, cc * LANE:(cc + 1) * LANE]


def _stage_a_kernel(*refs, n2t, n1, has_inv, has_fwd, has_div):
    it = iter(refs)
    y_ref = next(it) if has_inv else None
    g1_ref = next(it) if has_inv else None
    u_ref = next(it)
    mul_ref = next(it) if has_inv else None
    skip_ref = next(it) if has_inv else None
    div_ref = next(it) if has_div else None
    w1_ref = next(it) if has_fwd else None
    e_ref = next(it) if has_inv else None
    a_ref = next(it) if has_fwd else None
    for j in range(n2t):
        u = u_ref[j]
        if has_inv:
            nb, jl = j // SUBLANE, j % SUBLANE
            yre = _strided_rows(y_ref.at[nb], jl, n1)
            yim = _strided_rows(y_ref.at[nb], SUBLANE + jl, n1)
            rhs = jnp.concatenate([yre, yim], axis=0).astype(BF16)
            yy = jnp.dot(g1_ref[...], rhs, preferred_element_type=F32)
            uf = u.astype(F32)
            e = mul_ref[j].astype(F32) * (yy + uf * skip_ref[...])
            e_ref[j] = e.astype(e_ref.dtype)
            src = e.astype(BF16)
        elif has_div:
            src = (u / div_ref[...]).astype(BF16)
        else:
            src = u.astype(BF16)
        if has_fwd:
            a = jnp.dot(w1_ref[...], src, preferred_element_type=F32)
            for ri in range(2):
                for kt in range(n1 // SUBLANE):
                    _store_chunks(a_ref.at[kt], (j * 2 + ri) * SUBLANE,
                                  a[ri * n1 + kt * SUBLANE: ri * n1 + (kt + 1) * SUBLANE])


def _stage_a(tabs, nh, u, *, y=None, mul=None, skip=None, div=None, w1=None, want_fwd=True,
             ct=256, n2t=16, e_dtype=BF16, c_off=0):
    p, _, rows, c = u.shape
    n1 = 2 * nh
    has_inv = y is not None
    has_div = div is not None
    ct = min(ct, c)
    nbk = n2t // SUBLANE
    grid = (p, c // ct, N2 // n2t)
    args, specs = [], []
    if has_inv:
        args += [y, tabs['g1']]
        specs += [pl.BlockSpec((None, nbk, ct // LANE, n1 * 2 * SUBLANE, LANE),
                               lambda pi, ci, ni: (pi, ni, ci, 0, 0)),
                  pl.BlockSpec(tabs['g1'].shape, lambda pi, ci, ni: (0, 0))]
    args.append(u)
    specs.append(pl.BlockSpec((None, n2t, rows, ct), lambda pi, ci, ni: (pi, ni, 0, ci)))
    if has_inv:
        args += [mul, skip]
        specs += [pl.BlockSpec((None, n2t, rows, ct), lambda pi, ci, ni: (pi, ni, 0, ci)),
                  pl.BlockSpec((1, ct), lambda pi, ci, ni: (0, ci + c_off // ct))]
    if has_div:
        args.append(div)
        specs.append(pl.BlockSpec((1, ct), lambda pi, ci, ni: (0, ci)))
    if want_fwd:
        args.append(w1)
        specs.append(pl.BlockSpec(w1.shape, lambda pi, ci, ni: (0, 0)))
    out_shape, out_specs = [], []
    if has_inv:
        out_shape.append(jax.ShapeDtypeStruct(u.shape, e_dtype))
        out_specs.append(pl.BlockSpec((None, n2t, rows, ct), lambda pi, ci, ni: (pi, ni, 0, ci)))
    if want_fwd:
        out_shape.append(jax.ShapeDtypeStruct((p, n1 // SUBLANE, c // LANE, N2 * 2 * SUBLANE, LANE), F32))
        out_specs.append(pl.BlockSpec((None, n1 // SUBLANE, ct // LANE, n2t * 2 * SUBLANE, LANE),
                                      lambda pi, ci, ni: (pi, 0, ci, ni, 0)))
    outs = pl.pallas_call(
        functools.partial(_stage_a_kernel, n2t=n2t, n1=n1, has_inv=has_inv, has_fwd=want_fwd,
                          has_div=has_div),
        grid=grid, in_specs=specs, out_specs=out_specs, out_shape=out_shape,
        compiler_params=_cparams(("parallel", "parallel", "parallel")),
        name="hyena_stage_a" + ("_inv" if has_inv else "") + ("_fwd" if want_fwd else ""),
    )(*args)
    return outs


def _stage_b_kernel(*refs, filt_only):
    if filt_only:
        a_ref, f2_ref, o_ref = refs
    else:
        a_ref, f2_ref, g2_ref, kf_ref, o_ref = refs
    for kl in range(SUBLANE):
        are = _strided_rows(a_ref, kl, N2)
        aim = _strided_rows(a_ref, SUBLANE + kl, N2)
        rhs = jnp.concatenate([are, aim], axis=0).astype(BF16)
        t = jnp.dot(f2_ref[kl], rhs, preferred_element_type=F32)
        if filt_only:
            o_ref[kl] = t
            continue
        kf = kf_ref[kl]
        tre, tim = t[:N2], t[N2:]
        kre, kim = kf[:N2], kf[N2:]
        z = jnp.concatenate([tre * kre - tim * kim, tre * kim + tim * kre], axis=0).astype(BF16)
        y = jnp.dot(g2_ref[kl], z, preferred_element_type=F32)
        for ri in range(2):
            for nt in range(N2 // SUBLANE):
                _store_chunks(o_ref.at[nt], (kl * 2 + ri) * SUBLANE,
                              y[ri * N2 + nt * SUBLANE: ri * N2 + (nt + 1) * SUBLANE])


def _stage_b(tabs, a, kf=None, *, c_off=0, ct=512):
    p, nkt, ncc, _, _ = a.shape
    c = ncc * LANE
    n1 = nkt * SUBLANE
    ct = min(ct, c)
    filt_only = kf is None
    grid = (nkt, c // ct, p)
    a_spec = pl.BlockSpec((None, None, ct // LANE, N2 * 2 * SUBLANE, LANE),
                          lambda kt, ci, pi: (pi, kt, ci, 0, 0))
    tab_spec = pl.BlockSpec((SUBLANE, 2 * N2, 2 * N2), lambda kt, ci, pi: (kt, 0, 0))
    if filt_only:
        return pl.pallas_call(
            functools.partial(_stage_b_kernel, filt_only=True),
            grid=grid, in_specs=[a_spec, tab_spec],
            out_specs=pl.BlockSpec((SUBLANE, 2 * N2, ct), lambda kt, ci, pi: (kt, 0, ci)),
            out_shape=jax.ShapeDtypeStruct((n1, 2 * N2, c), F32),
            compiler_params=_cparams(("parallel", "parallel", "parallel")),
            name="hyena_stage_b_filter",
        )(a, tabs['f2'])
    return pl.pallas_call(
        functools.partial(_stage_b_kernel, filt_only=False),
        grid=grid,
        in_specs=[a_spec, tab_spec, tab_spec,
                  pl.BlockSpec((SUBLANE, 2 * N2, ct), lambda kt, ci, pi: (kt, 0, ci + c_off // ct))],
        out_specs=pl.BlockSpec((None, N2 // SUBLANE, ct // LANE, 2 * SUBLANE * SUBLANE, LANE),
                               lambda kt, ci, pi: (pi, 0, ci, kt, 0)),
        out_shape=jax.ShapeDtypeStruct((p, N2 // SUBLANE, c // LANE, n1 * 2 * SUBLANE, LANE), F32),
        compiler_params=_cparams(("parallel", "parallel", "parallel")),
        name="hyena_stage_b",
    )(a, tabs['f2'], tabs['g2'], kf)


def _permute_seq(a, nh):
    b, _, c = a.shape
    return a.reshape(b // 2, 2, nh, N2, c).transpose(0, 3, 1, 2, 4).reshape(b // 2, N2, 2 * nh, c)


def _unpermute_seq(a, nh):
    p, _, _, c = a.shape
    return a.reshape(p, N2, 2, nh, c).transpose(0, 2, 3, 1, 4).reshape(2 * p, nh * N2, c)


def _filter_spectrum(tabs, nh, k_un, ssum):
    n, call = k_un.shape
    kp = k_un.reshape(1, 2 * nh, N2, call).transpose(0, 2, 1, 3)
    (a,) = _stage_a(tabs, nh, kp, div=ssum.reshape(1, call), w1=tabs['w1_real'], ct=512, n2t=16)
    return _stage_b(tabs, a)


def _hyena_long(tabs, nh, kf, layer, v, x1, m2, skip):
    vp, x1p, m2p = (_permute_seq(t, nh) for t in (v, x1, m2))
    c0 = layer * HYENA_ORDER * W_HYENA
    skip2 = skip.reshape(1, HYENA_ORDER * W_HYENA)
    (a1,) = _stage_a(tabs, nh, vp, w1=tabs['w1_pair'])
    y1 = _stage_b(tabs, a1, kf, c_off=c0)
    z, a2 = _stage_a(tabs, nh, vp, y=y1, mul=x1p, skip=skip2, w1=tabs['w1_pair'], c_off=0)
    y2 = _stage_b(tabs, a2, kf, c_off=c0 + W_HYENA)
    (gp,) = _stage_a(tabs, nh, z, y=y2, mul=m2p, skip=skip2, want_fwd=False, c_off=W_HYENA)
    return _unpermute_seq(gp, nh)


def _ctx_conv_kernel(v_ref, x1_ref, m2_ref, k_ref, s_ref, skip_ref, ff_ref, fk_ref, gi_ref, o_ref,
                     *, n):
    hp = lax.Precision.HIGHEST

    def conv(u, o):
        kfull = k_ref[:, o * W_HYENA:(o + 1) * W_HYENA] / s_ref[:, o * W_HYENA:(o + 1) * W_HYENA]
        kf = jnp.dot(fk_ref[...], kfull, preferred_element_type=F32, precision=hp)
        uf = jnp.dot(ff_ref[...], u, preferred_element_type=F32, precision=hp)
        ure, uim, kre, kim = uf[:n], uf[n:], kf[:n], kf[n:]
        z = jnp.concatenate([ure * kre - uim * kim, ure * kim + uim * kre], axis=0)
        y = jnp.dot(gi_ref[...], z, preferred_element_type=F32, precision=hp)
        return y + u * skip_ref[o:o + 1, :]

    v = v_ref[...].astype(F32)
    z1 = x1_ref[...].astype(F32) * conv(v, 0)
    o_ref[...] = (m2_ref[...].astype(F32) * conv(z1, 1)).astype(o_ref.dtype)


def _ctx_tables(lc):
    n = 2 * lc
    k = jnp.arange(n, dtype=jnp.int32)
    ph = _angles(k[:, None], k[None, :], n)
    c, s = jnp.cos(ph), jnp.sin(ph)
    fk = jnp.concatenate([c, -s], axis=0)
    ff = fk[:, :lc]
    gi = jnp.concatenate([c[:lc], -s[:lc]], axis=1) / n
    return ff, fk, gi


def _ctx_hyena(ctabs, k_un, ssum, layer, v, x1, m2, skip):
    b, lc, c = v.shape
    n = 2 * lc
    ff, fk, gi = ctabs
    wc = HYENA_ORDER * W_HYENA
    tok = pl.BlockSpec((None, lc, c), lambda bi: (bi, 0, 0))
    return pl.pallas_call(
        functools.partial(_ctx_conv_kernel, n=n),
        grid=(b,),
        in_specs=[tok, tok, tok,
                  pl.BlockSpec((n, wc), lambda bi: (0, layer)),
                  pl.BlockSpec((None, 1, wc), lambda bi: (layer, 0, 0)),
                  pl.BlockSpec((HYENA_ORDER, c), lambda bi: (0, 0)),
                  _full_spec(ff), _full_spec(fk), _full_spec(gi)],
        out_specs=tok,
        out_shape=jax.ShapeDtypeStruct((b, lc, c), BF16),
        compiler_params=_cparams(("parallel",)),
        name="ctx_hyena",
    )(v, x1, m2, k_un, ssum, skip, ff, fk, gi)


def _merge_kernel(x_ref, a_ref, gh_ref, att_ref, sm_ref, g_ref, sc_ref, sh_ref, gt_ref,
                  wg_ref, bg_ref, sco_ref, hyo_ref, mlo_ref, wo_ref, fg_ref, o_ref, *, final):
    x = x_ref[...]
    d = x.shape[-1]
    h = _mod_norm(x, g_ref[...], sc_ref[...], sh_ref[...])
    gates = jax.nn.sigmoid(jnp.dot(h.astype(BF16), wg_ref[...], preferred_element_type=F32) + bg_ref[...])
    ya = jnp.dot(a_ref[...], sco_ref[...], preferred_element_type=F32)
    yh = jnp.dot(gh_ref[...], hyo_ref[...], preferred_element_type=F32)
    am = (sm_ref[...].astype(F32) * att_ref[...].astype(F32)).astype(BF16)
    ym = jnp.dot(am, mlo_ref[...], preferred_element_type=F32)
    y = gates[:, 0:d] * ya + gates[:, d:2 * d] * yh + gates[:, 2 * d:3 * d] * ym
    o = jnp.dot(y.astype(BF16), wo_ref[...], preferred_element_type=F32)
    xn = x + gt_ref[...] * o
    if final:
        xn = _rms(xn, fg_ref[...])
    o_ref[...] = xn


def _merge_call(x2d, seq_len, mods, g, lw, a, gh, att, sm, final_g, final):
    n_rows, d = x2d.shape
    tm = min(256, seq_len)
    tiles_per_seq = seq_len // tm
    scale, shift, gate = mods
    g2 = g.reshape(1, 1, d)
    fg = final_g.reshape(1, d)
    tok = lambda wd: pl.BlockSpec((tm, wd), lambda i: (i, 0))
    consts = [lw['wg'], lw['bg'], lw['sc_out'], lw['hy_out'], lw['mla_out'], lw['w_o'], fg]
    in_specs = [tok(d), tok(W_CONV), tok(W_HYENA), tok(W_MLA), tok(W_MLA),
                _mod_spec(g2, tiles_per_seq), _mod_spec(scale, tiles_per_seq),
                _mod_spec(shift, tiles_per_seq), _mod_spec(gate, tiles_per_seq)]
    in_specs += [_full_spec(c) for c in consts]
    return pl.pallas_call(
        functools.partial(_merge_kernel, final=final),
        grid=(n_rows // tm,),
        in_specs=in_specs,
        out_specs=tok(d),
        out_shape=jax.ShapeDtypeStruct((n_rows, d), F32),
        compiler_params=_cparams(("parallel",)),
        name="merge_final" if final else "merge",
    )(x2d, a, gh, att, sm, g2, scale, shift, gate, *consts)


def _rope_swap_cols(w):
    half = QK_ROPE // 2
    return jnp.concatenate([-w[..., half:], w[..., :half]], axis=-1)


def _layer_weights(i, w_in, b_in, mla_q_norm, mla_w_uq, mla_kv_norm, mla_w_ukv, sc_out, hy_out,
                   mla_out, w_o):
    wi, bi = w_in[i], b_in[i]
    d = wi.shape[0]
    lw = {}
    lw['w1'] = jnp.concatenate([wi[:, O_XIN:O_HPROJ], wi[:, O_ZM:O_GATES]], axis=1).astype(BF16)
    lw['b1'] = jnp.concatenate([bi[O_XIN:O_HPROJ], bi[O_ZM:O_GATES]])[None, :]
    lw['w2'] = wi[:, O_HPROJ:O_CQ].astype(BF16)
    lw['b2'] = bi[O_HPROJ:O_CQ][None, :]
    wkr, bkr = wi[:, O_KR:O_ZM], bi[O_KR:O_ZM]
    zw = lambda n: jnp.zeros((d, n), F32)
    zb = lambda n: jnp.zeros((n,), F32)
    lw['w3'] = jnp.concatenate([wi[:, O_CQ:O_KR], wkr, zw(96), _rope_swap_cols(wkr), zw(96)], axis=1).astype(BF16)
    lw['b3'] = jnp.concatenate([bi[O_CQ:O_KR], bkr, zb(96), _rope_swap_cols(bkr), zb(96)])[None, :]
    lw['qg'] = mla_q_norm[i][None, :]
    lw['kvg'] = mla_kv_norm[i][None, :]
    wuq = mla_w_uq[i].reshape(Q_LORA, N_HEADS, QK_NOPE + QK_ROPE)
    pad = jnp.zeros((Q_LORA, N_HEADS, HEAD_PAD - QK_NOPE - QK_ROPE), F32)
    lw['wq'] = jnp.concatenate([wuq, pad], axis=-1).reshape(Q_LORA, N_HEADS * HEAD_PAD).astype(BF16)
    zn = jnp.zeros((Q_LORA, N_HEADS, QK_NOPE), F32)
    lw['wqs'] = jnp.concatenate([zn, _rope_swap_cols(wuq[..., QK_NOPE:]), pad], axis=-1).reshape(
        Q_LORA, N_HEADS * HEAD_PAD).astype(BF16)
    wukv = mla_w_ukv[i].reshape(KV_LORA, N_HEADS, QK_NOPE + V_HEAD)
    padk = jnp.zeros((KV_LORA, N_HEADS, HEAD_PAD - QK_NOPE), F32)
    lw['wk'] = jnp.concatenate([wukv[..., :QK_NOPE], padk], axis=-1).reshape(
        KV_LORA, N_HEADS * HEAD_PAD).astype(BF16)
    padv = jnp.zeros((KV_LORA, N_HEADS, HEAD_PAD - V_HEAD), F32)
    lw['wv'] = jnp.concatenate([wukv[..., QK_NOPE:], padv], axis=-1).reshape(
        KV_LORA, N_HEADS * HEAD_PAD).astype(BF16)
    lw['vb'] = jnp.zeros((N_HEADS, HEAD_PAD), F32).at[:, V_HEAD].set(1.0).reshape(1, N_HEADS * HEAD_PAD)
    e2 = jnp.zeros((QK_ROPE, N_HEADS, HEAD_PAD), F32).at[:, :, QK_NOPE:QK_NOPE + QK_ROPE].set(
        jnp.eye(QK_ROPE, dtype=F32)[:, None, :])
    lw['e2'] = e2.reshape(QK_ROPE, N_HEADS * HEAD_PAD).astype(BF16)
    lw['wg'] = wi[:, O_GATES:].astype(BF16)
    lw['bg'] = bi[O_GATES:][None, :]
    lw['sc_out'] = sc_out[i].astype(BF16)
    lw['hy_out'] = hy_out[i].astype(BF16)
    lw['mla_out'] = mla_out[i].astype(BF16)
    lw['w_o'] = w_o[i].astype(BF16)
    return lw


def _rope_tables(seq_len, use_rope):
    scale = (QK_NOPE + QK_ROPE) ** -0.5 * math.log2(math.e)
    if use_rope:
        rows = seq_len // GRID_W
        row = jnp.broadcast_to(jnp.arange(rows, dtype=F32)[:, None], (rows, GRID_W)).reshape(seq_len)
        col = jnp.broadcast_to(jnp.arange(GRID_W, dtype=F32)[None, :], (rows, GRID_W)).reshape(seq_len)
        n_f = QK_ROPE // 4
        inv = ROPE_BASE ** (-jnp.arange(n_f, dtype=F32) / n_f)
        ang = jnp.concatenate([row[:, None] * inv, col[:, None] * inv], axis=-1)
        cos, sin = jnp.cos(ang), jnp.sin(ang)
    else:
        cos = jnp.ones((seq_len, QK_ROPE // 2), F32)
        sin = jnp.zeros((seq_len, QK_ROPE // 2), F32)
    cosk = jnp.concatenate([cos, cos], axis=-1)
    sink = jnp.concatenate([sin, sin], axis=-1)
    ones = jnp.ones((seq_len, QK_NOPE), F32)
    zeros = jnp.zeros((seq_len, QK_NOPE), F32)
    tail = jnp.zeros((seq_len, HEAD_PAD - QK_NOPE - QK_ROPE), F32)
    cosq = jnp.tile(jnp.concatenate([ones, cosk, tail], axis=-1), (1, N_HEADS)) * scale
    sinq = jnp.tile(jnp.concatenate([zeros, sink, tail], axis=-1), (1, N_HEADS)) * scale
    return cosq, sinq, cosk, sink


def kernel(x, c, ctx, c_ctx, ada_w, ada_b, norm_g, w_in, b_in, sc_conv_w, sc_conv_b, sc_out, hy_conv_w,
           hy_conv_b, hy_w1, hy_b1, hy_w2, hy_b2, hy_w3, hy_freq, hy_skip, hy_out, mla_q_norm, mla_w_uq,
           mla_kv_norm, mla_w_ukv, mla_out, w_o, final_g):
    bsz, seq, d = x.shape
    lc = ctx.shape[1]
    depth = ada_w.shape[0]
    nh = seq // N2
    assert d == D_MODEL and bsz % 2 == 0 and nh % SUBLANE == 0 and bsz <= 7

    cc = jnp.zeros((8, d), F32).at[:bsz].set(c).at[bsz].set(c_ctx)
    mods = _ada_mods(cc, ada_w, ada_b)

    tabs = _dft_tables(nh)
    k_un, ssum = _filter_gen(seq, depth, hy_w1, hy_b1, hy_w2, hy_b2, hy_w3, hy_freq)
    kf = _filter_spectrum(tabs, nh, k_un, ssum)
    if depth > 1:
        kc_un, sc_sum = _filter_gen(lc, depth - 1, hy_w1, hy_b1, hy_w2, hy_b2, hy_w3, hy_freq)
        ctabs = _ctx_tables(lc)

    rope_l = _rope_tables(seq, True)
    rope_c = _rope_tables(lc, False)

    x_lat = x.reshape(bsz * seq, d)
    x_ctx = ctx.reshape(bsz * lc, d)
    for i in range(depth):
        last = i == depth - 1
        lw = _layer_weights(i, w_in, b_in, mla_q_norm, mla_w_uq, mla_kv_norm, mla_w_ukv, sc_out,
                            hy_out, mla_out, w_o)
        m = mods[i]
        split = lambda r: tuple(r[:, None, j * d:(j + 1) * d] for j in range(3))
        shift_l, scale_l, gate_l = split(m[:bsz])
        shift_c, scale_c, gate_c = split(m[bsz:bsz + 1])
        g = norm_g[i]
        scw, scb = sc_conv_w[i], sc_conv_b[i][None, :]
        hcw, hcb = hy_conv_w[i], hy_conv_b[i][None, :]

        q_c, k_c, v_c = _kp3_call(x_ctx, lc, (scale_c, shift_c), g, lw, rope_c)
        q_l, k_l, v_l = _kp3_call(x_lat, seq, (scale_l, shift_l), g, lw, rope_l)
        hq = N_HEADS * HEAD_PAD
        k_all = jnp.concatenate([k_c.reshape(bsz, lc, hq), k_l.reshape(bsz, seq, hq)], axis=1)
        v_all = jnp.concatenate([v_c.reshape(bsz, lc, hq), v_l.reshape(bsz, seq, hq)], axis=1)
        att_l = _attention(q_l.reshape(bsz, seq, hq), k_all, v_all).reshape(bsz * seq, W_MLA)

        a_l, sm_l = _kp_conv_call(_kp1_kernel, "kp1_conv", x_lat, seq, (scale_l, shift_l), g,
                                  lw['w1'], lw['b1'], scw, scb, (W_CONV, W_MLA))
        v_h, x1_h, m2_h = _kp_conv_call(_kp2_kernel, "kp2_hyena", x_lat, seq, (scale_l, shift_l), g,
                                        lw['w2'], lw['b2'], hcw, hcb, (W_HYENA,) * 3)
        r3 = lambda t: t.reshape(bsz, seq, W_HYENA)
        gh_l = _hyena_long(tabs, nh, kf, i, r3(v_h), r3(x1_h), r3(m2_h), hy_skip[i])
        new_lat = _merge_call(x_lat, seq, (scale_l, shift_l, gate_l), g, lw, a_l,
                              gh_l.reshape(bsz * seq, W_HYENA), att_l, sm_l, final_g, last)
        if not last:
            att_c = _attention(q_c.reshape(bsz, lc, hq), k_c.reshape(bsz, lc, hq),
                               v_c.reshape(bsz, lc, hq)).reshape(bsz * lc, W_MLA)
            a_c, sm_c = _kp_conv_call(_kp1_kernel, "kp1_conv_ctx", x_ctx, lc, (scale_c, shift_c), g,
                                      lw['w1'], lw['b1'], scw, scb, (W_CONV, W_MLA))
            v_hc, x1_hc, m2_hc = _kp_conv_call(_kp2_kernel, "kp2_hyena_ctx", x_ctx, lc,
                                               (scale_c, shift_c), g, lw['w2'], lw['b2'], hcw, hcb,
                                               (W_HYENA,) * 3)
            rc = lambda t: t.reshape(bsz, lc, W_HYENA)
            gh_c = _ctx_hyena(ctabs, kc_un, sc_sum, i, rc(v_hc), rc(x1_hc), rc(m2_hc), hy_skip[i])
            x_ctx = _merge_call(x_ctx, lc, (scale_c, shift_c, gate_c), g, lw, a_c,
                                gh_c.reshape(bsz * lc, W_HYENA), att_c, sm_c, final_g, False)
        x_lat = new_lat
    return x_lat.reshape(bsz, seq, d)
```

```python
import functools
import math

import jax
import jax.numpy as jnp
from jax import lax
from jax.experimental import pallas as pl
from jax.experimental.pallas import tpu as pltpu

F32 = jnp.float32
BF16 = jnp.bfloat16

D_MODEL = 1024
DEPTH = 4
GRID_W = 64
W_CONV = 512
W_HYENA = 512
HYENA_ORDER = 2
HYENA_EMB = 33
HYENA_BANDS = (HYENA_EMB - 1) // 2
HYENA_HID = 64
HYENA_FAST_DECAY = 0.3
HYENA_SLOW_DECAY = 1.5
HYENA_TARGET = 1e-2
N_HEADS = 8
QK_NOPE = 64
QK_ROPE = 32
V_HEAD = 64
Q_LORA = 384
KV_LORA = 256
W_MLA = N_HEADS * V_HEAD
ROPE_BASE = 10000.0
N_BRANCH = 3
EPS = 1e-6

O_XIN, O_GB, O_GC, O_ZA = 0, 512, 1024, 1536
O_HPROJ, O_ZH = 2048, 3584
O_CQ, O_CKV, O_KR, O_ZM, O_GATES = 4096, 4480, 4736, 4768, 5280
N_IN = 8352

LANE = 128
SUBLANE = 8
HEAD_PAD = 128
N2 = 128
VMEM_LIMIT = 56 * 1024 * 1024


def _cparams(sem):
    return pltpu.CompilerParams(dimension_semantics=sem, vmem_limit_bytes=VMEM_LIMIT)


def _ada_kernel(c_ref, w_ref, b_ref, o_ref):
    c = c_ref[...]
    s = c * jax.nn.sigmoid(c)
    o_ref[...] = jnp.dot(s, w_ref[...], preferred_element_type=F32,
                         precision=lax.Precision.HIGHEST) + b_ref[...]


def _ada_mods(cc, ada_w, ada_b):
    depth = ada_w.shape[0]
    d = cc.shape[1]
    return pl.pallas_call(
        _ada_kernel,
        grid=(depth, 3),
        in_specs=[pl.BlockSpec((8, d), lambda l, j: (0, 0)),
                  pl.BlockSpec((None, d, d), lambda l, j: (l, 0, j)),
                  pl.BlockSpec((None, 1, d), lambda l, j: (l, 0, j))],
        out_specs=pl.BlockSpec((None, 8, d), lambda l, j: (l, 0, j)),
        out_shape=jax.ShapeDtypeStruct((depth, 8, 3 * d), F32),
        compiler_params=_cparams(("parallel", "parallel")),
        name="ada_mods",
    )(cc, ada_w, ada_b.reshape(depth, 1, 3 * d))


def _mod_norm(x, g, scale, shift):
    y = x * lax.rsqrt(jnp.mean(x * x, axis=-1, keepdims=True) + EPS)
    return (y * g) * (1.0 + scale) + shift


def _rms(x, g):
    return x * lax.rsqrt(jnp.mean(x * x, axis=-1, keepdims=True) + EPS) * g


def _silu(x):
    return x * jax.nn.sigmoid(x)


def _dwconv3_ext(u_ext, w, b, first, last, tm):
    n = tm + 2 * SUBLANE
    prev = pltpu.roll(u_ext, 1, 0)[SUBLANE:SUBLANE + tm]
    nxt = pltpu.roll(u_ext, n - 1, 0)[SUBLANE:SUBLANE + tm]
    row = lax.broadcasted_iota(jnp.int32, (tm, 1), 0)
    prev = jnp.where(jnp.logical_and(first, row == 0), 0.0, prev)
    nxt = jnp.where(jnp.logical_and(last, row == tm - 1), 0.0, nxt)
    return prev * w[0:1] + u_ext[SUBLANE:SUBLANE + tm] * w[1:2] + nxt * w[2:3] + b


def _proj_ext(xm_ref, xp_ref, xn_ref, g_ref, sc_ref, sh_ref, w_ref, b_ref):
    x_ext = jnp.concatenate([xp_ref[...], xm_ref[...], xn_ref[...]], axis=0)
    h = _mod_norm(x_ext, g_ref[...], sc_ref[...], sh_ref[...])
    return jnp.dot(h.astype(BF16), w_ref[...], preferred_element_type=F32) + b_ref[...]


def _kp1_kernel(xm_ref, xp_ref, xn_ref, g_ref, sc_ref, sh_ref, w_ref, b_ref, cw_ref, cb_ref,
                a_ref, sm_ref, *, tm, tiles_per_seq):
    i = pl.program_id(0)
    first = (i % tiles_per_seq) == 0
    last = (i % tiles_per_seq) == tiles_per_seq - 1
    p = _proj_ext(xm_ref, xp_ref, xn_ref, g_ref, sc_ref, sh_ref, w_ref, b_ref)
    prod = p[:, 2 * W_CONV:3 * W_CONV] * p[:, 0:W_CONV]
    conv = _dwconv3_ext(prod, cw_ref[...], cb_ref[...], first, last, tm)
    pm = p[SUBLANE:SUBLANE + tm]
    a = _silu(pm[:, 3 * W_CONV:4 * W_CONV]) * (pm[:, W_CONV:2 * W_CONV] * conv)
    a_ref[...] = a.astype(a_ref.dtype)
    sm_ref[...] = _silu(pm[:, 4 * W_CONV:4 * W_CONV + W_MLA]).astype(sm_ref.dtype)


def _kp2_kernel(xm_ref, xp_ref, xn_ref, g_ref, sc_ref, sh_ref, w_ref, b_ref, cw_ref, cb_ref,
                v_ref, x1_ref, m2_ref, *, tm, tiles_per_seq):
    i = pl.program_id(0)
    first = (i % tiles_per_seq) == 0
    last = (i % tiles_per_seq) == tiles_per_seq - 1
    p = _proj_ext(xm_ref, xp_ref, xn_ref, g_ref, sc_ref, sh_ref, w_ref, b_ref)
    u = _dwconv3_ext(p[:, 0:3 * W_HYENA], cw_ref[...], cb_ref[...], first, last, tm)
    zh = p[SUBLANE:SUBLANE + tm, 3 * W_HYENA:4 * W_HYENA]
    v_ref[...] = u[:, 0:W_HYENA].astype(v_ref.dtype)
    x1_ref[...] = u[:, W_HYENA:2 * W_HYENA].astype(x1_ref.dtype)
    m2_ref[...] = (_silu(zh) * u[:, 2 * W_HYENA:3 * W_HYENA]).astype(m2_ref.dtype)


def _kp3_kernel(xm_ref, g_ref, sc_ref, sh_ref, w_ref, b_ref, qg_ref, kvg_ref, wq_ref, wqs_ref,
                wk_ref, wv_ref, vb_ref, e2_ref, cosq_ref, sinq_ref, cosk_ref, sink_ref,
                q_ref, k_ref, v_ref):
    h = _mod_norm(xm_ref[...], g_ref[...], sc_ref[...], sh_ref[...])
    p = jnp.dot(h.astype(BF16), w_ref[...], preferred_element_type=F32) + b_ref[...]
    cq = _rms(p[:, 0:Q_LORA], qg_ref[...]).astype(BF16)
    ckv = _rms(p[:, Q_LORA:Q_LORA + KV_LORA], kvg_ref[...]).astype(BF16)
    kr = p[:, 640:640 + QK_ROPE]
    krs = p[:, 768:768 + QK_ROPE]
    qa = jnp.dot(cq, wq_ref[...], preferred_element_type=F32)
    qb = jnp.dot(cq, wqs_ref[...], preferred_element_type=F32)
    q_ref[...] = (qa * cosq_ref[...] + qb * sinq_ref[...]).astype(q_ref.dtype)
    kr_rot = (kr * cosk_ref[...] + krs * sink_ref[...]).astype(BF16)
    kn = jnp.dot(ckv, wk_ref[...], preferred_element_type=F32)
    krp = jnp.dot(kr_rot, e2_ref[...], preferred_element_type=F32)
    k_ref[...] = (kn + krp).astype(k_ref.dtype)
    v_ref[...] = (jnp.dot(ckv, wv_ref[...], preferred_element_type=F32) + vb_ref[...]).astype(v_ref.dtype)


def _mod_spec(mod, tiles_per_seq):
    d = mod.shape[-1]
    if mod.shape[0] == 1:
        return pl.BlockSpec((None, 1, d), lambda i: (0, 0, 0))
    return pl.BlockSpec((None, 1, d), lambda i: (i // tiles_per_seq, 0, 0))


def _full_spec(a):
    nd = a.ndim
    return pl.BlockSpec(a.shape, lambda i: (0,) * nd)


def _halo_specs(n_rows, tm, d):
    r = tm // SUBLANE
    nb = n_rows // SUBLANE
    return [pl.BlockSpec((tm, d), lambda i: (i, 0)),
            pl.BlockSpec((SUBLANE, d), lambda i: (jnp.maximum(i * r - 1, 0), 0)),
            pl.BlockSpec((SUBLANE, d), lambda i: (jnp.minimum((i + 1) * r, nb - 1), 0))]


def _kp_conv_call(kern, name, x2d, seq_len, mods, g, w, b, cw, cb, out_widths):
    n_rows, d = x2d.shape
    tm = min(512, seq_len)
    tiles_per_seq = seq_len // tm
    scale, shift = mods
    g2 = g.reshape(1, 1, d)
    in_specs = _halo_specs(n_rows, tm, d) + [
        _mod_spec(g2, tiles_per_seq), _mod_spec(scale, tiles_per_seq), _mod_spec(shift, tiles_per_seq),
        _full_spec(w), _full_spec(b), _full_spec(cw), _full_spec(cb)]
    return pl.pallas_call(
        functools.partial(kern, tm=tm, tiles_per_seq=tiles_per_seq),
        grid=(n_rows // tm,),
        in_specs=in_specs,
        out_specs=[pl.BlockSpec((tm, wd), lambda i: (i, 0)) for wd in out_widths],
        out_shape=[jax.ShapeDtypeStruct((n_rows, wd), BF16) for wd in out_widths],
        compiler_params=_cparams(("parallel",)),
        name=name,
    )(x2d, x2d, x2d, g2, scale, shift, w, b, cw, cb)


def _kp3_aliased_kernel(*refs):
    _kp3_kernel(*refs[:-5], *refs[-3:])


def _kp3_call(x2d, bsz, seq_len, kv_len, kv_row0, mods, g, lw, tabs, kv_bufs=None):
    n_rows, d = x2d.shape
    tm = min(512, seq_len)
    tiles_per_seq = seq_len // tm
    blk0 = kv_row0 // tm
    assert kv_row0 % tm == 0
    scale, shift = mods
    g2 = g.reshape(1, 1, d)
    cosq, sinq, cosk, sink = tabs
    consts = [lw['w3'], lw['b3'], lw['qg'], lw['kvg'], lw['wq'], lw['wqs'], lw['wk'], lw['wv'], lw['vb'],
              lw['e2']]

    def tab_spec(t):
        return pl.BlockSpec((tm, t.shape[1]), lambda i: (i % tiles_per_seq, 0))

    in_specs = [pl.BlockSpec((tm, d), lambda i: (i, 0)),
                _mod_spec(g2, tiles_per_seq), _mod_spec(scale, tiles_per_seq), _mod_spec(shift, tiles_per_seq)]
    in_specs += [_full_spec(a) for a in consts]
    in_specs += [tab_spec(t) for t in (cosq, sinq, cosk, sink)]
    args = [x2d, g2, scale, shift, *consts, cosq, sinq, cosk, sink]
    hq = N_HEADS * HEAD_PAD
    kv_spec = pl.BlockSpec((None, tm, hq), lambda i: (i // tiles_per_seq, blk0 + i % tiles_per_seq, 0))
    kv_shape = jax.ShapeDtypeStruct((bsz, kv_len, hq), BF16)
    aliases = {}
    kern = _kp3_kernel
    if kv_bufs is not None:
        kern = _kp3_aliased_kernel
        in_specs += [pl.BlockSpec(memory_space=pl.ANY)] * 2
        aliases = {len(args): 1, len(args) + 1: 2}
        args += list(kv_bufs)
    return pl.pallas_call(
        kern,
        grid=(n_rows // tm,),
        in_specs=in_specs,
        out_specs=[pl.BlockSpec((tm, hq), lambda i: (i, 0)), kv_spec, kv_spec],
        out_shape=[jax.ShapeDtypeStruct((n_rows, hq), BF16), kv_shape, kv_shape],
        input_output_aliases=aliases,
        compiler_params=_cparams(("parallel",)),
        name="kp3_mla" if kv_bufs is None else "kp3_mla_ctx",
    )(*args)


ATTN_CHAINS = 4


def _attn_kernel(q_ref, k_ref, v_ref, o_ref, s_ref, p_ref, m_ref, acc_ref):
    h = pl.program_id(2)
    ki = pl.program_id(3)

    @pl.when(ki == 0)
    def _():
        m_ref[...] = jnp.full(m_ref.shape, -jnp.inf, F32)
        acc_ref[...] = jnp.zeros(acc_ref.shape, F32)

    hq = q_ref.shape[0] // ATTN_CHAINS
    chains = [pl.ds(c * hq, hq) for c in range(ATTN_CHAINS)]
    for rows in chains:
        s_ref[rows, :] = lax.dot_general(q_ref[rows, :], k_ref[...], (((1,), (1,)), ((), ())),
                                         preferred_element_type=F32)
    alphas = []
    for rows in chains:
        m_prev = m_ref[rows, :]
        m_new = jnp.maximum(m_prev, jnp.max(s_ref[rows, :], axis=-1, keepdims=True))
        alphas.append(jnp.exp2(m_prev - m_new))
        m_ref[rows, :] = m_new
        p_ref[rows, :] = jnp.exp2((s_ref[rows, :] - m_new).astype(BF16))
    for rows, alpha in zip(chains, alphas):
        acc_ref[rows, :] = acc_ref[rows, :] * alpha + jnp.dot(p_ref[rows, :], v_ref[...],
                                                              preferred_element_type=F32)

    @pl.when(ki == pl.num_programs(3) - 1)
    def _():
        acc = acc_ref[...]
        o = (acc[:, 0:V_HEAD] / acc[:, V_HEAD:V_HEAD + 1]).astype(o_ref.dtype)

        @pl.when(h % 2 == 0)
        def _():
            o_ref[:, 0:V_HEAD] = o

        @pl.when(h % 2 == 1)
        def _():
            o_ref[:, V_HEAD:2 * V_HEAD] = o


MXU_DIM = 256
KV_TILE_CAP = 2816


def _kv_tile(lk):
    for step in (MXU_DIM, LANE):
        cands = [t for t in range(step, min(lk, KV_TILE_CAP) + 1, step) if lk % t == 0]
        if cands:
            return cands[-1]
    return lk


def _attention(q, k, v, kv_row0=0, kv_len=None):
    b, lq, _ = q.shape
    lk = k.shape[1] if kv_len is None else kv_len
    tq = min(1024, lq)
    tk = _kv_tile(lk)
    assert kv_row0 % tk == 0
    kb0 = kv_row0 // tk
    grid = (b, lq // tq, N_HEADS, lk // tk)
    return pl.pallas_call(
        _attn_kernel,
        grid=grid,
        in_specs=[pl.BlockSpec((None, tq, HEAD_PAD), lambda bi, qi, h, ki: (bi, qi, h)),
                  pl.BlockSpec((None, tk, HEAD_PAD), lambda bi, qi, h, ki: (bi, kb0 + ki, h)),
                  pl.BlockSpec((None, tk, HEAD_PAD), lambda bi, qi, h, ki: (bi, kb0 + ki, h))],
        out_specs=pl.BlockSpec((None, tq, 2 * V_HEAD), lambda bi, qi, h, ki: (bi, qi, h // 2)),
        out_shape=jax.ShapeDtypeStruct((b, lq, W_MLA), BF16),
        scratch_shapes=[pltpu.VMEM((tq, tk), F32), pltpu.VMEM((tq, tk), BF16),
                        pltpu.VMEM((tq, 1), F32), pltpu.VMEM((tq, HEAD_PAD), F32)],
        compiler_params=_cparams(("parallel", "parallel", "arbitrary", "arbitrary")),
        name="mla_attention",
    )(q, k, v)


COL_BWD = HYENA_EMB
COL_DROP = HYENA_EMB + 1


def _filt_kernel(z_ref, w1_ref, b1_ref, w2_ref, b2_ref, fr_ref, w3_ref, dl_ref, k_ref, s_ref):
    i = pl.program_id(1)
    hp = lax.Precision.HIGHEST
    wc = HYENA_ORDER * W_HYENA
    z = z_ref[...]
    fr = fr_ref[...]
    hid = jnp.sin(fr * (jnp.dot(z, w1_ref[...], preferred_element_type=F32, precision=hp) + b1_ref[...]))
    hid = jnp.sin(fr * (jnp.dot(hid, w2_ref[...], preferred_element_type=F32, precision=hp) + b2_ref[...]))
    h2 = jnp.dot(hid.astype(BF16), w3_ref[...], preferred_element_type=F32)
    h = jnp.where(z[:, COL_BWD:COL_BWD + 1] > 0.5, h2[:, wc:], h2[:, :wc])
    t = z[:, 0:1]
    h = h * jnp.exp(-t * dl_ref[...])

    @pl.when(i == 0)
    def _():
        s_ref[...] = jnp.zeros(s_ref.shape, F32)

    s_ref[...] += jnp.sum(jnp.abs(h), axis=0, keepdims=True)
    k_ref[...] = jnp.where(z[:, COL_DROP:COL_DROP + 1] > 0.5, 0.0, h)


def _filter_tables(seq_len, permuted):
    t = jnp.linspace(0.0, 1.0, seq_len, dtype=F32)[:, None]
    w = 2.0 * math.pi * jnp.arange(seq_len, dtype=F32)[:, None] / seq_len
    f = jnp.linspace(1e-4, HYENA_BANDS - 1, HYENA_BANDS, dtype=F32)[None, :]
    z = jnp.concatenate([t, jnp.cos(f * w), -jnp.sin(f * w)], axis=-1)
    idx = jnp.concatenate([jnp.arange(seq_len), jnp.zeros((1,), jnp.int32),
                           seq_len - jnp.arange(1, seq_len)])
    n = jnp.arange(2 * seq_len)
    flags = jnp.stack([n >= seq_len, n == seq_len], axis=-1).astype(F32)
    zz = jnp.concatenate([z[idx], flags], axis=-1)
    zz = jnp.pad(zz, ((0, 0), (0, LANE - zz.shape[1])))
    if permuted:
        zz = zz.reshape(2 * seq_len // N2, N2, LANE).transpose(1, 0, 2).reshape(2 * seq_len, LANE)
    return zz


def _hyena_deltas():
    max_decay = math.log(HYENA_TARGET) / HYENA_FAST_DECAY
    min_decay = math.log(HYENA_TARGET) / HYENA_SLOW_DECAY
    deltas = jnp.abs(jnp.linspace(min_decay, max_decay, W_HYENA, dtype=F32))
    return jnp.tile(deltas, HYENA_ORDER)[None, :]


def _filter_gen(seq_len, n_layers, permuted, hy_w1, hy_b1, hy_w2, hy_b2, hy_w3, hy_freq):
    zz = _filter_tables(seq_len, permuted)
    tmf = min(512, seq_len)
    tiles_half = seq_len // tmf
    wc = HYENA_ORDER * W_HYENA
    w1p = jnp.pad(hy_w1[:n_layers], ((0, 0), (0, LANE - HYENA_EMB), (0, 0)))
    w3r = hy_w3[:n_layers].astype(BF16)
    r1 = lambda a: a[:n_layers].reshape(n_layers, 1, HYENA_HID)
    return pl.pallas_call(
        _filt_kernel,
        grid=(n_layers, 2 * tiles_half),
        in_specs=[pl.BlockSpec((tmf, LANE), lambda l, i: (i, 0)),
                  pl.BlockSpec((None, LANE, HYENA_HID), lambda l, i: (l, 0, 0)),
                  pl.BlockSpec((None, 1, HYENA_HID), lambda l, i: (l, 0, 0)),
                  pl.BlockSpec((None, HYENA_HID, HYENA_HID), lambda l, i: (l, 0, 0)),
                  pl.BlockSpec((None, 1, HYENA_HID), lambda l, i: (l, 0, 0)),
                  pl.BlockSpec((None, 1, HYENA_HID), lambda l, i: (l, 0, 0)),
                  pl.BlockSpec((None, HYENA_HID, 2 * wc), lambda l, i: (l, 0, 0)),
                  pl.BlockSpec((1, wc), lambda l, i: (0, 0))],
        out_specs=[pl.BlockSpec((tmf, wc), lambda l, i: (i, l)),
                   pl.BlockSpec((None, 1, wc), lambda l, i: (l, 0, 0))],
        out_shape=[jax.ShapeDtypeStruct((2 * seq_len, n_layers * wc), F32),
                   jax.ShapeDtypeStruct((n_layers, 1, wc), F32)],
        compiler_params=_cparams(("parallel", "arbitrary")),
        name="hyena_filter",
    )(zz, w1p, r1(hy_b1), hy_w2[:n_layers], r1(hy_b2), r1(hy_freq), w3r, _hyena_deltas())


def _angles(a, b, n):
    m = (a * b) % n
    return m.astype(F32) * (2.0 * math.pi / n)


def _dft_tables(nh):
    n1 = 2 * nh
    n = n1 * N2
    k1 = jnp.arange(n1, dtype=jnp.int32)
    th = _angles(k1[:, None], jnp.arange(nh, dtype=jnp.int32)[None, :], n1)
    c, s = jnp.cos(th), jnp.sin(th)
    w1_pair = jnp.concatenate([jnp.concatenate([c, s], 1), jnp.concatenate([-s, c], 1)], 0)
    thf = _angles(k1[:, None], k1[None, :], n1)
    w1_real = jnp.concatenate([jnp.cos(thf), -jnp.sin(thf)], 0)
    g1 = jnp.concatenate([jnp.concatenate([c.T, -s.T], 1), jnp.concatenate([s.T, c.T], 1)], 0) / n
    n2 = jnp.arange(N2, dtype=jnp.int32)
    kk = k1[:, None, None] + n1 * n2[None, :, None]
    ph = _angles(kk, n2[None, None, :], n)
    cp, sp = jnp.cos(ph), jnp.sin(ph)
    f2 = jnp.concatenate([jnp.concatenate([cp, sp], 2), jnp.concatenate([-sp, cp], 2)], 1)
    cpt, spt = cp.transpose(0, 2, 1), sp.transpose(0, 2, 1)
    g2 = jnp.concatenate([jnp.concatenate([cpt, -spt], 2), jnp.concatenate([spt, cpt], 2)], 1)
    return dict(w1_pair=w1_pair.astype(BF16), w1_real=w1_real.astype(BF16), g1=g1.astype(BF16),
                f2=f2.astype(BF16), g2=g2.astype(BF16))


def _strided_rows(ref, start, size):
    parts = [ref[cc, pl.ds(start, size, stride=2 * SUBLANE), :] for cc in range(ref.shape[0])]
    return parts[0] if len(parts) == 1 else jnp.concatenate(parts, axis=1)


def _store_chunks(ref, row0, val):
    for cc in range(ref.shape[0]):
        ref[cc, pl.ds(row0, SUBLANE), :] = val[:, cc * LANE:(cc + 1) * LANE]


def _stage_a_kernel(*refs, n2t, n1, has_inv, has_fwd, has_div):
    it = iter(refs)
    y_ref = next(it) if has_inv else None
    g1_ref = next(it) if has_inv else None
    u_ref = next(it)
    mul_ref = next(it) if has_inv else None
    skip_ref = next(it) if has_inv else None
    div_ref = next(it) if has_div else None
    w1_ref = next(it) if has_fwd else None
    e_ref = next(it) if has_inv else None
    a_ref = next(it) if has_fwd else None
    for j in range(n2t):
        u = u_ref[j]
        if has_inv:
            nb, jl = j // SUBLANE, j % SUBLANE
            yre = _strided_rows(y_ref.at[nb], jl, n1)
            yim = _strided_rows(y_ref.at[nb], SUBLANE + jl, n1)
            rhs = jnp.concatenate([yre, yim], axis=0).astype(BF16)
            yy = jnp.dot(g1_ref[...], rhs, preferred_element_type=F32)
            uf = u.astype(F32)
            e = mul_ref[j].astype(F32) * (yy + uf * skip_ref[...])
            e_ref[j] = e.astype(e_ref.dtype)
            src = e.astype(BF16)
        elif has_div:
            src = (u / div_ref[...]).astype(BF16)
        else:
            src = u.astype(BF16)
        if has_fwd:
            a = jnp.dot(w1_ref[...], src, preferred_element_type=F32)
            for ri in range(2):
                for kt in range(n1 // SUBLANE):
                    _store_chunks(a_ref.at[kt], (j * 2 + ri) * SUBLANE,
                                  a[ri * n1 + kt * SUBLANE: ri * n1 + (kt + 1) * SUBLANE])


def _stage_a(tabs, nh, u, *, y=None, mul=None, skip=None, div=None, w1=None, want_fwd=True,
             ct=256, n2t=16, e_dtype=BF16, c_off=0):
    p, _, rows, c = u.shape
    n1 = 2 * nh
    has_inv = y is not None
    has_div = div is not None
    ct = min(ct, c)
    nbk = n2t // SUBLANE
    grid = (p, c // ct, N2 // n2t)
    args, specs = [], []
    if has_inv:
        args += [y, tabs['g1']]
        specs += [pl.BlockSpec((None, nbk, ct // LANE, n1 * 2 * SUBLANE, LANE),
                               lambda pi, ci, ni: (pi, ni, ci, 0, 0)),
                  pl.BlockSpec(tabs['g1'].shape, lambda pi, ci, ni: (0, 0))]
    args.append(u)
    specs.append(pl.BlockSpec((None, n2t, rows, ct), lambda pi, ci, ni: (pi, ni, 0, ci)))
    if has_inv:
        args += [mul, skip]
        specs += [pl.BlockSpec((None, n2t, rows, ct), lambda pi, ci, ni: (pi, ni, 0, ci)),
                  pl.BlockSpec((1, ct), lambda pi, ci, ni: (0, ci + c_off // ct))]
    if has_div:
        args.append(div)
        specs.append(pl.BlockSpec((1, ct), lambda pi, ci, ni: (0, ci)))
    if want_fwd:
        args.append(w1)
        specs.append(pl.BlockSpec(w1.shape, lambda pi, ci, ni: (0, 0)))
    out_shape, out_specs = [], []
    if has_inv:
        out_shape.append(jax.ShapeDtypeStruct(u.shape, e_dtype))
        out_specs.append(pl.BlockSpec((None, n2t, rows, ct), lambda pi, ci, ni: (pi, ni, 0, ci)))
    if want_fwd:
        out_shape.append(jax.ShapeDtypeStruct((p, n1 // SUBLANE, c // LANE, N2 * 2 * SUBLANE, LANE), F32))
        out_specs.append(pl.BlockSpec((None, n1 // SUBLANE, ct // LANE, n2t * 2 * SUBLANE, LANE),
                                      lambda pi, ci, ni: (pi, 0, ci, ni, 0)))
    outs = pl.pallas_call(
        functools.partial(_stage_a_kernel, n2t=n2t, n1=n1, has_inv=has_inv, has_fwd=want_fwd,
                          has_div=has_div),
        grid=grid, in_specs=specs, out_specs=out_specs, out_shape=out_shape,
        compiler_params=_cparams(("parallel", "parallel", "parallel")),
        name="hyena_stage_a" + ("_inv" if has_inv else "") + ("_fwd" if want_fwd else ""),
    )(*args)
    return outs


def _stage_b_kernel(*refs, filt_only):
    if filt_only:
        a_ref, f2_ref, o_ref = refs
    else:
        a_ref, f2_ref, g2_ref, kf_ref, o_ref = refs
    for kl in range(SUBLANE):
        are = _strided_rows(a_ref, kl, N2)
        aim = _strided_rows(a_ref, SUBLANE + kl, N2)
        rhs = jnp.concatenate([are, aim], axis=0).astype(BF16)
        t = jnp.dot(f2_ref[kl], rhs, preferred_element_type=F32)
        if filt_only:
            o_ref[kl] = t
            continue
        kf = kf_ref[kl]
        tre, tim = t[:N2], t[N2:]
        kre, kim = kf[:N2], kf[N2:]
        z = jnp.concatenate([tre * kre - tim * kim, tre * kim + tim * kre], axis=0).astype(BF16)
        y = jnp.dot(g2_ref[kl], z, preferred_element_type=F32)
        for ri in range(2):
            for nt in range(N2 // SUBLANE):
                _store_chunks(o_ref.at[nt], (kl * 2 + ri) * SUBLANE,
                              y[ri * N2 + nt * SUBLANE: ri * N2 + (nt + 1) * SUBLANE])


def _stage_b(tabs, a, kf=None, *, c_off=0, ct=512):
    p, nkt, ncc, _, _ = a.shape
    c = ncc * LANE
    n1 = nkt * SUBLANE
    ct = min(ct, c)
    filt_only = kf is None
    grid = (nkt, c // ct, p)
    a_spec = pl.BlockSpec((None, None, ct // LANE, N2 * 2 * SUBLANE, LANE),
                          lambda kt, ci, pi: (pi, kt, ci, 0, 0))
    tab_spec = pl.BlockSpec((SUBLANE, 2 * N2, 2 * N2), lambda kt, ci, pi: (kt, 0, 0))
    if filt_only:
        return pl.pallas_call(
            functools.partial(_stage_b_kernel, filt_only=True),
            grid=grid, in_specs=[a_spec, tab_spec],
            out_specs=pl.BlockSpec((SUBLANE, 2 * N2, ct), lambda kt, ci, pi: (kt, 0, ci)),
            out_shape=jax.ShapeDtypeStruct((n1, 2 * N2, c), F32),
            compiler_params=_cparams(("parallel", "parallel", "parallel")),
            name="hyena_stage_b_filter",
        )(a, tabs['f2'])
    return pl.pallas_call(
        functools.partial(_stage_b_kernel, filt_only=False),
        grid=grid,
        in_specs=[a_spec, tab_spec, tab_spec,
                  pl.BlockSpec((SUBLANE, 2 * N2, ct), lambda kt, ci, pi: (kt, 0, ci + c_off // ct))],
        out_specs=pl.BlockSpec((None, N2 // SUBLANE, ct // LANE, 2 * SUBLANE * SUBLANE, LANE),
                               lambda kt, ci, pi: (pi, 0, ci, kt, 0)),
        out_shape=jax.ShapeDtypeStruct((p, N2 // SUBLANE, c // LANE, n1 * 2 * SUBLANE, LANE), F32),
        compiler_params=_cparams(("parallel", "parallel", "parallel")),
        name="hyena_stage_b",
    )(a, tabs['f2'], tabs['g2'], kf)


def _permute_seq(a, nh):
    b, _, c = a.shape
    return a.reshape(b // 2, 2, nh, N2, c).transpose(0, 3, 1, 2, 4).reshape(b // 2, N2, 2 * nh, c)


def _unpermute_seq(a, nh):
    p, _, _, c = a.shape
    return a.reshape(p, N2, 2, nh, c).transpose(0, 2, 3, 1, 4).reshape(2 * p, nh * N2, c)


def _filter_spectrum(tabs, nh, k_un, ssum):
    n, call = k_un.shape
    kp = k_un.reshape(1, N2, 2 * nh, call)
    (a,) = _stage_a(tabs, nh, kp, div=ssum.reshape(1, call), w1=tabs['w1_real'], ct=512, n2t=16)
    return _stage_b(tabs, a)


def _hyena_long(tabs, nh, kf, layer, v, x1, m2, skip):
    vp, x1p, m2p = (_permute_seq(t, nh) for t in (v, x1, m2))
    c0 = layer * HYENA_ORDER * W_HYENA
    skip2 = skip.reshape(1, HYENA_ORDER * W_HYENA)
    (a1,) = _stage_a(tabs, nh, vp, w1=tabs['w1_pair'])
    y1 = _stage_b(tabs, a1, kf, c_off=c0)
    z, a2 = _stage_a(tabs, nh, vp, y=y1, mul=x1p, skip=skip2, w1=tabs['w1_pair'], c_off=0)
    y2 = _stage_b(tabs, a2, kf, c_off=c0 + W_HYENA)
    (gp,) = _stage_a(tabs, nh, z, y=y2, mul=m2p, skip=skip2, want_fwd=False, c_off=W_HYENA)
    return _unpermute_seq(gp, nh)


def _ctx_conv_kernel(v_ref, x1_ref, m2_ref, k_ref, s_ref, skip_ref, ff_ref, fk_ref, gi_ref, o_ref,
                     *, n):
    hp = lax.Precision.HIGHEST

    def conv(u, o):
        kfull = k_ref[:, o * W_HYENA:(o + 1) * W_HYENA] / s_ref[:, o * W_HYENA:(o + 1) * W_HYENA]
        kf = jnp.dot(fk_ref[...], kfull, preferred_element_type=F32, precision=hp)
        uf = jnp.dot(ff_ref[...], u, preferred_element_type=F32, precision=hp)
        ure, uim, kre, kim = uf[:n], uf[n:], kf[:n], kf[n:]
        z = jnp.concatenate([ure * kre - uim * kim, ure * kim + uim * kre], axis=0)
        y = jnp.dot(gi_ref[...], z, preferred_element_type=F32, precision=hp)
        return y + u * skip_ref[o:o + 1, :]

    v = v_ref[...].astype(F32)
    z1 = x1_ref[...].astype(F32) * conv(v, 0)
    o_ref[...] = (m2_ref[...].astype(F32) * conv(z1, 1)).astype(o_ref.dtype)


def _ctx_tables(lc):
    n = 2 * lc
    k = jnp.arange(n, dtype=jnp.int32)
    ph = _angles(k[:, None], k[None, :], n)
    c, s = jnp.cos(ph), jnp.sin(ph)
    fk = jnp.concatenate([c, -s], axis=0)
    ff = fk[:, :lc]
    gi = jnp.concatenate([c[:lc], -s[:lc]], axis=1) / n
    return ff, fk, gi


def _ctx_hyena(ctabs, k_un, ssum, layer, v, x1, m2, skip):
    b, lc, c = v.shape
    n = 2 * lc
    ff, fk, gi = ctabs
    wc = HYENA_ORDER * W_HYENA
    tok = pl.BlockSpec((None, lc, c), lambda bi: (bi, 0, 0))
    return pl.pallas_call(
        functools.partial(_ctx_conv_kernel, n=n),
        grid=(b,),
        in_specs=[tok, tok, tok,
                  pl.BlockSpec((n, wc), lambda bi: (0, layer)),
                  pl.BlockSpec((None, 1, wc), lambda bi: (layer, 0, 0)),
                  pl.BlockSpec((HYENA_ORDER, c), lambda bi: (0, 0)),
                  _full_spec(ff), _full_spec(fk), _full_spec(gi)],
        out_specs=tok,
        out_shape=jax.ShapeDtypeStruct((b, lc, c), BF16),
        compiler_params=_cparams(("parallel",)),
        name="ctx_hyena",
    )(v, x1, m2, k_un, ssum, skip, ff, fk, gi)


def _merge_kernel(x_ref, a_ref, gh_ref, att_ref, sm_ref, g_ref, sc_ref, sh_ref, gt_ref,
                  wg_ref, bg_ref, sco_ref, hyo_ref, mlo_ref, wo_ref, fg_ref, o_ref, *, final):
    x = x_ref[...]
    d = x.shape[-1]
    h = _mod_norm(x, g_ref[...], sc_ref[...], sh_ref[...])
    gates = jax.nn.sigmoid(jnp.dot(h.astype(BF16), wg_ref[...], preferred_element_type=F32) + bg_ref[...])
    ya = jnp.dot(a_ref[...], sco_ref[...], preferred_element_type=F32)
    yh = jnp.dot(gh_ref[...], hyo_ref[...], preferred_element_type=F32)
    am = (sm_ref[...].astype(F32) * att_ref[...].astype(F32)).astype(BF16)
    ym = jnp.dot(am, mlo_ref[...], preferred_element_type=F32)
    y = gates[:, 0:d] * ya + gates[:, d:2 * d] * yh + gates[:, 2 * d:3 * d] * ym
    o = jnp.dot(y.astype(BF16), wo_ref[...], preferred_element_type=F32)
    xn = x + gt_ref[...] * o
    if final:
        xn = _rms(xn, fg_ref[...])
    o_ref[...] = xn


def _merge_call(x2d, seq_len, mods, g, lw, a, gh, att, sm, final_g, final):
    n_rows, d = x2d.shape
    tm = min(256, seq_len)
    tiles_per_seq = seq_len // tm
    scale, shift, gate = mods
    g2 = g.reshape(1, 1, d)
    fg = final_g.reshape(1, d)
    tok = lambda wd: pl.BlockSpec((tm, wd), lambda i: (i, 0))
    consts = [lw['wg'], lw['bg'], lw['sc_out'], lw['hy_out'], lw['mla_out'], lw['w_o'], fg]
    in_specs = [tok(d), tok(W_CONV), tok(W_HYENA), tok(W_MLA), tok(W_MLA),
                _mod_spec(g2, tiles_per_seq), _mod_spec(scale, tiles_per_seq),
                _mod_spec(shift, tiles_per_seq), _mod_spec(gate, tiles_per_seq)]
    in_specs += [_full_spec(c) for c in consts]
    return pl.pallas_call(
        functools.partial(_merge_kernel, final=final),
        grid=(n_rows // tm,),
        in_specs=in_specs,
        out_specs=tok(d),
        out_shape=jax.ShapeDtypeStruct((n_rows, d), F32),
        compiler_params=_cparams(("parallel",)),
        name="merge_final" if final else "merge",
    )(x2d, a, gh, att, sm, g2, scale, shift, gate, *consts)


def _rope_swap_cols(w):
    half = QK_ROPE // 2
    return jnp.concatenate([-w[..., half:], w[..., :half]], axis=-1)


def _layer_weights(i, w_in, b_in, mla_q_norm, mla_w_uq, mla_kv_norm, mla_w_ukv, sc_out, hy_out,
                   mla_out, w_o):
    wi, bi = w_in[i], b_in[i]
    d = wi.shape[0]
    lw = {}
    lw['w1'] = jnp.concatenate([wi[:, O_XIN:O_HPROJ], wi[:, O_ZM:O_GATES]], axis=1).astype(BF16)
    lw['b1'] = jnp.concatenate([bi[O_XIN:O_HPROJ], bi[O_ZM:O_GATES]])[None, :]
    lw['w2'] = wi[:, O_HPROJ:O_CQ].astype(BF16)
    lw['b2'] = bi[O_HPROJ:O_CQ][None, :]
    wkr, bkr = wi[:, O_KR:O_ZM], bi[O_KR:O_ZM]
    zw = lambda n: jnp.zeros((d, n), F32)
    zb = lambda n: jnp.zeros((n,), F32)
    lw['w3'] = jnp.concatenate([wi[:, O_CQ:O_KR], wkr, zw(96), _rope_swap_cols(wkr), zw(96)], axis=1).astype(BF16)
    lw['b3'] = jnp.concatenate([bi[O_CQ:O_KR], bkr, zb(96), _rope_swap_cols(bkr), zb(96)])[None, :]
    lw['qg'] = mla_q_norm[i][None, :]
    lw['kvg'] = mla_kv_norm[i][None, :]
    wuq = mla_w_uq[i].reshape(Q_LORA, N_HEADS, QK_NOPE + QK_ROPE)
    pad = jnp.zeros((Q_LORA, N_HEADS, HEAD_PAD - QK_NOPE - QK_ROPE), F32)
    lw['wq'] = jnp.concatenate([wuq, pad], axis=-1).reshape(Q_LORA, N_HEADS * HEAD_PAD).astype(BF16)
    zn = jnp.zeros((Q_LORA, N_HEADS, QK_NOPE), F32)
    lw['wqs'] = jnp.concatenate([zn, _rope_swap_cols(wuq[..., QK_NOPE:]), pad], axis=-1).reshape(
        Q_LORA, N_HEADS * HEAD_PAD).astype(BF16)
    wukv = mla_w_ukv[i].reshape(KV_LORA, N_HEADS, QK_NOPE + V_HEAD)
    padk = jnp.zeros((KV_LORA, N_HEADS, HEAD_PAD - QK_NOPE), F32)
    lw['wk'] = jnp.concatenate([wukv[..., :QK_NOPE], padk], axis=-1).reshape(
        KV_LORA, N_HEADS * HEAD_PAD).astype(BF16)
    padv = jnp.zeros((KV_LORA, N_HEADS, HEAD_PAD - V_HEAD), F32)
    lw['wv'] = jnp.concatenate([wukv[..., QK_NOPE:], padv], axis=-1).reshape(
        KV_LORA, N_HEADS * HEAD_PAD).astype(BF16)
    lw['vb'] = jnp.zeros((N_HEADS, HEAD_PAD), F32).at[:, V_HEAD].set(1.0).reshape(1, N_HEADS * HEAD_PAD)
    e2 = jnp.zeros((QK_ROPE, N_HEADS, HEAD_PAD), F32).at[:, :, QK_NOPE:QK_NOPE + QK_ROPE].set(
        jnp.eye(QK_ROPE, dtype=F32)[:, None, :])
    lw['e2'] = e2.reshape(QK_ROPE, N_HEADS * HEAD_PAD).astype(BF16)
    lw['wg'] = wi[:, O_GATES:].astype(BF16)
    lw['bg'] = bi[O_GATES:][None, :]
    lw['sc_out'] = sc_out[i].astype(BF16)
    lw['hy_out'] = hy_out[i].astype(BF16)
    lw['mla_out'] = mla_out[i].astype(BF16)
    lw['w_o'] = w_o[i].astype(BF16)
    return lw


def _rope_tables(seq_len, use_rope):
    scale = (QK_NOPE + QK_ROPE) ** -0.5 * math.log2(math.e)
    if use_rope:
        rows = seq_len // GRID_W
        row = jnp.broadcast_to(jnp.arange(rows, dtype=F32)[:, None], (rows, GRID_W)).reshape(seq_len)
        col = jnp.broadcast_to(jnp.arange(GRID_W, dtype=F32)[None, :], (rows, GRID_W)).reshape(seq_len)
        n_f = QK_ROPE // 4
        inv = ROPE_BASE ** (-jnp.arange(n_f, dtype=F32) / n_f)
        ang = jnp.concatenate([row[:, None] * inv, col[:, None] * inv], axis=-1)
        cos, sin = jnp.cos(ang), jnp.sin(ang)
    else:
        cos = jnp.ones((seq_len, QK_ROPE // 2), F32)
        sin = jnp.zeros((seq_len, QK_ROPE // 2), F32)
    cosk = jnp.concatenate([cos, cos], axis=-1)
    sink = jnp.concatenate([sin, sin], axis=-1)
    ones = jnp.ones((seq_len, QK_NOPE), F32)
    zeros = jnp.zeros((seq_len, QK_NOPE), F32)
    tail = jnp.zeros((seq_len, HEAD_PAD - QK_NOPE - QK_ROPE), F32)
    cosq = jnp.tile(jnp.concatenate([ones, cosk, tail], axis=-1), (1, N_HEADS)) * scale
    sinq = jnp.tile(jnp.concatenate([zeros, sink, tail], axis=-1), (1, N_HEADS)) * scale
    return cosq, sinq, cosk, sink


def kernel(x, c, ctx, c_ctx, ada_w, ada_b, norm_g, w_in, b_in, sc_conv_w, sc_conv_b, sc_out, hy_conv_w,
           hy_conv_b, hy_w1, hy_b1, hy_w2, hy_b2, hy_w3, hy_freq, hy_skip, hy_out, mla_q_norm, mla_w_uq,
           mla_kv_norm, mla_w_ukv, mla_out, w_o, final_g):
    bsz, seq, d = x.shape
    lc = ctx.shape[1]
    depth = ada_w.shape[0]
    nh = seq // N2
    assert d == D_MODEL and bsz % 2 == 0 and nh % SUBLANE == 0 and bsz <= 7

    cc = jnp.zeros((8, d), F32).at[:bsz].set(c).at[bsz].set(c_ctx)
    mods = _ada_mods(cc, ada_w, ada_b)

    tabs = _dft_tables(nh)
    k_un, ssum = _filter_gen(seq, depth, True, hy_w1, hy_b1, hy_w2, hy_b2, hy_w3, hy_freq)
    kf = _filter_spectrum(tabs, nh, k_un, ssum)
    if depth > 1:
        kc_un, sc_sum = _filter_gen(lc, depth - 1, False, hy_w1, hy_b1, hy_w2, hy_b2, hy_w3, hy_freq)
        ctabs = _ctx_tables(lc)

    rope_l = _rope_tables(seq, True)
    rope_c = _rope_tables(lc, False)

    x_lat = x.reshape(bsz * seq, d)
    x_ctx = ctx.reshape(bsz * lc, d)
    for i in range(depth):
        last = i == depth - 1
        lw = _layer_weights(i, w_in, b_in, mla_q_norm, mla_w_uq, mla_kv_norm, mla_w_ukv, sc_out,
                            hy_out, mla_out, w_o)
        m = mods[i]
        split = lambda r: tuple(r[:, None, j * d:(j + 1) * d] for j in range(3))
        shift_l, scale_l, gate_l = split(m[:bsz])
        shift_c, scale_c, gate_c = split(m[bsz:bsz + 1])
        g = norm_g[i]
        scw, scb = sc_conv_w[i], sc_conv_b[i][None, :]
        hcw, hcb = hy_conv_w[i], hy_conv_b[i][None, :]

        hq = N_HEADS * HEAD_PAD
        q_l, k_all, v_all = _kp3_call(x_lat, bsz, seq, seq + lc, 0, (scale_l, shift_l), g, lw, rope_l)
        q_c, k_all, v_all = _kp3_call(x_ctx, bsz, lc, seq + lc, seq, (scale_c, shift_c), g, lw, rope_c,
                                      kv_bufs=(k_all, v_all))
        att_l = _attention(q_l.reshape(bsz, seq, hq), k_all, v_all).reshape(bsz * seq, W_MLA)

        a_l, sm_l = _kp_conv_call(_kp1_kernel, "kp1_conv", x_lat, seq, (scale_l, shift_l), g,
                                  lw['w1'], lw['b1'], scw, scb, (W_CONV, W_MLA))
        v_h, x1_h, m2_h = _kp_conv_call(_kp2_kernel, "kp2_hyena", x_lat, seq, (scale_l, shift_l), g,
                                        lw['w2'], lw['b2'], hcw, hcb, (W_HYENA,) * 3)
        r3 = lambda t: t.reshape(bsz, seq, W_HYENA)
        gh_l = _hyena_long(tabs, nh, kf, i, r3(v_h), r3(x1_h), r3(m2_h), hy_skip[i])
        new_lat = _merge_call(x_lat, seq, (scale_l, shift_l, gate_l), g, lw, a_l,
                              gh_l.reshape(bsz * seq, W_HYENA), att_l, sm_l, final_g, last)
        if not last:
            att_c = _attention(q_c.reshape(bsz, lc, hq), k_all, v_all, kv_row0=seq,
                               kv_len=lc).reshape(bsz * lc, W_MLA)
            a_c, sm_c = _kp_conv_call(_kp1_kernel, "kp1_conv_ctx", x_ctx, lc, (scale_c, shift_c), g,
                                      lw['w1'], lw['b1'], scw, scb, (W_CONV, W_MLA))
            v_hc, x1_hc, m2_hc = _kp_conv_call(_kp2_kernel, "kp2_hyena_ctx", x_ctx, lc,
                                               (scale_c, shift_c), g, lw['w2'], lw['b2'], hcw, hcb,
                                               (W_HYENA,) * 3)
            rc = lambda t: t.reshape(bsz, lc, W_HYENA)
            gh_c = _ctx_hyena(ctabs, kc_un, sc_sum, i, rc(v_hc), rc(x1_hc), rc(m2_hc), hy_skip[i])
            x_ctx = _merge_call(x_ctx, lc, (scale_c, shift_c, gate_c), g, lw, a_c,
                                gh_c.reshape(bsz * lc, W_HYENA), att_c, sm_c, final_g, False)
        x_lat = new_lat
    return x_lat.reshape(bsz, seq, d)
```

```python
import functools
import math

import jax
import jax.numpy as jnp
from jax import lax
from jax.experimental import pallas as pl
from jax.experimental.pallas import tpu as pltpu

F32 = jnp.float32
BF16 = jnp.bfloat16

D_MODEL = 1024
DEPTH = 4
GRID_W = 64
W_CONV = 512
W_HYENA = 512
HYENA_ORDER = 2
HYENA_EMB = 33
HYENA_BANDS = (HYENA_EMB - 1) // 2
HYENA_HID = 64
HYENA_FAST_DECAY = 0.3
HYENA_SLOW_DECAY = 1.5
HYENA_TARGET = 1e-2
N_HEADS = 8
QK_NOPE = 64
QK_ROPE = 32
V_HEAD = 64
Q_LORA = 384
KV_LORA = 256
W_MLA = N_HEADS * V_HEAD
ROPE_BASE = 10000.0
N_BRANCH = 3
EPS = 1e-6

O_XIN, O_GB, O_GC, O_ZA = 0, 512, 1024, 1536
O_HPROJ, O_ZH = 2048, 3584
O_CQ, O_CKV, O_KR, O_ZM, O_GATES = 4096, 4480, 4736, 4768, 5280
N_IN = 8352

LANE = 128
SUBLANE = 8
HEAD_PAD = 128
N2 = 128
VMEM_LIMIT = 56 * 1024 * 1024


def _cparams(sem):
    return pltpu.CompilerParams(dimension_semantics=sem, vmem_limit_bytes=VMEM_LIMIT)


def _ada_kernel(c_ref, w_ref, b_ref, o_ref):
    c = c_ref[...]
    s = c * jax.nn.sigmoid(c)
    o_ref[...] = jnp.dot(s, w_ref[...], preferred_element_type=F32,
                         precision=lax.Precision.HIGHEST) + b_ref[...]


def _ada_mods(cc, ada_w, ada_b):
    depth = ada_w.shape[0]
    d = cc.shape[1]
    return pl.pallas_call(
        _ada_kernel,
        grid=(depth, 3),
        in_specs=[pl.BlockSpec((8, d), lambda l, j: (0, 0)),
                  pl.BlockSpec((None, d, d), lambda l, j: (l, 0, j)),
                  pl.BlockSpec((None, 1, d), lambda l, j: (l, 0, j))],
        out_specs=pl.BlockSpec((None, 8, d), lambda l, j: (l, 0, j)),
        out_shape=jax.ShapeDtypeStruct((depth, 8, 3 * d), F32),
        compiler_params=_cparams(("parallel", "parallel")),
        name="ada_mods",
    )(cc, ada_w, ada_b.reshape(depth, 1, 3 * d))


def _mod_norm(x, g, scale, shift):
    y = x * lax.rsqrt(jnp.mean(x * x, axis=-1, keepdims=True) + EPS)
    return (y * g) * (1.0 + scale) + shift


def _rms(x, g):
    return x * lax.rsqrt(jnp.mean(x * x, axis=-1, keepdims=True) + EPS) * g


def _silu(x):
    return x * jax.nn.sigmoid(x)


def _dwconv3_ext(u_ext, w, b, first, last, tm):
    n = tm + 2 * SUBLANE
    prev = pltpu.roll(u_ext, 1, 0)[SUBLANE:SUBLANE + tm]
    nxt = pltpu.roll(u_ext, n - 1, 0)[SUBLANE:SUBLANE + tm]
    row = lax.broadcasted_iota(jnp.int32, (tm, 1), 0)
    prev = jnp.where(jnp.logical_and(first, row == 0), 0.0, prev)
    nxt = jnp.where(jnp.logical_and(last, row == tm - 1), 0.0, nxt)
    return prev * w[0:1] + u_ext[SUBLANE:SUBLANE + tm] * w[1:2] + nxt * w[2:3] + b


def _proj_ext(xm_ref, xp_ref, xn_ref, g_ref, sc_ref, sh_ref, w_ref, b_ref):
    x_ext = jnp.concatenate([xp_ref[...], xm_ref[...], xn_ref[...]], axis=0)
    h = _mod_norm(x_ext, g_ref[...], sc_ref[...], sh_ref[...])
    return jnp.dot(h.astype(BF16), w_ref[...], preferred_element_type=F32) + b_ref[...]


def _kp1_kernel(xm_ref, xp_ref, xn_ref, g_ref, sc_ref, sh_ref, w_ref, b_ref, cw_ref, cb_ref,
                a_ref, sm_ref, *, tm, tiles_per_seq):
    i = pl.program_id(0)
    first = (i % tiles_per_seq) == 0
    last = (i % tiles_per_seq) == tiles_per_seq - 1
    p = _proj_ext(xm_ref, xp_ref, xn_ref, g_ref, sc_ref, sh_ref, w_ref, b_ref)
    prod = p[:, 2 * W_CONV:3 * W_CONV] * p[:, 0:W_CONV]
    conv = _dwconv3_ext(prod, cw_ref[...], cb_ref[...], first, last, tm)
    pm = p[SUBLANE:SUBLANE + tm]
    a = _silu(pm[:, 3 * W_CONV:4 * W_CONV]) * (pm[:, W_CONV:2 * W_CONV] * conv)
    a_ref[...] = a.astype(a_ref.dtype)
    sm_ref[...] = _silu(pm[:, 4 * W_CONV:4 * W_CONV + W_MLA]).astype(sm_ref.dtype)


def _kp2_kernel(xm_ref, xp_ref, xn_ref, g_ref, sc_ref, sh_ref, w_ref, b_ref, cw_ref, cb_ref,
                v_ref, x1_ref, m2_ref, *, tm, tiles_per_seq):
    i = pl.program_id(0)
    first = (i % tiles_per_seq) == 0
    last = (i % tiles_per_seq) == tiles_per_seq - 1
    p = _proj_ext(xm_ref, xp_ref, xn_ref, g_ref, sc_ref, sh_ref, w_ref, b_ref)
    u = _dwconv3_ext(p[:, 0:3 * W_HYENA], cw_ref[...], cb_ref[...], first, last, tm)
    zh = p[SUBLANE:SUBLANE + tm, 3 * W_HYENA:4 * W_HYENA]
    v_ref[...] = u[:, 0:W_HYENA].astype(v_ref.dtype)
    x1_ref[...] = u[:, W_HYENA:2 * W_HYENA].astype(x1_ref.dtype)
    m2_ref[...] = (_silu(zh) * u[:, 2 * W_HYENA:3 * W_HYENA]).astype(m2_ref.dtype)


def _kp3_kernel(xm_ref, g_ref, sc_ref, sh_ref, w_ref, b_ref, qg_ref, kvg_ref, wq_ref, wqs_ref,
                wk_ref, wv_ref, vb_ref, e2_ref, cosq_ref, sinq_ref, cosk_ref, sink_ref,
                q_ref, k_ref, v_ref):
    h = _mod_norm(xm_ref[...], g_ref[...], sc_ref[...], sh_ref[...])
    p = jnp.dot(h.astype(BF16), w_ref[...], preferred_element_type=F32) + b_ref[...]
    cq = _rms(p[:, 0:Q_LORA], qg_ref[...]).astype(BF16)
    ckv = _rms(p[:, Q_LORA:Q_LORA + KV_LORA], kvg_ref[...]).astype(BF16)
    kr = p[:, 640:640 + QK_ROPE]
    krs = p[:, 768:768 + QK_ROPE]
    qa = jnp.dot(cq, wq_ref[...], preferred_element_type=F32)
    qb = jnp.dot(cq, wqs_ref[...], preferred_element_type=F32)
    q_ref[...] = (qa * cosq_ref[...] + qb * sinq_ref[...]).astype(q_ref.dtype)
    kr_rot = (kr * cosk_ref[...] + krs * sink_ref[...]).astype(BF16)
    kn = jnp.dot(ckv, wk_ref[...], preferred_element_type=F32)
    krp = jnp.dot(kr_rot, e2_ref[...], preferred_element_type=F32)
    k_ref[...] = (kn + krp).astype(k_ref.dtype)
    v_ref[...] = (jnp.dot(ckv, wv_ref[...], preferred_element_type=F32) + vb_ref[...]).astype(v_ref.dtype)


def _mod_spec(mod, tiles_per_seq):
    d = mod.shape[-1]
    if mod.shape[0] == 1:
        return pl.BlockSpec((None, 1, d), lambda i: (0, 0, 0))
    return pl.BlockSpec((None, 1, d), lambda i: (i // tiles_per_seq, 0, 0))


def _full_spec(a):
    nd = a.ndim
    return pl.BlockSpec(a.shape, lambda i: (0,) * nd)


def _halo_specs(n_rows, tm, d):
    r = tm // SUBLANE
    nb = n_rows // SUBLANE
    return [pl.BlockSpec((tm, d), lambda i: (i, 0)),
            pl.BlockSpec((SUBLANE, d), lambda i: (jnp.maximum(i * r - 1, 0), 0)),
            pl.BlockSpec((SUBLANE, d), lambda i: (jnp.minimum((i + 1) * r, nb - 1), 0))]


def _kp_conv_call(kern, name, x2d, seq_len, mods, g, w, b, cw, cb, out_widths):
    n_rows, d = x2d.shape
    tm = min(512, seq_len)
    tiles_per_seq = seq_len // tm
    scale, shift = mods
    g2 = g.reshape(1, 1, d)
    in_specs = _halo_specs(n_rows, tm, d) + [
        _mod_spec(g2, tiles_per_seq), _mod_spec(scale, tiles_per_seq), _mod_spec(shift, tiles_per_seq),
        _full_spec(w), _full_spec(b), _full_spec(cw), _full_spec(cb)]
    return pl.pallas_call(
        functools.partial(kern, tm=tm, tiles_per_seq=tiles_per_seq),
        grid=(n_rows // tm,),
        in_specs=in_specs,
        out_specs=[pl.BlockSpec((tm, wd), lambda i: (i, 0)) for wd in out_widths],
        out_shape=[jax.ShapeDtypeStruct((n_rows, wd), BF16) for wd in out_widths],
        compiler_params=_cparams(("parallel",)),
        name=name,
    )(x2d, x2d, x2d, g2, scale, shift, w, b, cw, cb)


def _kp3_aliased_kernel(*refs):
    _kp3_kernel(*refs[:-5], *refs[-3:])


def _kp3_call(x2d, bsz, seq_len, kv_len, kv_row0, mods, g, lw, tabs, kv_bufs=None):
    n_rows, d = x2d.shape
    tm = min(512, seq_len)
    tiles_per_seq = seq_len // tm
    blk0 = kv_row0 // tm
    assert kv_row0 % tm == 0
    scale, shift = mods
    g2 = g.reshape(1, 1, d)
    cosq, sinq, cosk, sink = tabs
    consts = [lw['w3'], lw['b3'], lw['qg'], lw['kvg'], lw['wq'], lw['wqs'], lw['wk'], lw['wv'], lw['vb'],
              lw['e2']]

    def tab_spec(t):
        return pl.BlockSpec((tm, t.shape[1]), lambda i: (i % tiles_per_seq, 0))

    in_specs = [pl.BlockSpec((tm, d), lambda i: (i, 0)),
                _mod_spec(g2, tiles_per_seq), _mod_spec(scale, tiles_per_seq), _mod_spec(shift, tiles_per_seq)]
    in_specs += [_full_spec(a) for a in consts]
    in_specs += [tab_spec(t) for t in (cosq, sinq, cosk, sink)]
    args = [x2d, g2, scale, shift, *consts, cosq, sinq, cosk, sink]
    hq = N_HEADS * HEAD_PAD
    kv_spec = pl.BlockSpec((None, tm, hq), lambda i: (i // tiles_per_seq, blk0 + i % tiles_per_seq, 0))
    kv_shape = jax.ShapeDtypeStruct((bsz, kv_len, hq), BF16)
    aliases = {}
    kern = _kp3_kernel
    if kv_bufs is not None:
        kern = _kp3_aliased_kernel
        in_specs += [pl.BlockSpec(memory_space=pl.ANY)] * 2
        aliases = {len(args): 1, len(args) + 1: 2}
        args += list(kv_bufs)
    return pl.pallas_call(
        kern,
        grid=(n_rows // tm,),
        in_specs=in_specs,
        out_specs=[pl.BlockSpec((tm, hq), lambda i: (i, 0)), kv_spec, kv_spec],
        out_shape=[jax.ShapeDtypeStruct((n_rows, hq), BF16), kv_shape, kv_shape],
        input_output_aliases=aliases,
        compiler_params=_cparams(("parallel",)),
        name="kp3_mla" if kv_bufs is None else "kp3_mla_ctx",
    )(*args)


ATTN_CHAINS = 4


def _attn_kernel(q_ref, k_ref, v_ref, o_ref, s_ref, p_ref, m_ref, acc_ref):
    h = pl.program_id(2)
    ki = pl.program_id(3)

    @pl.when(ki == 0)
    def _():
        m_ref[...] = jnp.full(m_ref.shape, -jnp.inf, F32)
        acc_ref[...] = jnp.zeros(acc_ref.shape, F32)

    hq = q_ref.shape[0] // ATTN_CHAINS
    chains = [pl.ds(c * hq, hq) for c in range(ATTN_CHAINS)]
    for rows in chains:
        s_ref[rows, :] = lax.dot_general(q_ref[rows, :], k_ref[...], (((1,), (1,)), ((), ())),
                                         preferred_element_type=F32)
    alphas = []
    for rows in chains:
        m_prev = m_ref[rows, :]
        m_new = jnp.maximum(m_prev, jnp.max(s_ref[rows, :], axis=-1, keepdims=True))
        alphas.append(jnp.exp2(m_prev - m_new))
        m_ref[rows, :] = m_new
        p_ref[rows, :] = jnp.exp2((s_ref[rows, :] - m_new).astype(BF16))
    for rows, alpha in zip(chains, alphas):
        acc_ref[rows, :] = acc_ref[rows, :] * alpha + jnp.dot(p_ref[rows, :], v_ref[...],
                                                              preferred_element_type=F32)

    @pl.when(ki == pl.num_programs(3) - 1)
    def _():
        acc = acc_ref[...]
        o = (acc[:, 0:V_HEAD] / acc[:, V_HEAD:V_HEAD + 1]).astype(o_ref.dtype)

        @pl.when(h % 2 == 0)
        def _():
            o_ref[:, 0:V_HEAD] = o

        @pl.when(h % 2 == 1)
        def _():
            o_ref[:, V_HEAD:2 * V_HEAD] = o


MXU_DIM = 256
Q_TILE = 2048
KV_TILE_CAP = 2816


def _kv_tile(lk):
    for step in (MXU_DIM, LANE):
        cands = [t for t in range(step, min(lk, KV_TILE_CAP) + 1, step) if lk % t == 0]
        if cands:
            return cands[-1]
    return lk


def _attention(q, k, v, kv_row0=0, kv_len=None):
    b, lq, _ = q.shape
    lk = k.shape[1] if kv_len is None else kv_len
    tq = min(Q_TILE, lq)
    tk = _kv_tile(lk)
    assert kv_row0 % tk == 0
    kb0 = kv_row0 // tk
    grid = (b, lq // tq, N_HEADS, lk // tk)
    return pl.pallas_call(
        _attn_kernel,
        grid=grid,
        in_specs=[pl.BlockSpec((None, tq, HEAD_PAD), lambda bi, qi, h, ki: (bi, qi, h)),
                  pl.BlockSpec((None, tk, HEAD_PAD), lambda bi, qi, h, ki: (bi, kb0 + ki, h)),
                  pl.BlockSpec((None, tk, HEAD_PAD), lambda bi, qi, h, ki: (bi, kb0 + ki, h))],
        out_specs=pl.BlockSpec((None, tq, 2 * V_HEAD), lambda bi, qi, h, ki: (bi, qi, h // 2)),
        out_shape=jax.ShapeDtypeStruct((b, lq, W_MLA), BF16),
        scratch_shapes=[pltpu.VMEM((tq, tk), F32), pltpu.VMEM((tq, tk), BF16),
                        pltpu.VMEM((tq, 1), F32), pltpu.VMEM((tq, HEAD_PAD), F32)],
        compiler_params=_cparams(("parallel", "parallel", "arbitrary", "arbitrary")),
        name="mla_attention",
    )(q, k, v)


COL_BWD = HYENA_EMB
COL_DROP = HYENA_EMB + 1


def _filt_kernel(z_ref, w1_ref, b1_ref, w2_ref, b2_ref, fr_ref, w3_ref, dl_ref, k_ref, s_ref):
    i = pl.program_id(1)
    hp = lax.Precision.HIGHEST
    wc = HYENA_ORDER * W_HYENA
    z = z_ref[...]
    fr = fr_ref[...]
    hid = jnp.sin(fr * (jnp.dot(z, w1_ref[...], preferred_element_type=F32, precision=hp) + b1_ref[...]))
    hid = jnp.sin(fr * (jnp.dot(hid, w2_ref[...], preferred_element_type=F32, precision=hp) + b2_ref[...]))
    h2 = jnp.dot(hid.astype(BF16), w3_ref[...], preferred_element_type=F32)
    h = jnp.where(z[:, COL_BWD:COL_BWD + 1] > 0.5, h2[:, wc:], h2[:, :wc])
    t = z[:, 0:1]
    h = h * jnp.exp(-t * dl_ref[...])

    @pl.when(i == 0)
    def _():
        s_ref[...] = jnp.zeros(s_ref.shape, F32)

    s_ref[...] += jnp.sum(jnp.abs(h), axis=0, keepdims=True)
    k_ref[...] = jnp.where(z[:, COL_DROP:COL_DROP + 1] > 0.5, 0.0, h)


def _filter_tables(seq_len, permuted):
    t = jnp.linspace(0.0, 1.0, seq_len, dtype=F32)[:, None]
    w = 2.0 * math.pi * jnp.arange(seq_len, dtype=F32)[:, None] / seq_len
    f = jnp.linspace(1e-4, HYENA_BANDS - 1, HYENA_BANDS, dtype=F32)[None, :]
    z = jnp.concatenate([t, jnp.cos(f * w), -jnp.sin(f * w)], axis=-1)
    idx = jnp.concatenate([jnp.arange(seq_len), jnp.zeros((1,), jnp.int32),
                           seq_len - jnp.arange(1, seq_len)])
    n = jnp.arange(2 * seq_len)
    flags = jnp.stack([n >= seq_len, n == seq_len], axis=-1).astype(F32)
    zz = jnp.concatenate([z[idx], flags], axis=-1)
    zz = jnp.pad(zz, ((0, 0), (0, LANE - zz.shape[1])))
    if permuted:
        zz = zz.reshape(2 * seq_len // N2, N2, LANE).transpose(1, 0, 2).reshape(2 * seq_len, LANE)
    return zz


def _hyena_deltas():
    max_decay = math.log(HYENA_TARGET) / HYENA_FAST_DECAY
    min_decay = math.log(HYENA_TARGET) / HYENA_SLOW_DECAY
    deltas = jnp.abs(jnp.linspace(min_decay, max_decay, W_HYENA, dtype=F32))
    return jnp.tile(deltas, HYENA_ORDER)[None, :]


def _filter_gen(seq_len, n_layers, permuted, hy_w1, hy_b1, hy_w2, hy_b2, hy_w3, hy_freq):
    zz = _filter_tables(seq_len, permuted)
    tmf = min(512, seq_len)
    tiles_half = seq_len // tmf
    wc = HYENA_ORDER * W_HYENA
    w1p = jnp.pad(hy_w1[:n_layers], ((0, 0), (0, LANE - HYENA_EMB), (0, 0)))
    w3r = hy_w3[:n_layers].astype(BF16)
    r1 = lambda a: a[:n_layers].reshape(n_layers, 1, HYENA_HID)
    return pl.pallas_call(
        _filt_kernel,
        grid=(n_layers, 2 * tiles_half),
        in_specs=[pl.BlockSpec((tmf, LANE), lambda l, i: (i, 0)),
                  pl.BlockSpec((None, LANE, HYENA_HID), lambda l, i: (l, 0, 0)),
                  pl.BlockSpec((None, 1, HYENA_HID), lambda l, i: (l, 0, 0)),
                  pl.BlockSpec((None, HYENA_HID, HYENA_HID), lambda l, i: (l, 0, 0)),
                  pl.BlockSpec((None, 1, HYENA_HID), lambda l, i: (l, 0, 0)),
                  pl.BlockSpec((None, 1, HYENA_HID), lambda l, i: (l, 0, 0)),
                  pl.BlockSpec((None, HYENA_HID, 2 * wc), lambda l, i: (l, 0, 0)),
                  pl.BlockSpec((1, wc), lambda l, i: (0, 0))],
        out_specs=[pl.BlockSpec((tmf, wc), lambda l, i: (i, l)),
                   pl.BlockSpec((None, 1, wc), lambda l, i: (l, 0, 0))],
        out_shape=[jax.ShapeDtypeStruct((2 * seq_len, n_layers * wc), F32),
                   jax.ShapeDtypeStruct((n_layers, 1, wc), F32)],
        compiler_params=_cparams(("parallel", "arbitrary")),
        name="hyena_filter",
    )(zz, w1p, r1(hy_b1), hy_w2[:n_layers], r1(hy_b2), r1(hy_freq), w3r, _hyena_deltas())


def _angles(a, b, n):
    m = (a * b) % n
    return m.astype(F32) * (2.0 * math.pi / n)


def _dft_tables(nh):
    n1 = 2 * nh
    n = n1 * N2
    k1 = jnp.arange(n1, dtype=jnp.int32)
    th = _angles(k1[:, None], jnp.arange(nh, dtype=jnp.int32)[None, :], n1)
    c, s = jnp.cos(th), jnp.sin(th)
    w1_pair = jnp.concatenate([jnp.concatenate([c, s], 1), jnp.concatenate([-s, c], 1)], 0)
    thf = _angles(k1[:, None], k1[None, :], n1)
    w1_real = jnp.concatenate([jnp.cos(thf), -jnp.sin(thf)], 0)
    g1 = jnp.concatenate([jnp.concatenate([c.T, -s.T], 1), jnp.concatenate([s.T, c.T], 1)], 0) / n
    n2 = jnp.arange(N2, dtype=jnp.int32)
    kk = k1[:, None, None] + n1 * n2[None, :, None]
    ph = _angles(kk, n2[None, None, :], n)
    cp, sp = jnp.cos(ph), jnp.sin(ph)
    f2 = jnp.concatenate([jnp.concatenate([cp, sp], 2), jnp.concatenate([-sp, cp], 2)], 1)
    cpt, spt = cp.transpose(0, 2, 1), sp.transpose(0, 2, 1)
    g2 = jnp.concatenate([jnp.concatenate([cpt, -spt], 2), jnp.concatenate([spt, cpt], 2)], 1)
    g1 = g1.reshape(2 * nh, 2, n1).transpose(0, 2, 1).reshape(2 * nh, 2 * n1)
    f2 = f2.reshape(n1, 2 * N2, 2, N2).transpose(0, 1, 3, 2).reshape(n1, 2 * N2, 2 * N2)
    return dict(w1_pair=w1_pair.astype(BF16), w1_real=w1_real.astype(BF16), g1=g1.astype(BF16),
                f2=f2.astype(BF16), g2=g2.astype(BF16))


def _strided_rows(ref, start, size):
    parts = [ref[cc, pl.ds(start, size, stride=SUBLANE), :] for cc in range(ref.shape[0])]
    return parts[0] if len(parts) == 1 else jnp.concatenate(parts, axis=1)


def _pack_pair(re, im):
    lo = pltpu.bitcast(re.astype(BF16).astype(F32), jnp.uint32) >> 16
    hi = pltpu.bitcast(im.astype(BF16).astype(F32), jnp.uint32) & jnp.uint32(0xFFFF0000)
    return lo | hi


def _unpack_pairs(packed):
    return pltpu.bitcast(packed, BF16)


def _store_chunks(ref, row0, val):
    for cc in range(ref.shape[0]):
        ref[cc, pl.ds(row0, SUBLANE), :] = val[:, cc * LANE:(cc + 1) * LANE]


def _stage_a_kernel(*refs, n2t, n1, has_inv, has_fwd, has_div):
    it = iter(refs)
    y_ref = next(it) if has_inv else None
    g1_ref = next(it) if has_inv else None
    u_ref = next(it)
    mul_ref = next(it) if has_inv else None
    skip_ref = next(it) if has_inv else None
    div_ref = next(it) if has_div else None
    w1_ref = next(it) if has_fwd else None
    e_ref = next(it) if has_inv else None
    a_ref = next(it) if has_fwd else None
    for j in range(n2t):
        u = u_ref[j]
        if has_inv:
            nb, jl = j // SUBLANE, j % SUBLANE
            rhs = _unpack_pairs(_strided_rows(y_ref.at[nb], jl, n1))
            yy = jnp.dot(g1_ref[...], rhs, preferred_element_type=F32)
            uf = u.astype(F32)
            e = mul_ref[j].astype(F32) * (yy + uf * skip_ref[...])
            e_ref[j] = e.astype(e_ref.dtype)
            src = e.astype(BF16)
        elif has_div:
            src = (u / div_ref[...]).astype(BF16)
        else:
            src = u.astype(BF16)
        if has_fwd:
            a = jnp.dot(w1_ref[...], src, preferred_element_type=F32)
            packed = _pack_pair(a[:n1], a[n1:])
            for kt in range(n1 // SUBLANE):
                _store_chunks(a_ref.at[kt], j * SUBLANE, packed[kt * SUBLANE:(kt + 1) * SUBLANE])


def _stage_a(tabs, nh, u, *, y=None, mul=None, skip=None, div=None, w1=None, want_fwd=True,
             ct=256, n2t=16, e_dtype=BF16, c_off=0):
    p, _, rows, c = u.shape
    n1 = 2 * nh
    has_inv = y is not None
    has_div = div is not None
    ct = min(ct, c)
    nbk = n2t // SUBLANE
    grid = (p, c // ct, N2 // n2t)
    args, specs = [], []
    if has_inv:
        args += [y, tabs['g1']]
        specs += [pl.BlockSpec((None, nbk, ct // LANE, n1 * SUBLANE, LANE),
                               lambda pi, ci, ni: (pi, ni, ci, 0, 0)),
                  pl.BlockSpec(tabs['g1'].shape, lambda pi, ci, ni: (0, 0))]
    args.append(u)
    specs.append(pl.BlockSpec((None, n2t, rows, ct), lambda pi, ci, ni: (pi, ni, 0, ci)))
    if has_inv:
        args += [mul, skip]
        specs += [pl.BlockSpec((None, n2t, rows, ct), lambda pi, ci, ni: (pi, ni, 0, ci)),
                  pl.BlockSpec((1, ct), lambda pi, ci, ni: (0, ci + c_off // ct))]
    if has_div:
        args.append(div)
        specs.append(pl.BlockSpec((1, ct), lambda pi, ci, ni: (0, ci)))
    if want_fwd:
        args.append(w1)
        specs.append(pl.BlockSpec(w1.shape, lambda pi, ci, ni: (0, 0)))
    out_shape, out_specs = [], []
    if has_inv:
        out_shape.append(jax.ShapeDtypeStruct(u.shape, e_dtype))
        out_specs.append(pl.BlockSpec((None, n2t, rows, ct), lambda pi, ci, ni: (pi, ni, 0, ci)))
    if want_fwd:
        out_shape.append(jax.ShapeDtypeStruct((p, n1 // SUBLANE, c // LANE, N2 * SUBLANE, LANE), jnp.uint32))
        out_specs.append(pl.BlockSpec((None, n1 // SUBLANE, ct // LANE, n2t * SUBLANE, LANE),
                                      lambda pi, ci, ni: (pi, 0, ci, ni, 0)))
    outs = pl.pallas_call(
        functools.partial(_stage_a_kernel, n2t=n2t, n1=n1, has_inv=has_inv, has_fwd=want_fwd,
                          has_div=has_div),
        grid=grid, in_specs=specs, out_specs=out_specs, out_shape=out_shape,
        compiler_params=_cparams(("parallel", "parallel", "parallel")),
        name="hyena_stage_a" + ("_inv" if has_inv else "") + ("_fwd" if want_fwd else ""),
    )(*args)
    return outs


def _stage_b_kernel(*refs, filt_only):
    if filt_only:
        a_ref, f2_ref, o_ref = refs
    else:
        a_ref, f2_ref, g2_ref, kf_ref, o_ref = refs
    for kl in range(SUBLANE):
        rhs = _unpack_pairs(_strided_rows(a_ref, kl, N2))
        t = jnp.dot(f2_ref[kl], rhs, preferred_element_type=F32)
        if filt_only:
            o_ref[kl] = t.astype(o_ref.dtype)
            continue
        kf = kf_ref[kl].astype(F32)
        tre, tim = t[:N2], t[N2:]
        kre, kim = kf[:N2], kf[N2:]
        z = jnp.concatenate([tre * kre - tim * kim, tre * kim + tim * kre], axis=0).astype(BF16)
        y = jnp.dot(g2_ref[kl], z, preferred_element_type=F32)
        packed = _pack_pair(y[:N2], y[N2:])
        for nt in range(N2 // SUBLANE):
            _store_chunks(o_ref.at[nt], kl * SUBLANE, packed[nt * SUBLANE:(nt + 1) * SUBLANE])


def _stage_b(tabs, a, kf=None, *, c_off=0, ct=512):
    p, nkt, ncc, _, _ = a.shape
    c = ncc * LANE
    n1 = nkt * SUBLANE
    ct = min(ct, c)
    filt_only = kf is None
    grid = (nkt, c // ct, p)
    a_spec = pl.BlockSpec((None, None, ct // LANE, N2 * SUBLANE, LANE),
                          lambda kt, ci, pi: (pi, kt, ci, 0, 0))
    tab_spec = pl.BlockSpec((SUBLANE, 2 * N2, 2 * N2), lambda kt, ci, pi: (kt, 0, 0))
    if filt_only:
        return pl.pallas_call(
            functools.partial(_stage_b_kernel, filt_only=True),
            grid=grid, in_specs=[a_spec, tab_spec],
            out_specs=pl.BlockSpec((SUBLANE, 2 * N2, ct), lambda kt, ci, pi: (kt, 0, ci)),
            out_shape=jax.ShapeDtypeStruct((n1, 2 * N2, c), BF16),
            compiler_params=_cparams(("parallel", "parallel", "parallel")),
            name="hyena_stage_b_filter",
        )(a, tabs['f2'])
    return pl.pallas_call(
        functools.partial(_stage_b_kernel, filt_only=False),
        grid=grid,
        in_specs=[a_spec, tab_spec, tab_spec,
                  pl.BlockSpec((SUBLANE, 2 * N2, ct), lambda kt, ci, pi: (kt, 0, ci + c_off // ct))],
        out_specs=pl.BlockSpec((None, N2 // SUBLANE, ct // LANE, SUBLANE * SUBLANE, LANE),
                               lambda kt, ci, pi: (pi, 0, ci, kt, 0)),
        out_shape=jax.ShapeDtypeStruct((p, N2 // SUBLANE, c // LANE, n1 * SUBLANE, LANE), jnp.uint32),
        compiler_params=_cparams(("parallel", "parallel", "parallel")),
        name="hyena_stage_b",
    )(a, tabs['f2'], tabs['g2'], kf)


def _permute_seq(a, nh):
    b, _, c = a.shape
    return a.reshape(b // 2, 2, nh, N2, c).transpose(0, 3, 1, 2, 4).reshape(b // 2, N2, 2 * nh, c)


def _unpermute_seq(a, nh):
    p, _, _, c = a.shape
    return a.reshape(p, N2, 2, nh, c).transpose(0, 2, 3, 1, 4).reshape(2 * p, nh * N2, c)


def _filter_spectrum(tabs, nh, k_un, ssum):
    n, call = k_un.shape
    kp = k_un.reshape(1, N2, 2 * nh, call)
    (a,) = _stage_a(tabs, nh, kp, div=ssum.reshape(1, call), w1=tabs['w1_real'], ct=512, n2t=16)
    return _stage_b(tabs, a)


def _hyena_long(tabs, nh, kf, layer, v, x1, m2, skip):
    vp, x1p, m2p = (_permute_seq(t, nh) for t in (v, x1, m2))
    c0 = layer * HYENA_ORDER * W_HYENA
    skip2 = skip.reshape(1, HYENA_ORDER * W_HYENA)
    (a1,) = _stage_a(tabs, nh, vp, w1=tabs['w1_pair'])
    y1 = _stage_b(tabs, a1, kf, c_off=c0)
    z, a2 = _stage_a(tabs, nh, vp, y=y1, mul=x1p, skip=skip2, w1=tabs['w1_pair'], c_off=0)
    y2 = _stage_b(tabs, a2, kf, c_off=c0 + W_HYENA)
    (gp,) = _stage_a(tabs, nh, z, y=y2, mul=m2p, skip=skip2, want_fwd=False, c_off=W_HYENA)
    return _unpermute_seq(gp, nh)


def _ctx_conv_kernel(v_ref, x1_ref, m2_ref, k_ref, s_ref, skip_ref, ff_ref, fk_ref, gi_ref, o_ref,
                     *, n):
    hp = lax.Precision.HIGHEST

    def conv(u, o):
        kfull = k_ref[:, o * W_HYENA:(o + 1) * W_HYENA] / s_ref[:, o * W_HYENA:(o + 1) * W_HYENA]
        kf = jnp.dot(fk_ref[...], kfull, preferred_element_type=F32, precision=hp)
        uf = jnp.dot(ff_ref[...], u, preferred_element_type=F32, precision=hp)
        ure, uim, kre, kim = uf[:n], uf[n:], kf[:n], kf[n:]
        z = jnp.concatenate([ure * kre - uim * kim, ure * kim + uim * kre], axis=0)
        y = jnp.dot(gi_ref[...], z, preferred_element_type=F32, precision=hp)
        return y + u * skip_ref[o:o + 1, :]

    v = v_ref[...].astype(F32)
    z1 = x1_ref[...].astype(F32) * conv(v, 0)
    o_ref[...] = (m2_ref[...].astype(F32) * conv(z1, 1)).astype(o_ref.dtype)


def _ctx_tables(lc):
    n = 2 * lc
    k = jnp.arange(n, dtype=jnp.int32)
    ph = _angles(k[:, None], k[None, :], n)
    c, s = jnp.cos(ph), jnp.sin(ph)
    fk = jnp.concatenate([c, -s], axis=0)
    ff = fk[:, :lc]
    gi = jnp.concatenate([c[:lc], -s[:lc]], axis=1) / n
    return ff, fk, gi


def _ctx_hyena(ctabs, k_un, ssum, layer, v, x1, m2, skip):
    b, lc, c = v.shape
    n = 2 * lc
    ff, fk, gi = ctabs
    wc = HYENA_ORDER * W_HYENA
    tok = pl.BlockSpec((None, lc, c), lambda bi: (bi, 0, 0))
    return pl.pallas_call(
        functools.partial(_ctx_conv_kernel, n=n),
        grid=(b,),
        in_specs=[tok, tok, tok,
                  pl.BlockSpec((n, wc), lambda bi: (0, layer)),
                  pl.BlockSpec((None, 1, wc), lambda bi: (layer, 0, 0)),
                  pl.BlockSpec((HYENA_ORDER, c), lambda bi: (0, 0)),
                  _full_spec(ff), _full_spec(fk), _full_spec(gi)],
        out_specs=tok,
        out_shape=jax.ShapeDtypeStruct((b, lc, c), BF16),
        compiler_params=_cparams(("parallel",)),
        name="ctx_hyena",
    )(v, x1, m2, k_un, ssum, skip, ff, fk, gi)


def _merge_kernel(x_ref, a_ref, gh_ref, att_ref, sm_ref, g_ref, sc_ref, sh_ref, gt_ref,
                  wg_ref, bg_ref, sco_ref, hyo_ref, mlo_ref, wo_ref, fg_ref, o_ref, *, final):
    x = x_ref[...]
    d = x.shape[-1]
    h = _mod_norm(x, g_ref[...], sc_ref[...], sh_ref[...])
    gates = jax.nn.sigmoid(jnp.dot(h.astype(BF16), wg_ref[...], preferred_element_type=F32) + bg_ref[...])
    ya = jnp.dot(a_ref[...], sco_ref[...], preferred_element_type=F32)
    yh = jnp.dot(gh_ref[...], hyo_ref[...], preferred_element_type=F32)
    am = (sm_ref[...].astype(F32) * att_ref[...].astype(F32)).astype(BF16)
    ym = jnp.dot(am, mlo_ref[...], preferred_element_type=F32)
    y = gates[:, 0:d] * ya + gates[:, d:2 * d] * yh + gates[:, 2 * d:3 * d] * ym
    o = jnp.dot(y.astype(BF16), wo_ref[...], preferred_element_type=F32)
    xn = x + gt_ref[...] * o
    if final:
        xn = _rms(xn, fg_ref[...])
    o_ref[...] = xn


def _merge_call(x2d, seq_len, mods, g, lw, a, gh, att, sm, final_g, final):
    n_rows, d = x2d.shape
    tm = min(512, seq_len)
    tiles_per_seq = seq_len // tm
    scale, shift, gate = mods
    g2 = g.reshape(1, 1, d)
    fg = final_g.reshape(1, d)
    tok = lambda wd: pl.BlockSpec((tm, wd), lambda i: (i, 0))
    consts = [lw['wg'], lw['bg'], lw['sc_out'], lw['hy_out'], lw['mla_out'], lw['w_o'], fg]
    in_specs = [tok(d), tok(W_CONV), tok(W_HYENA), tok(W_MLA), tok(W_MLA),
                _mod_spec(g2, tiles_per_seq), _mod_spec(scale, tiles_per_seq),
                _mod_spec(shift, tiles_per_seq), _mod_spec(gate, tiles_per_seq)]
    in_specs += [_full_spec(c) for c in consts]
    return pl.pallas_call(
        functools.partial(_merge_kernel, final=final),
        grid=(n_rows // tm,),
        in_specs=in_specs,
        out_specs=tok(d),
        out_shape=jax.ShapeDtypeStruct((n_rows, d), F32),
        compiler_params=_cparams(("parallel",)),
        name="merge_final" if final else "merge",
    )(x2d, a, gh, att, sm, g2, scale, shift, gate, *consts)


def _rope_swap_cols(w):
    half = QK_ROPE // 2
    return jnp.concatenate([-w[..., half:], w[..., :half]], axis=-1)


def _layer_weights(i, w_in, b_in, mla_q_norm, mla_w_uq, mla_kv_norm, mla_w_ukv, sc_out, hy_out,
                   mla_out, w_o):
    wi, bi = w_in[i], b_in[i]
    d = wi.shape[0]
    lw = {}
    lw['w1'] = jnp.concatenate([wi[:, O_XIN:O_HPROJ], wi[:, O_ZM:O_GATES]], axis=1).astype(BF16)
    lw['b1'] = jnp.concatenate([bi[O_XIN:O_HPROJ], bi[O_ZM:O_GATES]])[None, :]
    lw['w2'] = wi[:, O_HPROJ:O_CQ].astype(BF16)
    lw['b2'] = bi[O_HPROJ:O_CQ][None, :]
    wkr, bkr = wi[:, O_KR:O_ZM], bi[O_KR:O_ZM]
    zw = lambda n: jnp.zeros((d, n), F32)
    zb = lambda n: jnp.zeros((n,), F32)
    lw['w3'] = jnp.concatenate([wi[:, O_CQ:O_KR], wkr, zw(96), _rope_swap_cols(wkr), zw(96)], axis=1).astype(BF16)
    lw['b3'] = jnp.concatenate([bi[O_CQ:O_KR], bkr, zb(96), _rope_swap_cols(bkr), zb(96)])[None, :]
    lw['qg'] = mla_q_norm[i][None, :]
    lw['kvg'] = mla_kv_norm[i][None, :]
    wuq = mla_w_uq[i].reshape(Q_LORA, N_HEADS, QK_NOPE + QK_ROPE)
    pad = jnp.zeros((Q_LORA, N_HEADS, HEAD_PAD - QK_NOPE - QK_ROPE), F32)
    lw['wq'] = jnp.concatenate([wuq, pad], axis=-1).reshape(Q_LORA, N_HEADS * HEAD_PAD).astype(BF16)
    zn = jnp.zeros((Q_LORA, N_HEADS, QK_NOPE), F32)
    lw['wqs'] = jnp.concatenate([zn, _rope_swap_cols(wuq[..., QK_NOPE:]), pad], axis=-1).reshape(
        Q_LORA, N_HEADS * HEAD_PAD).astype(BF16)
    wukv = mla_w_ukv[i].reshape(KV_LORA, N_HEADS, QK_NOPE + V_HEAD)
    padk = jnp.zeros((KV_LORA, N_HEADS, HEAD_PAD - QK_NOPE), F32)
    lw['wk'] = jnp.concatenate([wukv[..., :QK_NOPE], padk], axis=-1).reshape(
        KV_LORA, N_HEADS * HEAD_PAD).astype(BF16)
    padv = jnp.zeros((KV_LORA, N_HEADS, HEAD_PAD - V_HEAD), F32)
    lw['wv'] = jnp.concatenate([wukv[..., QK_NOPE:], padv], axis=-1).reshape(
        KV_LORA, N_HEADS * HEAD_PAD).astype(BF16)
    lw['vb'] = jnp.zeros((N_HEADS, HEAD_PAD), F32).at[:, V_HEAD].set(1.0).reshape(1, N_HEADS * HEAD_PAD)
    e2 = jnp.zeros((QK_ROPE, N_HEADS, HEAD_PAD), F32).at[:, :, QK_NOPE:QK_NOPE + QK_ROPE].set(
        jnp.eye(QK_ROPE, dtype=F32)[:, None, :])
    lw['e2'] = e2.reshape(QK_ROPE, N_HEADS * HEAD_PAD).astype(BF16)
    lw['wg'] = wi[:, O_GATES:].astype(BF16)
    lw['bg'] = bi[O_GATES:][None, :]
    lw['sc_out'] = sc_out[i].astype(BF16)
    lw['hy_out'] = hy_out[i].astype(BF16)
    lw['mla_out'] = mla_out[i].astype(BF16)
    lw['w_o'] = w_o[i].astype(BF16)
    return lw


def _rope_tables(seq_len, use_rope):
    scale = (QK_NOPE + QK_ROPE) ** -0.5 * math.log2(math.e)
    if use_rope:
        rows = seq_len // GRID_W
        row = jnp.broadcast_to(jnp.arange(rows, dtype=F32)[:, None], (rows, GRID_W)).reshape(seq_len)
        col = jnp.broadcast_to(jnp.arange(GRID_W, dtype=F32)[None, :], (rows, GRID_W)).reshape(seq_len)
        n_f = QK_ROPE // 4
        inv = ROPE_BASE ** (-jnp.arange(n_f, dtype=F32) / n_f)
        ang = jnp.concatenate([row[:, None] * inv, col[:, None] * inv], axis=-1)
        cos, sin = jnp.cos(ang), jnp.sin(ang)
    else:
        cos = jnp.ones((seq_len, QK_ROPE // 2), F32)
        sin = jnp.zeros((seq_len, QK_ROPE // 2), F32)
    cosk = jnp.concatenate([cos, cos], axis=-1)
    sink = jnp.concatenate([sin, sin], axis=-1)
    ones = jnp.ones((seq_len, QK_NOPE), F32)
    zeros = jnp.zeros((seq_len, QK_NOPE), F32)
    tail = jnp.zeros((seq_len, HEAD_PAD - QK_NOPE - QK_ROPE), F32)
    cosq = jnp.tile(jnp.concatenate([ones, cosk, tail], axis=-1), (1, N_HEADS)) * scale
    sinq = jnp.tile(jnp.concatenate([zeros, sink, tail], axis=-1), (1, N_HEADS)) * scale
    return cosq, sinq, cosk, sink


def kernel(x, c, ctx, c_ctx, ada_w, ada_b, norm_g, w_in, b_in, sc_conv_w, sc_conv_b, sc_out, hy_conv_w,
           hy_conv_b, hy_w1, hy_b1, hy_w2, hy_b2, hy_w3, hy_freq, hy_skip, hy_out, mla_q_norm, mla_w_uq,
           mla_kv_norm, mla_w_ukv, mla_out, w_o, final_g):
    bsz, seq, d = x.shape
    lc = ctx.shape[1]
    depth = ada_w.shape[0]
    nh = seq // N2
    assert d == D_MODEL and bsz % 2 == 0 and nh % SUBLANE == 0 and bsz <= 7

    cc = jnp.zeros((8, d), F32).at[:bsz].set(c).at[bsz].set(c_ctx)
    mods = _ada_mods(cc, ada_w, ada_b)

    tabs = _dft_tables(nh)
    k_un, ssum = _filter_gen(seq, depth, True, hy_w1, hy_b1, hy_w2, hy_b2, hy_w3, hy_freq)
    kf = _filter_spectrum(tabs, nh, k_un, ssum)
    if depth > 1:
        kc_un, sc_sum = _filter_gen(lc, depth - 1, False, hy_w1, hy_b1, hy_w2, hy_b2, hy_w3, hy_freq)
        ctabs = _ctx_tables(lc)

    rope_l = _rope_tables(seq, True)
    rope_c = _rope_tables(lc, False)

    x_lat = x.reshape(bsz * seq, d)
    x_ctx = ctx.reshape(bsz * lc, d)
    for i in range(depth):
        last = i == depth - 1
        lw = _layer_weights(i, w_in, b_in, mla_q_norm, mla_w_uq, mla_kv_norm, mla_w_ukv, sc_out,
                            hy_out, mla_out, w_o)
        m = mods[i]
        split = lambda r: tuple(r[:, None, j * d:(j + 1) * d] for j in range(3))
        shift_l, scale_l, gate_l = split(m[:bsz])
        shift_c, scale_c, gate_c = split(m[bsz:bsz + 1])
        g = norm_g[i]
        scw, scb = sc_conv_w[i], sc_conv_b[i][None, :]
        hcw, hcb = hy_conv_w[i], hy_conv_b[i][None, :]

        hq = N_HEADS * HEAD_PAD
        q_l, k_all, v_all = _kp3_call(x_lat, bsz, seq, seq + lc, 0, (scale_l, shift_l), g, lw, rope_l)
        q_c, k_all, v_all = _kp3_call(x_ctx, bsz, lc, seq + lc, seq, (scale_c, shift_c), g, lw, rope_c,
                                      kv_bufs=(k_all, v_all))
        att_l = _attention(q_l.reshape(bsz, seq, hq), k_all, v_all).reshape(bsz * seq, W_MLA)

        a_l, sm_l = _kp_conv_call(_kp1_kernel, "kp1_conv", x_lat, seq, (scale_l, shift_l), g,
                                  lw['w1'], lw['b1'], scw, scb, (W_CONV, W_MLA))
        v_h, x1_h, m2_h = _kp_conv_call(_kp2_kernel, "kp2_hyena", x_lat, seq, (scale_l, shift_l), g,
                                        lw['w2'], lw['b2'], hcw, hcb, (W_HYENA,) * 3)
        r3 = lambda t: t.reshape(bsz, seq, W_HYENA)
        gh_l = _hyena_long(tabs, nh, kf, i, r3(v_h), r3(x1_h), r3(m2_h), hy_skip[i])
        new_lat = _merge_call(x_lat, seq, (scale_l, shift_l, gate_l), g, lw, a_l,
                              gh_l.reshape(bsz * seq, W_HYENA), att_l, sm_l, final_g, last)
        if not last:
            att_c = _attention(q_c.reshape(bsz, lc, hq), k_all, v_all, kv_row0=seq,
                               kv_len=lc).reshape(bsz * lc, W_MLA)
            a_c, sm_c = _kp_conv_call(_kp1_kernel, "kp1_conv_ctx", x_ctx, lc, (scale_c, shift_c), g,
                                      lw['w1'], lw['b1'], scw, scb, (W_CONV, W_MLA))
            v_hc, x1_hc, m2_hc = _kp_conv_call(_kp2_kernel, "kp2_hyena_ctx", x_ctx, lc,
                                               (scale_c, shift_c), g, lw['w2'], lw['b2'], hcw, hcb,
                                               (W_HYENA,) * 3)
            rc = lambda t: t.reshape(bsz, lc, W_HYENA)
            gh_c = _ctx_hyena(ctabs, kc_un, sc_sum, i, rc(v_hc), rc(x1_hc), rc(m2_hc), hy_skip[i])
            x_ctx = _merge_call(x_ctx, lc, (scale_c, shift_c, gate_c), g, lw, a_c,
                                gh_c.reshape(bsz * lc, W_HYENA), att_c, sm_c, final_g, False)
        x_lat = new_lat
    return x_lat.reshape(bsz, seq, d)
```

```python
import functools
import math

import jax
import jax.numpy as jnp
from jax import lax
from jax.experimental import pallas as pl
from jax.experimental.pallas import tpu as pltpu

F32 = jnp.float32
BF16 = jnp.bfloat16

D_MODEL = 1024
DEPTH = 4
GRID_W = 64
W_CONV = 512
W_HYENA = 512
HYENA_ORDER = 2
HYENA_EMB = 33
HYENA_BANDS = (HYENA_EMB - 1) // 2
HYENA_HID = 64
HYENA_FAST_DECAY = 0.3
HYENA_SLOW_DECAY = 1.5
HYENA_TARGET = 1e-2
N_HEADS = 8
QK_NOPE = 64
QK_ROPE = 32
V_HEAD = 64
Q_LORA = 384
KV_LORA = 256
W_MLA = N_HEADS * V_HEAD
ROPE_BASE = 10000.0
N_BRANCH = 3
EPS = 1e-6

O_XIN, O_GB, O_GC, O_ZA = 0, 512, 1024, 1536
O_HPROJ, O_ZH = 2048, 3584
O_CQ, O_CKV, O_KR, O_ZM, O_GATES = 4096, 4480, 4736, 4768, 5280
N_IN = 8352

LANE = 128
SUBLANE = 8
HEAD_PAD = 128
N2 = 128
VMEM_LIMIT = 56 * 1024 * 1024


def _cparams(sem):
    return pltpu.CompilerParams(dimension_semantics=sem, vmem_limit_bytes=VMEM_LIMIT)


def _ada_kernel(c_ref, w_ref, b_ref, o_ref):
    c = c_ref[...]
    s = c * jax.nn.sigmoid(c)
    o_ref[...] = jnp.dot(s, w_ref[...], preferred_element_type=F32,
                         precision=lax.Precision.HIGHEST) + b_ref[...]


def _ada_mods(cc, ada_w, ada_b):
    depth = ada_w.shape[0]
    d = cc.shape[1]
    return pl.pallas_call(
        _ada_kernel,
        grid=(depth, 3),
        in_specs=[pl.BlockSpec((8, d), lambda l, j: (0, 0)),
                  pl.BlockSpec((None, d, d), lambda l, j: (l, 0, j)),
                  pl.BlockSpec((None, 1, d), lambda l, j: (l, 0, j))],
        out_specs=pl.BlockSpec((None, 8, d), lambda l, j: (l, 0, j)),
        out_shape=jax.ShapeDtypeStruct((depth, 8, 3 * d), F32),
        compiler_params=_cparams(("parallel", "parallel")),
        name="ada_mods",
    )(cc, ada_w, ada_b.reshape(depth, 1, 3 * d))


def _mod_norm(x, g, scale, shift):
    y = x * lax.rsqrt(jnp.mean(x * x, axis=-1, keepdims=True) + EPS)
    return (y * g) * (1.0 + scale) + shift


def _rms(x, g):
    return x * lax.rsqrt(jnp.mean(x * x, axis=-1, keepdims=True) + EPS) * g


def _silu(x):
    return x * jax.nn.sigmoid(x)


def _dwconv3_ext(u_ext, w, b, first, last, tm):
    n = tm + 2 * SUBLANE
    prev = pltpu.roll(u_ext, 1, 0)[SUBLANE:SUBLANE + tm]
    nxt = pltpu.roll(u_ext, n - 1, 0)[SUBLANE:SUBLANE + tm]
    row = lax.broadcasted_iota(jnp.int32, (tm, 1), 0)
    prev = jnp.where(jnp.logical_and(first, row == 0), 0.0, prev)
    nxt = jnp.where(jnp.logical_and(last, row == tm - 1), 0.0, nxt)
    return prev * w[0:1] + u_ext[SUBLANE:SUBLANE + tm] * w[1:2] + nxt * w[2:3] + b


def _proj_ext(xm_ref, xp_ref, xn_ref, g_ref, sc_ref, sh_ref, w_ref, b_ref):
    x_ext = jnp.concatenate([xp_ref[...], xm_ref[...], xn_ref[...]], axis=0)
    h = _mod_norm(x_ext, g_ref[...], sc_ref[...], sh_ref[...])
    return jnp.dot(h.astype(BF16), w_ref[...], preferred_element_type=F32) + b_ref[...]


def _kp1_kernel(xm_ref, xp_ref, xn_ref, g_ref, sc_ref, sh_ref, w_ref, b_ref, cw_ref, cb_ref,
                a_ref, sm_ref, *, tm, tiles_per_seq):
    i = pl.program_id(0)
    first = (i % tiles_per_seq) == 0
    last = (i % tiles_per_seq) == tiles_per_seq - 1
    p = _proj_ext(xm_ref, xp_ref, xn_ref, g_ref, sc_ref, sh_ref, w_ref, b_ref)
    prod = p[:, 2 * W_CONV:3 * W_CONV] * p[:, 0:W_CONV]
    conv = _dwconv3_ext(prod, cw_ref[...], cb_ref[...], first, last, tm)
    pm = p[SUBLANE:SUBLANE + tm]
    a = _silu(pm[:, 3 * W_CONV:4 * W_CONV]) * (pm[:, W_CONV:2 * W_CONV] * conv)
    a_ref[...] = a.astype(a_ref.dtype)
    sm_ref[...] = _silu(pm[:, 4 * W_CONV:4 * W_CONV + W_MLA]).astype(sm_ref.dtype)


def _kp2_kernel(xm_ref, xp_ref, xn_ref, g_ref, sc_ref, sh_ref, w_ref, b_ref, cw_ref, cb_ref,
                v_ref, x1_ref, m2_ref, *, tm, tiles_per_seq):
    i = pl.program_id(0)
    first = (i % tiles_per_seq) == 0
    last = (i % tiles_per_seq) == tiles_per_seq - 1
    p = _proj_ext(xm_ref, xp_ref, xn_ref, g_ref, sc_ref, sh_ref, w_ref, b_ref)
    u = _dwconv3_ext(p[:, 0:3 * W_HYENA], cw_ref[...], cb_ref[...], first, last, tm)
    zh = p[SUBLANE:SUBLANE + tm, 3 * W_HYENA:4 * W_HYENA]
    v_ref[...] = u[:, 0:W_HYENA].astype(v_ref.dtype)
    x1_ref[...] = u[:, W_HYENA:2 * W_HYENA].astype(x1_ref.dtype)
    m2_ref[...] = (_silu(zh) * u[:, 2 * W_HYENA:3 * W_HYENA]).astype(m2_ref.dtype)


def _kp3_kernel(xm_ref, g_ref, sc_ref, sh_ref, w_ref, b_ref, qg_ref, kvg_ref, wq_ref, wqs_ref,
                wk_ref, wv_ref, vb_ref, e2_ref, cosq_ref, sinq_ref, cosk_ref, sink_ref,
                q_ref, k_ref, v_ref):
    h = _mod_norm(xm_ref[...], g_ref[...], sc_ref[...], sh_ref[...])
    p = jnp.dot(h.astype(BF16), w_ref[...], preferred_element_type=F32) + b_ref[...]
    cq = _rms(p[:, 0:Q_LORA], qg_ref[...]).astype(BF16)
    ckv = _rms(p[:, Q_LORA:Q_LORA + KV_LORA], kvg_ref[...]).astype(BF16)
    kr = p[:, 640:640 + QK_ROPE]
    krs = p[:, 768:768 + QK_ROPE]
    qa = jnp.dot(cq, wq_ref[...], preferred_element_type=F32)
    qb = jnp.dot(cq, wqs_ref[...], preferred_element_type=F32)
    q_ref[...] = (qa * cosq_ref[...] + qb * sinq_ref[...]).astype(q_ref.dtype)
    kr_rot = (kr * cosk_ref[...] + krs * sink_ref[...]).astype(BF16)
    kn = jnp.dot(ckv, wk_ref[...], preferred_element_type=F32)
    krp = jnp.dot(kr_rot, e2_ref[...], preferred_element_type=F32)
    k_ref[...] = (kn + krp).astype(k_ref.dtype)
    v_ref[...] = (jnp.dot(ckv, wv_ref[...], preferred_element_type=F32) + vb_ref[...]).astype(v_ref.dtype)


def _mod_spec(mod, tiles_per_seq):
    d = mod.shape[-1]
    if mod.shape[0] == 1:
        return pl.BlockSpec((None, 1, d), lambda i: (0, 0, 0))
    return pl.BlockSpec((None, 1, d), lambda i: (i // tiles_per_seq, 0, 0))


def _full_spec(a):
    nd = a.ndim
    return pl.BlockSpec(a.shape, lambda i: (0,) * nd)


def _halo_specs(n_rows, tm, d):
    r = tm // SUBLANE
    nb = n_rows // SUBLANE
    return [pl.BlockSpec((tm, d), lambda i: (i, 0)),
            pl.BlockSpec((SUBLANE, d), lambda i: (jnp.maximum(i * r - 1, 0), 0)),
            pl.BlockSpec((SUBLANE, d), lambda i: (jnp.minimum((i + 1) * r, nb - 1), 0))]


def _kp_conv_call(kern, name, x2d, seq_len, mods, g, w, b, cw, cb, out_widths):
    n_rows, d = x2d.shape
    tm = min(512, seq_len)
    tiles_per_seq = seq_len // tm
    scale, shift = mods
    g2 = g.reshape(1, 1, d)
    in_specs = _halo_specs(n_rows, tm, d) + [
        _mod_spec(g2, tiles_per_seq), _mod_spec(scale, tiles_per_seq), _mod_spec(shift, tiles_per_seq),
        _full_spec(w), _full_spec(b), _full_spec(cw), _full_spec(cb)]
    return pl.pallas_call(
        functools.partial(kern, tm=tm, tiles_per_seq=tiles_per_seq),
        grid=(n_rows // tm,),
        in_specs=in_specs,
        out_specs=[pl.BlockSpec((tm, wd), lambda i: (i, 0)) for wd in out_widths],
        out_shape=[jax.ShapeDtypeStruct((n_rows, wd), BF16) for wd in out_widths],
        compiler_params=_cparams(("parallel",)),
        name=name,
    )(x2d, x2d, x2d, g2, scale, shift, w, b, cw, cb)


def _kp3_aliased_kernel(*refs):
    _kp3_kernel(*refs[:-5], *refs[-3:])


def _kp3_call(x2d, bsz, seq_len, kv_len, kv_row0, mods, g, lw, tabs, kv_bufs=None):
    n_rows, d = x2d.shape
    tm = min(512, seq_len)
    tiles_per_seq = seq_len // tm
    blk0 = kv_row0 // tm
    assert kv_row0 % tm == 0
    scale, shift = mods
    g2 = g.reshape(1, 1, d)
    cosq, sinq, cosk, sink = tabs
    consts = [lw['w3'], lw['b3'], lw['qg'], lw['kvg'], lw['wq'], lw['wqs'], lw['wk'], lw['wv'], lw['vb'],
              lw['e2']]

    def tab_spec(t):
        return pl.BlockSpec((tm, t.shape[1]), lambda i: (i % tiles_per_seq, 0))

    in_specs = [pl.BlockSpec((tm, d), lambda i: (i, 0)),
                _mod_spec(g2, tiles_per_seq), _mod_spec(scale, tiles_per_seq), _mod_spec(shift, tiles_per_seq)]
    in_specs += [_full_spec(a) for a in consts]
    in_specs += [tab_spec(t) for t in (cosq, sinq, cosk, sink)]
    args = [x2d, g2, scale, shift, *consts, cosq, sinq, cosk, sink]
    hq = N_HEADS * HEAD_PAD
    kv_spec = pl.BlockSpec((None, tm, hq), lambda i: (i // tiles_per_seq, blk0 + i % tiles_per_seq, 0))
    kv_shape = jax.ShapeDtypeStruct((bsz, kv_len, hq), BF16)
    aliases = {}
    kern = _kp3_kernel
    if kv_bufs is not None:
        kern = _kp3_aliased_kernel
        in_specs += [pl.BlockSpec(memory_space=pl.ANY)] * 2
        aliases = {len(args): 1, len(args) + 1: 2}
        args += list(kv_bufs)
    return pl.pallas_call(
        kern,
        grid=(n_rows // tm,),
        in_specs=in_specs,
        out_specs=[pl.BlockSpec((tm, hq), lambda i: (i, 0)), kv_spec, kv_spec],
        out_shape=[jax.ShapeDtypeStruct((n_rows, hq), BF16), kv_shape, kv_shape],
        input_output_aliases=aliases,
        compiler_params=_cparams(("parallel",)),
        name="kp3_mla" if kv_bufs is None else "kp3_mla_ctx",
    )(*args)


ATTN_CHAINS = 4


def _attn_kernel(q_ref, k_ref, v_ref, o_ref, s_ref, p_ref, m_ref, acc_ref):
    h = pl.program_id(2)
    ki = pl.program_id(3)

    @pl.when(ki == 0)
    def _():
        m_ref[...] = jnp.full(m_ref.shape, -jnp.inf, F32)
        acc_ref[...] = jnp.zeros(acc_ref.shape, F32)

    hq = q_ref.shape[0] // ATTN_CHAINS
    chains = [pl.ds(c * hq, hq) for c in range(ATTN_CHAINS)]
    for rows in chains:
        s_ref[rows, :] = lax.dot_general(q_ref[rows, :], k_ref[...], (((1,), (1,)), ((), ())),
                                         preferred_element_type=F32)
    alphas = []
    reps = s_ref.shape[1] // LANE
    for rows in chains:
        m_prev = m_ref[rows, :]
        m_new = jnp.maximum(m_prev, jnp.max(s_ref[rows, :], axis=-1, keepdims=True))
        alphas.append(jnp.exp2(m_prev - m_new))
        m_ref[rows, :] = m_new
        p_ref[rows, :] = jnp.exp2((s_ref[rows, :] - pltpu.repeat(m_new, reps, axis=1)).astype(BF16))
    for rows, alpha in zip(chains, alphas):
        acc_ref[rows, :] = acc_ref[rows, :] * alpha + jnp.dot(p_ref[rows, :], v_ref[...],
                                                              preferred_element_type=F32)

    @pl.when(ki == pl.num_programs(3) - 1)
    def _():
        acc = acc_ref[...]
        o = (acc[:, 0:V_HEAD] / acc[:, V_HEAD:V_HEAD + 1]).astype(o_ref.dtype)

        @pl.when(h % 2 == 0)
        def _():
            o_ref[:, 0:V_HEAD] = o

        @pl.when(h % 2 == 1)
        def _():
            o_ref[:, V_HEAD:2 * V_HEAD] = o


MXU_DIM = 256
Q_TILE = 2048
KV_TILE_CAP = 2816


def _kv_tile(lk):
    for step in (MXU_DIM, LANE):
        cands = [t for t in range(step, min(lk, KV_TILE_CAP) + 1, step) if lk % t == 0]
        if cands:
            return cands[-1]
    return lk


def _attention(q, k, v, kv_row0=0, kv_len=None):
    b, lq, _ = q.shape
    lk = k.shape[1] if kv_len is None else kv_len
    tq = min(Q_TILE, lq)
    tk = _kv_tile(lk)
    assert kv_row0 % tk == 0
    kb0 = kv_row0 // tk
    grid = (b, lq // tq, N_HEADS, lk // tk)
    return pl.pallas_call(
        _attn_kernel,
        grid=grid,
        in_specs=[pl.BlockSpec((None, tq, HEAD_PAD), lambda bi, qi, h, ki: (bi, qi, h)),
                  pl.BlockSpec((None, tk, HEAD_PAD), lambda bi, qi, h, ki: (bi, kb0 + ki, h)),
                  pl.BlockSpec((None, tk, HEAD_PAD), lambda bi, qi, h, ki: (bi, kb0 + ki, h))],
        out_specs=pl.BlockSpec((None, tq, 2 * V_HEAD), lambda bi, qi, h, ki: (bi, qi, h // 2)),
        out_shape=jax.ShapeDtypeStruct((b, lq, W_MLA), BF16),
        scratch_shapes=[pltpu.VMEM((tq, tk), F32), pltpu.VMEM((tq, tk), BF16),
                        pltpu.VMEM((tq, LANE), F32), pltpu.VMEM((tq, HEAD_PAD), F32)],
        compiler_params=_cparams(("parallel", "parallel", "arbitrary", "arbitrary")),
        name="mla_attention",
    )(q, k, v)


COL_BWD = HYENA_EMB
COL_DROP = HYENA_EMB + 1


def _filt_kernel(z_ref, w1_ref, b1_ref, w2_ref, b2_ref, fr_ref, w3_ref, dl_ref, k_ref, s_ref):
    i = pl.program_id(1)
    hp = lax.Precision.HIGHEST
    wc = HYENA_ORDER * W_HYENA
    z = z_ref[...]
    fr = fr_ref[...]
    hid = jnp.sin(fr * (jnp.dot(z, w1_ref[...], preferred_element_type=F32, precision=hp) + b1_ref[...]))
    hid = jnp.sin(fr * (jnp.dot(hid, w2_ref[...], preferred_element_type=F32, precision=hp) + b2_ref[...]))
    h2 = jnp.dot(hid.astype(BF16), w3_ref[...], preferred_element_type=F32)
    h = jnp.where(z[:, COL_BWD:COL_BWD + 1] > 0.5, h2[:, wc:], h2[:, :wc])
    t = z[:, 0:1]
    h = h * jnp.exp(-t * dl_ref[...])

    @pl.when(i == 0)
    def _():
        s_ref[...] = jnp.zeros(s_ref.shape, F32)

    s_ref[...] += jnp.sum(jnp.abs(h), axis=0, keepdims=True)
    k_ref[...] = jnp.where(z[:, COL_DROP:COL_DROP + 1] > 0.5, 0.0, h)


def _filter_tables(seq_len, permuted):
    t = jnp.linspace(0.0, 1.0, seq_len, dtype=F32)[:, None]
    w = 2.0 * math.pi * jnp.arange(seq_len, dtype=F32)[:, None] / seq_len
    f = jnp.linspace(1e-4, HYENA_BANDS - 1, HYENA_BANDS, dtype=F32)[None, :]
    z = jnp.concatenate([t, jnp.cos(f * w), -jnp.sin(f * w)], axis=-1)
    idx = jnp.concatenate([jnp.arange(seq_len), jnp.zeros((1,), jnp.int32),
                           seq_len - jnp.arange(1, seq_len)])
    n = jnp.arange(2 * seq_len)
    flags = jnp.stack([n >= seq_len, n == seq_len], axis=-1).astype(F32)
    zz = jnp.concatenate([z[idx], flags], axis=-1)
    zz = jnp.pad(zz, ((0, 0), (0, LANE - zz.shape[1])))
    if permuted:
        zz = zz.reshape(2 * seq_len // N2, N2, LANE).transpose(1, 0, 2).reshape(2 * seq_len, LANE)
    return zz


def _hyena_deltas():
    max_decay = math.log(HYENA_TARGET) / HYENA_FAST_DECAY
    min_decay = math.log(HYENA_TARGET) / HYENA_SLOW_DECAY
    deltas = jnp.abs(jnp.linspace(min_decay, max_decay, W_HYENA, dtype=F32))
    return jnp.tile(deltas, HYENA_ORDER)[None, :]


def _filter_gen(seq_len, n_layers, permuted, hy_w1, hy_b1, hy_w2, hy_b2, hy_w3, hy_freq):
    zz = _filter_tables(seq_len, permuted)
    tmf = min(512, seq_len)
    tiles_half = seq_len // tmf
    wc = HYENA_ORDER * W_HYENA
    w1p = jnp.pad(hy_w1[:n_layers], ((0, 0), (0, LANE - HYENA_EMB), (0, 0)))
    w3r = hy_w3[:n_layers].astype(BF16)
    r1 = lambda a: a[:n_layers].reshape(n_layers, 1, HYENA_HID)
    return pl.pallas_call(
        _filt_kernel,
        grid=(n_layers, 2 * tiles_half),
        in_specs=[pl.BlockSpec((tmf, LANE), lambda l, i: (i, 0)),
                  pl.BlockSpec((None, LANE, HYENA_HID), lambda l, i: (l, 0, 0)),
                  pl.BlockSpec((None, 1, HYENA_HID), lambda l, i: (l, 0, 0)),
                  pl.BlockSpec((None, HYENA_HID, HYENA_HID), lambda l, i: (l, 0, 0)),
                  pl.BlockSpec((None, 1, HYENA_HID), lambda l, i: (l, 0, 0)),
                  pl.BlockSpec((None, 1, HYENA_HID), lambda l, i: (l, 0, 0)),
                  pl.BlockSpec((None, HYENA_HID, 2 * wc), lambda l, i: (l, 0, 0)),
                  pl.BlockSpec((1, wc), lambda l, i: (0, 0))],
        out_specs=[pl.BlockSpec((tmf, wc), lambda l, i: (i, l)),
                   pl.BlockSpec((None, 1, wc), lambda l, i: (l, 0, 0))],
        out_shape=[jax.ShapeDtypeStruct((2 * seq_len, n_layers * wc), F32),
                   jax.ShapeDtypeStruct((n_layers, 1, wc), F32)],
        compiler_params=_cparams(("parallel", "arbitrary")),
        name="hyena_filter",
    )(zz, w1p, r1(hy_b1), hy_w2[:n_layers], r1(hy_b2), r1(hy_freq), w3r, _hyena_deltas())


def _angles(a, b, n):
    m = (a * b) % n
    return m.astype(F32) * (2.0 * math.pi / n)


def _dft_tables(nh):
    n1 = 2 * nh
    n = n1 * N2
    k1 = jnp.arange(n1, dtype=jnp.int32)
    th = _angles(k1[:, None], jnp.arange(nh, dtype=jnp.int32)[None, :], n1)
    c, s = jnp.cos(th), jnp.sin(th)
    w1_pair = jnp.concatenate([jnp.concatenate([c, s], 1), jnp.concatenate([-s, c], 1)], 0)
    thf = _angles(k1[:, None], k1[None, :], n1)
    w1_real = jnp.concatenate([jnp.cos(thf), -jnp.sin(thf)], 0)
    g1 = jnp.concatenate([jnp.concatenate([c.T, -s.T], 1), jnp.concatenate([s.T, c.T], 1)], 0) / n
    n2 = jnp.arange(N2, dtype=jnp.int32)
    kk = k1[:, None, None] + n1 * n2[None, :, None]
    ph = _angles(kk, n2[None, None, :], n)
    cp, sp = jnp.cos(ph), jnp.sin(ph)
    f2 = jnp.concatenate([jnp.concatenate([cp, sp], 2), jnp.concatenate([-sp, cp], 2)], 1)
    cpt, spt = cp.transpose(0, 2, 1), sp.transpose(0, 2, 1)
    g2 = jnp.concatenate([jnp.concatenate([cpt, -spt], 2), jnp.concatenate([spt, cpt], 2)], 1)
    g1 = g1.reshape(2 * nh, 2, n1).transpose(0, 2, 1).reshape(2 * nh, 2 * n1)
    f2 = f2.reshape(n1, 2 * N2, 2, N2).transpose(0, 1, 3, 2).reshape(n1, 2 * N2, 2 * N2)
    g2 = g2.reshape(n1, 2, N2, 2 * N2).transpose(0, 2, 1, 3).reshape(n1, 2 * N2, 2 * N2)
    w1_pair = w1_pair.reshape(2, n1, 2 * nh).transpose(1, 0, 2).reshape(2 * n1, 2 * nh)
    w1_real = w1_real.reshape(2, n1, n1).transpose(1, 0, 2).reshape(2 * n1, n1)
    return dict(w1_pair=w1_pair.astype(BF16), w1_real=w1_real.astype(BF16), g1=g1.astype(BF16),
                f2=f2.astype(BF16), g2=g2.astype(BF16))


def _strided_rows(ref, start, size):
    parts = [ref[cc, pl.ds(start, size, stride=SUBLANE), :] for cc in range(ref.shape[0])]
    return parts[0] if len(parts) == 1 else jnp.concatenate(parts, axis=1)


def _pack_pairs(x):
    return pltpu.bitcast(x.astype(BF16), jnp.uint32)


def _unpack_pairs(packed):
    return pltpu.bitcast(packed, BF16)


def _store_chunks(ref, row0, val):
    for cc in range(ref.shape[0]):
        ref[cc, pl.ds(row0, SUBLANE), :] = val[:, cc * LANE:(cc + 1) * LANE]


def _stage_a_kernel(*refs, n2t, n1, has_inv, has_fwd, has_div):
    it = iter(refs)
    y_ref = next(it) if has_inv else None
    g1_ref = next(it) if has_inv else None
    u_ref = next(it)
    mul_ref = next(it) if has_inv else None
    skip_ref = next(it) if has_inv else None
    div_ref = next(it) if has_div else None
    w1_ref = next(it) if has_fwd else None
    e_ref = next(it) if has_inv else None
    a_ref = next(it) if has_fwd else None
    yy_ref = next(it) if has_inv else None
    if has_inv:
        for j in range(n2t):
            nb, jl = j // SUBLANE, j % SUBLANE
            rhs = _unpack_pairs(_strided_rows(y_ref.at[nb], jl, n1))
            yy_ref[j] = jnp.dot(g1_ref[...], rhs, preferred_element_type=F32)
        for j in range(n2t):
            e = mul_ref[j].astype(F32) * (yy_ref[j] + u_ref[j].astype(F32) * skip_ref[...])
            e_ref[j] = e.astype(e_ref.dtype)
    if has_fwd:
        for j in range(n2t):
            if has_inv:
                src = e_ref[j]
            elif has_div:
                src = (u_ref[j] / div_ref[...]).astype(BF16)
            else:
                src = u_ref[j].astype(BF16)
            a = jnp.dot(w1_ref[...], src, preferred_element_type=F32)
            packed = _pack_pairs(a)
            for kt in range(n1 // SUBLANE):
                _store_chunks(a_ref.at[kt], j * SUBLANE, packed[kt * SUBLANE:(kt + 1) * SUBLANE])


def _stage_a(tabs, nh, u, *, y=None, mul=None, skip=None, div=None, w1=None, want_fwd=True,
             ct=256, n2t=32, e_dtype=BF16, c_off=0):
    p, _, rows, c = u.shape
    n1 = 2 * nh
    has_inv = y is not None
    has_div = div is not None
    ct = min(ct, c)
    nbk = n2t // SUBLANE
    grid = (p, c // ct, N2 // n2t)
    args, specs = [], []
    if has_inv:
        args += [y, tabs['g1']]
        specs += [pl.BlockSpec((None, nbk, ct // LANE, n1 * SUBLANE, LANE),
                               lambda pi, ci, ni: (pi, ni, ci, 0, 0)),
                  pl.BlockSpec(tabs['g1'].shape, lambda pi, ci, ni: (0, 0))]
    args.append(u)
    specs.append(pl.BlockSpec((None, n2t, rows, ct), lambda pi, ci, ni: (pi, ni, 0, ci)))
    if has_inv:
        args += [mul, skip]
        specs += [pl.BlockSpec((None, n2t, rows, ct), lambda pi, ci, ni: (pi, ni, 0, ci)),
                  pl.BlockSpec((1, ct), lambda pi, ci, ni: (0, ci + c_off // ct))]
    if has_div:
        args.append(div)
        specs.append(pl.BlockSpec((1, ct), lambda pi, ci, ni: (0, ci)))
    if want_fwd:
        args.append(w1)
        specs.append(pl.BlockSpec(w1.shape, lambda pi, ci, ni: (0, 0)))
    out_shape, out_specs = [], []
    if has_inv:
        out_shape.append(jax.ShapeDtypeStruct(u.shape, e_dtype))
        out_specs.append(pl.BlockSpec((None, n2t, rows, ct), lambda pi, ci, ni: (pi, ni, 0, ci)))
    if want_fwd:
        out_shape.append(jax.ShapeDtypeStruct((p, n1 // SUBLANE, c // LANE, N2 * SUBLANE, LANE), jnp.uint32))
        out_specs.append(pl.BlockSpec((None, n1 // SUBLANE, ct // LANE, n2t * SUBLANE, LANE),
                                      lambda pi, ci, ni: (pi, 0, ci, ni, 0)))
    outs = pl.pallas_call(
        functools.partial(_stage_a_kernel, n2t=n2t, n1=n1, has_inv=has_inv, has_fwd=want_fwd,
                          has_div=has_div),
        grid=grid, in_specs=specs, out_specs=out_specs, out_shape=out_shape,
        scratch_shapes=[pltpu.VMEM((n2t, rows, ct), F32)] if has_inv else [],
        compiler_params=_cparams(("parallel", "parallel", "parallel")),
        name="hyena_stage_a" + ("_inv" if has_inv else "") + ("_fwd" if want_fwd else ""),
    )(*args)
    return outs


def _stage_b_kernel(*refs, filt_only):
    if filt_only:
        a_ref, f2_ref, o_ref = refs
        for kl in range(SUBLANE):
            rhs = _unpack_pairs(_strided_rows(a_ref, kl, N2))
            o_ref[kl] = jnp.dot(f2_ref[kl], rhs, preferred_element_type=F32).astype(o_ref.dtype)
        return
    a_ref, f2_ref, g2_ref, kf_ref, o_ref, t_ref, z_ref = refs
    for kl in range(SUBLANE):
        rhs = _unpack_pairs(_strided_rows(a_ref, kl, N2))
        t_ref[kl] = jnp.dot(f2_ref[kl], rhs, preferred_element_type=F32)
    for kl in range(SUBLANE):
        kf = kf_ref[kl].astype(F32)
        tre, tim = t_ref[kl, :N2], t_ref[kl, N2:]
        kre, kim = kf[:N2], kf[N2:]
        z_ref[kl, :N2] = (tre * kre - tim * kim).astype(BF16)
        z_ref[kl, N2:] = (tre * kim + tim * kre).astype(BF16)
    for kl in range(SUBLANE):
        y = jnp.dot(g2_ref[kl], z_ref[kl], preferred_element_type=F32)
        packed = _pack_pairs(y)
        for nt in range(N2 // SUBLANE):
            _store_chunks(o_ref.at[nt], kl * SUBLANE, packed[nt * SUBLANE:(nt + 1) * SUBLANE])


def _stage_b(tabs, a, kf=None, *, c_off=0, ct=512):
    p, nkt, ncc, _, _ = a.shape
    c = ncc * LANE
    n1 = nkt * SUBLANE
    ct = min(ct, c)
    filt_only = kf is None
    grid = (nkt, c // ct, p)
    a_spec = pl.BlockSpec((None, None, ct // LANE, N2 * SUBLANE, LANE),
                          lambda kt, ci, pi: (pi, kt, ci, 0, 0))
    tab_spec = pl.BlockSpec((SUBLANE, 2 * N2, 2 * N2), lambda kt, ci, pi: (kt, 0, 0))
    if filt_only:
        return pl.pallas_call(
            functools.partial(_stage_b_kernel, filt_only=True),
            grid=grid, in_specs=[a_spec, tab_spec],
            out_specs=pl.BlockSpec((SUBLANE, 2 * N2, ct), lambda kt, ci, pi: (kt, 0, ci)),
            out_shape=jax.ShapeDtypeStruct((n1, 2 * N2, c), BF16),
            compiler_params=_cparams(("parallel", "parallel", "parallel")),
            name="hyena_stage_b_filter",
        )(a, tabs['f2'])
    return pl.pallas_call(
        functools.partial(_stage_b_kernel, filt_only=False),
        grid=grid,
        in_specs=[a_spec, tab_spec, tab_spec,
                  pl.BlockSpec((SUBLANE, 2 * N2, ct), lambda kt, ci, pi: (kt, 0, ci + c_off // ct))],
        out_specs=pl.BlockSpec((None, N2 // SUBLANE, ct // LANE, SUBLANE * SUBLANE, LANE),
                               lambda kt, ci, pi: (pi, 0, ci, kt, 0)),
        out_shape=jax.ShapeDtypeStruct((p, N2 // SUBLANE, c // LANE, n1 * SUBLANE, LANE), jnp.uint32),
        scratch_shapes=[pltpu.VMEM((SUBLANE, 2 * N2, ct), F32), pltpu.VMEM((SUBLANE, 2 * N2, ct), BF16)],
        compiler_params=_cparams(("parallel", "parallel", "parallel")),
        name="hyena_stage_b",
    )(a, tabs['f2'], tabs['g2'], kf)


def _permute_seq(a, nh):
    b, _, c = a.shape
    return a.reshape(b // 2, 2, nh, N2, c).transpose(0, 3, 1, 2, 4).reshape(b // 2, N2, 2 * nh, c)


def _unpermute_seq(a, nh):
    p, _, _, c = a.shape
    return a.reshape(p, N2, 2, nh, c).transpose(0, 2, 3, 1, 4).reshape(2 * p, nh * N2, c)


def _filter_spectrum(tabs, nh, k_un, ssum):
    n, call = k_un.shape
    kp = k_un.reshape(1, N2, 2 * nh, call)
    (a,) = _stage_a(tabs, nh, kp, div=ssum.reshape(1, call), w1=tabs['w1_real'], ct=512, n2t=16)
    return _stage_b(tabs, a)


def _hyena_long(tabs, nh, kf, layer, v, x1, m2, skip):
    vp, x1p, m2p = (_permute_seq(t, nh) for t in (v, x1, m2))
    c0 = layer * HYENA_ORDER * W_HYENA
    skip2 = skip.reshape(1, HYENA_ORDER * W_HYENA)
    (a1,) = _stage_a(tabs, nh, vp, w1=tabs['w1_pair'])
    y1 = _stage_b(tabs, a1, kf, c_off=c0)
    z, a2 = _stage_a(tabs, nh, vp, y=y1, mul=x1p, skip=skip2, w1=tabs['w1_pair'], c_off=0)
    y2 = _stage_b(tabs, a2, kf, c_off=c0 + W_HYENA)
    (gp,) = _stage_a(tabs, nh, z, y=y2, mul=m2p, skip=skip2, want_fwd=False, c_off=W_HYENA)
    return _unpermute_seq(gp, nh)


def _ctx_conv_kernel(v_ref, x1_ref, m2_ref, k_ref, s_ref, skip_ref, ff_ref, fk_ref, gi_ref, o_ref,
                     *, n):
    hp = lax.Precision.HIGHEST

    def conv(u, o):
        kfull = k_ref[:, o * W_HYENA:(o + 1) * W_HYENA] / s_ref[:, o * W_HYENA:(o + 1) * W_HYENA]
        kf = jnp.dot(fk_ref[...], kfull, preferred_element_type=F32, precision=hp)
        uf = jnp.dot(ff_ref[...], u, preferred_element_type=F32, precision=hp)
        ure, uim, kre, kim = uf[:n], uf[n:], kf[:n], kf[n:]
        z = jnp.concatenate([ure * kre - uim * kim, ure * kim + uim * kre], axis=0)
        y = jnp.dot(gi_ref[...], z, preferred_element_type=F32, precision=hp)
        return y + u * skip_ref[o:o + 1, :]

    v = v_ref[...].astype(F32)
    z1 = x1_ref[...].astype(F32) * conv(v, 0)
    o_ref[...] = (m2_ref[...].astype(F32) * conv(z1, 1)).astype(o_ref.dtype)


def _ctx_tables(lc):
    n = 2 * lc
    k = jnp.arange(n, dtype=jnp.int32)
    ph = _angles(k[:, None], k[None, :], n)
    c, s = jnp.cos(ph), jnp.sin(ph)
    fk = jnp.concatenate([c, -s], axis=0)
    ff = fk[:, :lc]
    gi = jnp.concatenate([c[:lc], -s[:lc]], axis=1) / n
    return ff, fk, gi


def _ctx_hyena(ctabs, k_un, ssum, layer, v, x1, m2, skip):
    b, lc, c = v.shape
    n = 2 * lc
    ff, fk, gi = ctabs
    wc = HYENA_ORDER * W_HYENA
    tok = pl.BlockSpec((None, lc, c), lambda bi: (bi, 0, 0))
    return pl.pallas_call(
        functools.partial(_ctx_conv_kernel, n=n),
        grid=(b,),
        in_specs=[tok, tok, tok,
                  pl.BlockSpec((n, wc), lambda bi: (0, layer)),
                  pl.BlockSpec((None, 1, wc), lambda bi: (layer, 0, 0)),
                  pl.BlockSpec((HYENA_ORDER, c), lambda bi: (0, 0)),
                  _full_spec(ff), _full_spec(fk), _full_spec(gi)],
        out_specs=tok,
        out_shape=jax.ShapeDtypeStruct((b, lc, c), BF16),
        compiler_params=_cparams(("parallel",)),
        name="ctx_hyena",
    )(v, x1, m2, k_un, ssum, skip, ff, fk, gi)


def _merge_kernel(x_ref, a_ref, gh_ref, att_ref, sm_ref, g_ref, sc_ref, sh_ref, gt_ref,
                  wg_ref, bg_ref, sco_ref, hyo_ref, mlo_ref, wo_ref, fg_ref, o_ref, *, final):
    x = x_ref[...]
    d = x.shape[-1]
    h = _mod_norm(x, g_ref[...], sc_ref[...], sh_ref[...])
    gates = jax.nn.sigmoid(jnp.dot(h.astype(BF16), wg_ref[...], preferred_element_type=F32) + bg_ref[...])
    ya = jnp.dot(a_ref[...], sco_ref[...], preferred_element_type=F32)
    yh = jnp.dot(gh_ref[...], hyo_ref[...], preferred_element_type=F32)
    am = (sm_ref[...].astype(F32) * att_ref[...].astype(F32)).astype(BF16)
    ym = jnp.dot(am, mlo_ref[...], preferred_element_type=F32)
    y = gates[:, 0:d] * ya + gates[:, d:2 * d] * yh + gates[:, 2 * d:3 * d] * ym
    o = jnp.dot(y.astype(BF16), wo_ref[...], preferred_element_type=F32)
    xn = x + gt_ref[...] * o
    if final:
        xn = _rms(xn, fg_ref[...])
    o_ref[...] = xn


def _merge_call(x2d, seq_len, mods, g, lw, a, gh, att, sm, final_g, final):
    n_rows, d = x2d.shape
    tm = min(512, seq_len)
    tiles_per_seq = seq_len // tm
    scale, shift, gate = mods
    g2 = g.reshape(1, 1, d)
    fg = final_g.reshape(1, d)
    tok = lambda wd: pl.BlockSpec((tm, wd), lambda i: (i, 0))
    consts = [lw['wg'], lw['bg'], lw['sc_out'], lw['hy_out'], lw['mla_out'], lw['w_o'], fg]
    in_specs = [tok(d), tok(W_CONV), tok(W_HYENA), tok(W_MLA), tok(W_MLA),
                _mod_spec(g2, tiles_per_seq), _mod_spec(scale, tiles_per_seq),
                _mod_spec(shift, tiles_per_seq), _mod_spec(gate, tiles_per_seq)]
    in_specs += [_full_spec(c) for c in consts]
    return pl.pallas_call(
        functools.partial(_merge_kernel, final=final),
        grid=(n_rows // tm,),
        in_specs=in_specs,
        out_specs=tok(d),
        out_shape=jax.ShapeDtypeStruct((n_rows, d), F32),
        compiler_params=_cparams(("parallel",)),
        name="merge_final" if final else "merge",
    )(x2d, a, gh, att, sm, g2, scale, shift, gate, *consts)


def _rope_swap_cols(w):
    half = QK_ROPE // 2
    return jnp.concatenate([-w[..., half:], w[..., :half]], axis=-1)


def _layer_weights(i, w_in, b_in, mla_q_norm, mla_w_uq, mla_kv_norm, mla_w_ukv, sc_out, hy_out,
                   mla_out, w_o):
    wi, bi = w_in[i], b_in[i]
    d = wi.shape[0]
    lw = {}
    lw['w1'] = jnp.concatenate([wi[:, O_XIN:O_HPROJ], wi[:, O_ZM:O_GATES]], axis=1).astype(BF16)
    lw['b1'] = jnp.concatenate([bi[O_XIN:O_HPROJ], bi[O_ZM:O_GATES]])[None, :]
    lw['w2'] = wi[:, O_HPROJ:O_CQ].astype(BF16)
    lw['b2'] = bi[O_HPROJ:O_CQ][None, :]
    wkr, bkr = wi[:, O_KR:O_ZM], bi[O_KR:O_ZM]
    zw = lambda n: jnp.zeros((d, n), F32)
    zb = lambda n: jnp.zeros((n,), F32)
    lw['w3'] = jnp.concatenate([wi[:, O_CQ:O_KR], wkr, zw(96), _rope_swap_cols(wkr), zw(96)], axis=1).astype(BF16)
    lw['b3'] = jnp.concatenate([bi[O_CQ:O_KR], bkr, zb(96), _rope_swap_cols(bkr), zb(96)])[None, :]
    lw['qg'] = mla_q_norm[i][None, :]
    lw['kvg'] = mla_kv_norm[i][None, :]
    wuq = mla_w_uq[i].reshape(Q_LORA, N_HEADS, QK_NOPE + QK_ROPE)
    pad = jnp.zeros((Q_LORA, N_HEADS, HEAD_PAD - QK_NOPE - QK_ROPE), F32)
    lw['wq'] = jnp.concatenate([wuq, pad], axis=-1).reshape(Q_LORA, N_HEADS * HEAD_PAD).astype(BF16)
    zn = jnp.zeros((Q_LORA, N_HEADS, QK_NOPE), F32)
    lw['wqs'] = jnp.concatenate([zn, _rope_swap_cols(wuq[..., QK_NOPE:]), pad], axis=-1).reshape(
        Q_LORA, N_HEADS * HEAD_PAD).astype(BF16)
    wukv = mla_w_ukv[i].reshape(KV_LORA, N_HEADS, QK_NOPE + V_HEAD)
    padk = jnp.zeros((KV_LORA, N_HEADS, HEAD_PAD - QK_NOPE), F32)
    lw['wk'] = jnp.concatenate([wukv[..., :QK_NOPE], padk], axis=-1).reshape(
        KV_LORA, N_HEADS * HEAD_PAD).astype(BF16)
    padv = jnp.zeros((KV_LORA, N_HEADS, HEAD_PAD - V_HEAD), F32)
    lw['wv'] = jnp.concatenate([wukv[..., QK_NOPE:], padv], axis=-1).reshape(
        KV_LORA, N_HEADS * HEAD_PAD).astype(BF16)
    lw['vb'] = jnp.zeros((N_HEADS, HEAD_PAD), F32).at[:, V_HEAD].set(1.0).reshape(1, N_HEADS * HEAD_PAD)
    e2 = jnp.zeros((QK_ROPE, N_HEADS, HEAD_PAD), F32).at[:, :, QK_NOPE:QK_NOPE + QK_ROPE].set(
        jnp.eye(QK_ROPE, dtype=F32)[:, None, :])
    lw['e2'] = e2.reshape(QK_ROPE, N_HEADS * HEAD_PAD).astype(BF16)
    lw['wg'] = wi[:, O_GATES:].astype(BF16)
    lw['bg'] = bi[O_GATES:][None, :]
    lw['sc_out'] = sc_out[i].astype(BF16)
    lw['hy_out'] = hy_out[i].astype(BF16)
    lw['mla_out'] = mla_out[i].astype(BF16)
    lw['w_o'] = w_o[i].astype(BF16)
    return lw


def _rope_tables(seq_len, use_rope):
    scale = (QK_NOPE + QK_ROPE) ** -0.5 * math.log2(math.e)
    if use_rope:
        rows = seq_len // GRID_W
        row = jnp.broadcast_to(jnp.arange(rows, dtype=F32)[:, None], (rows, GRID_W)).reshape(seq_len)
        col = jnp.broadcast_to(jnp.arange(GRID_W, dtype=F32)[None, :], (rows, GRID_W)).reshape(seq_len)
        n_f = QK_ROPE // 4
        inv = ROPE_BASE ** (-jnp.arange(n_f, dtype=F32) / n_f)
        ang = jnp.concatenate([row[:, None] * inv, col[:, None] * inv], axis=-1)
        cos, sin = jnp.cos(ang), jnp.sin(ang)
    else:
        cos = jnp.ones((seq_len, QK_ROPE // 2), F32)
        sin = jnp.zeros((seq_len, QK_ROPE // 2), F32)
    cosk = jnp.concatenate([cos, cos], axis=-1)
    sink = jnp.concatenate([sin, sin], axis=-1)
    ones = jnp.ones((seq_len, QK_NOPE), F32)
    zeros = jnp.zeros((seq_len, QK_NOPE), F32)
    tail = jnp.zeros((seq_len, HEAD_PAD - QK_NOPE - QK_ROPE), F32)
    cosq = jnp.tile(jnp.concatenate([ones, cosk, tail], axis=-1), (1, N_HEADS)) * scale
    sinq = jnp.tile(jnp.concatenate([zeros, sink, tail], axis=-1), (1, N_HEADS)) * scale
    return cosq, sinq, cosk, sink


def kernel(x, c, ctx, c_ctx, ada_w, ada_b, norm_g, w_in, b_in, sc_conv_w, sc_conv_b, sc_out, hy_conv_w,
           hy_conv_b, hy_w1, hy_b1, hy_w2, hy_b2, hy_w3, hy_freq, hy_skip, hy_out, mla_q_norm, mla_w_uq,
           mla_kv_norm, mla_w_ukv, mla_out, w_o, final_g):
    bsz, seq, d = x.shape
    lc = ctx.shape[1]
    depth = ada_w.shape[0]
    nh = seq // N2
    assert d == D_MODEL and bsz % 2 == 0 and nh % SUBLANE == 0 and bsz <= 7

    cc = jnp.zeros((8, d), F32).at[:bsz].set(c).at[bsz].set(c_ctx)
    mods = _ada_mods(cc, ada_w, ada_b)

    tabs = _dft_tables(nh)
    k_un, ssum = _filter_gen(seq, depth, True, hy_w1, hy_b1, hy_w2, hy_b2, hy_w3, hy_freq)
    kf = _filter_spectrum(tabs, nh, k_un, ssum)
    if depth > 1:
        kc_un, sc_sum = _filter_gen(lc, depth - 1, False, hy_w1, hy_b1, hy_w2, hy_b2, hy_w3, hy_freq)
        ctabs = _ctx_tables(lc)

    rope_l = _rope_tables(seq, True)
    rope_c = _rope_tables(lc, False)

    x_lat = x.reshape(bsz * seq, d)
    x_ctx = ctx.reshape(bsz * lc, d)
    for i in range(depth):
        last = i == depth - 1
        lw = _layer_weights(i, w_in, b_in, mla_q_norm, mla_w_uq, mla_kv_norm, mla_w_ukv, sc_out,
                            hy_out, mla_out, w_o)
        m = mods[i]
        split = lambda r: tuple(r[:, None, j * d:(j + 1) * d] for j in range(3))
        shift_l, scale_l, gate_l = split(m[:bsz])
        shift_c, scale_c, gate_c = split(m[bsz:bsz + 1])
        g = norm_g[i]
        scw, scb = sc_conv_w[i], sc_conv_b[i][None, :]
        hcw, hcb = hy_conv_w[i], hy_conv_b[i][None, :]

        hq = N_HEADS * HEAD_PAD
        q_l, k_all, v_all = _kp3_call(x_lat, bsz, seq, seq + lc, 0, (scale_l, shift_l), g, lw, rope_l)
        q_c, k_all, v_all = _kp3_call(x_ctx, bsz, lc, seq + lc, seq, (scale_c, shift_c), g, lw, rope_c,
                                      kv_bufs=(k_all, v_all))
        att_l = _attention(q_l.reshape(bsz, seq, hq), k_all, v_all).reshape(bsz * seq, W_MLA)

        a_l, sm_l = _kp_conv_call(_kp1_kernel, "kp1_conv", x_lat, seq, (scale_l, shift_l), g,
                                  lw['w1'], lw['b1'], scw, scb, (W_CONV, W_MLA))
        v_h, x1_h, m2_h = _kp_conv_call(_kp2_kernel, "kp2_hyena", x_lat, seq, (scale_l, shift_l), g,
                                        lw['w2'], lw['b2'], hcw, hcb, (W_HYENA,) * 3)
        r3 = lambda t: t.reshape(bsz, seq, W_HYENA)
        gh_l = _hyena_long(tabs, nh, kf, i, r3(v_h), r3(x1_h), r3(m2_h), hy_skip[i])
        new_lat = _merge_call(x_lat, seq, (scale_l, shift_l, gate_l), g, lw, a_l,
                              gh_l.reshape(bsz * seq, W_HYENA), att_l, sm_l, final_g, last)
        if not last:
            att_c = _attention(q_c.reshape(bsz, lc, hq), k_all, v_all, kv_row0=seq,
                               kv_len=lc).reshape(bsz * lc, W_MLA)
            a_c, sm_c = _kp_conv_call(_kp1_kernel, "kp1_conv_ctx", x_ctx, lc, (scale_c, shift_c), g,
                                      lw['w1'], lw['b1'], scw, scb, (W_CONV, W_MLA))
            v_hc, x1_hc, m2_hc = _kp_conv_call(_kp2_kernel, "kp2_hyena_ctx", x_ctx, lc,
                                               (scale_c, shift_c), g, lw['w2'], lw['b2'], hcw, hcb,
                                               (W_HYENA,) * 3)
            rc = lambda t: t.reshape(bsz, lc, W_HYENA)
            gh_c = _ctx_hyena(ctabs, kc_un, sc_sum, i, rc(v_hc), rc(x1_hc), rc(m2_hc), hy_skip[i])
            x_ctx = _merge_call(x_ctx, lc, (scale_c, shift_c, gate_c), g, lw, a_c,
                                gh_c.reshape(bsz * lc, W_HYENA), att_c, sm_c, final_g, False)
        x_lat = new_lat
    return x_lat.reshape(bsz, seq, d)
```

```python
import functools
import math

import jax
import jax.numpy as jnp
from jax import lax
from jax.experimental import pallas as pl
from jax.experimental.pallas import tpu as pltpu

F32 = jnp.float32
BF16 = jnp.bfloat16

D_MODEL = 1024
DEPTH = 4
GRID_W = 64
W_CONV = 512
W_HYENA = 512
HYENA_ORDER = 2
HYENA_EMB = 33
HYENA_BANDS = (HYENA_EMB - 1) // 2
HYENA_HID = 64
HYENA_FAST_DECAY = 0.3
HYENA_SLOW_DECAY = 1.5
HYENA_TARGET = 1e-2
N_HEADS = 8
QK_NOPE = 64
QK_ROPE = 32
V_HEAD = 64
Q_LORA = 384
KV_LORA = 256
W_MLA = N_HEADS * V_HEAD
ROPE_BASE = 10000.0
N_BRANCH = 3
EPS = 1e-6

O_XIN, O_GB, O_GC, O_ZA = 0, 512, 1024, 1536
O_HPROJ, O_ZH = 2048, 3584
O_CQ, O_CKV, O_KR, O_ZM, O_GATES = 4096, 4480, 4736, 4768, 5280
N_IN = 8352

LANE = 128
SUBLANE = 8
HEAD_PAD = 128
N2 = 128
VMEM_LIMIT = 56 * 1024 * 1024


def _cparams(sem):
    return pltpu.CompilerParams(dimension_semantics=sem, vmem_limit_bytes=VMEM_LIMIT)


def _ada_kernel(c_ref, w_ref, b_ref, o_ref):
    c = c_ref[...]
    s = c * jax.nn.sigmoid(c)
    o_ref[...] = jnp.dot(s, w_ref[...], preferred_element_type=F32,
                         precision=lax.Precision.HIGHEST) + b_ref[...]


def _ada_mods(cc, ada_w, ada_b):
    depth = ada_w.shape[0]
    d = cc.shape[1]
    return pl.pallas_call(
        _ada_kernel,
        grid=(depth, 3),
        in_specs=[pl.BlockSpec((8, d), lambda l, j: (0, 0)),
                  pl.BlockSpec((None, d, d), lambda l, j: (l, 0, j)),
                  pl.BlockSpec((None, 1, d), lambda l, j: (l, 0, j))],
        out_specs=pl.BlockSpec((None, 8, d), lambda l, j: (l, 0, j)),
        out_shape=jax.ShapeDtypeStruct((depth, 8, 3 * d), F32),
        compiler_params=_cparams(("parallel", "parallel")),
        name="ada_mods",
    )(cc, ada_w, ada_b.reshape(depth, 1, 3 * d))


def _mod_norm(x, g, scale, shift):
    y = x * lax.rsqrt(jnp.mean(x * x, axis=-1, keepdims=True) + EPS)
    return (y * g) * (1.0 + scale) + shift


def _rms(x, g):
    return x * lax.rsqrt(jnp.mean(x * x, axis=-1, keepdims=True) + EPS) * g


def _silu(x):
    return x * jax.nn.sigmoid(x)


def _dwconv3_ext(u_ext, w, b, first, last, tm):
    n = tm + 2 * SUBLANE
    prev = pltpu.roll(u_ext, 1, 0)[SUBLANE:SUBLANE + tm]
    nxt = pltpu.roll(u_ext, n - 1, 0)[SUBLANE:SUBLANE + tm]
    row = lax.broadcasted_iota(jnp.int32, (tm, 1), 0)
    prev = jnp.where(jnp.logical_and(first, row == 0), 0.0, prev)
    nxt = jnp.where(jnp.logical_and(last, row == tm - 1), 0.0, nxt)
    return prev * w[0:1] + u_ext[SUBLANE:SUBLANE + tm] * w[1:2] + nxt * w[2:3] + b


def _proj_ext(xm_ref, xp_ref, xn_ref, g_ref, sc_ref, sh_ref, w_ref, b_ref):
    x_ext = jnp.concatenate([xp_ref[...], xm_ref[...], xn_ref[...]], axis=0)
    h = _mod_norm(x_ext, g_ref[...], sc_ref[...], sh_ref[...])
    return jnp.dot(h.astype(BF16), w_ref[...], preferred_element_type=F32) + b_ref[...]


def _kp2_kernel(xm_ref, xp_ref, xn_ref, g_ref, sc_ref, sh_ref, w_ref, b_ref, cw_ref, cb_ref,
                v_ref, x1_ref, m2_ref, *, tm, tiles_per_seq):
    i = pl.program_id(0)
    first = (i % tiles_per_seq) == 0
    last = (i % tiles_per_seq) == tiles_per_seq - 1
    p = _proj_ext(xm_ref, xp_ref, xn_ref, g_ref, sc_ref, sh_ref, w_ref, b_ref)
    u = _dwconv3_ext(p[:, 0:3 * W_HYENA], cw_ref[...], cb_ref[...], first, last, tm)
    zh = p[SUBLANE:SUBLANE + tm, 3 * W_HYENA:4 * W_HYENA]
    v_ref[...] = u[:, 0:W_HYENA].astype(v_ref.dtype)
    x1_ref[...] = u[:, W_HYENA:2 * W_HYENA].astype(x1_ref.dtype)
    m2_ref[...] = (_silu(zh) * u[:, 2 * W_HYENA:3 * W_HYENA]).astype(m2_ref.dtype)


def _kp3_kernel(xm_ref, g_ref, sc_ref, sh_ref, w_ref, b_ref, qg_ref, kvg_ref, wq_ref, wqs_ref,
                wk_ref, wv_ref, vb_ref, e2_ref, cosq_ref, sinq_ref, cosk_ref, sink_ref,
                q_ref, k_ref, v_ref):
    h = _mod_norm(xm_ref[...], g_ref[...], sc_ref[...], sh_ref[...])
    p = jnp.dot(h.astype(BF16), w_ref[...], preferred_element_type=F32) + b_ref[...]
    cq = _rms(p[:, 0:Q_LORA], qg_ref[...]).astype(BF16)
    ckv = _rms(p[:, Q_LORA:Q_LORA + KV_LORA], kvg_ref[...]).astype(BF16)
    kr = p[:, 640:640 + QK_ROPE]
    krs = p[:, 768:768 + QK_ROPE]
    qa = jnp.dot(cq, wq_ref[...], preferred_element_type=F32)
    qb = jnp.dot(cq, wqs_ref[...], preferred_element_type=F32)
    q_ref[...] = (qa * cosq_ref[...] + qb * sinq_ref[...]).astype(q_ref.dtype)
    kr_rot = (kr * cosk_ref[...] + krs * sink_ref[...]).astype(BF16)
    kn = jnp.dot(ckv, wk_ref[...], preferred_element_type=F32)
    krp = jnp.dot(kr_rot, e2_ref[...], preferred_element_type=F32)
    k_ref[...] = (kn + krp).astype(k_ref.dtype)
    v_ref[...] = (jnp.dot(ckv, wv_ref[...], preferred_element_type=F32) + vb_ref[...]).astype(v_ref.dtype)


def _mod_spec(mod, tiles_per_seq):
    d = mod.shape[-1]
    if mod.shape[0] == 1:
        return pl.BlockSpec((None, 1, d), lambda i: (0, 0, 0))
    return pl.BlockSpec((None, 1, d), lambda i: (i // tiles_per_seq, 0, 0))


def _full_spec(a):
    nd = a.ndim
    return pl.BlockSpec(a.shape, lambda i: (0,) * nd)


def _halo_specs(n_rows, tm, d):
    r = tm // SUBLANE
    nb = n_rows // SUBLANE
    return [pl.BlockSpec((tm, d), lambda i: (i, 0)),
            pl.BlockSpec((SUBLANE, d), lambda i: (jnp.maximum(i * r - 1, 0), 0)),
            pl.BlockSpec((SUBLANE, d), lambda i: (jnp.minimum((i + 1) * r, nb - 1), 0))]


def _kp_conv_call(kern, name, x2d, seq_len, mods, g, w, b, cw, cb, out_widths):
    n_rows, d = x2d.shape
    tm = min(512, seq_len)
    tiles_per_seq = seq_len // tm
    scale, shift = mods
    g2 = g.reshape(1, 1, d)
    in_specs = _halo_specs(n_rows, tm, d) + [
        _mod_spec(g2, tiles_per_seq), _mod_spec(scale, tiles_per_seq), _mod_spec(shift, tiles_per_seq),
        _full_spec(w), _full_spec(b), _full_spec(cw), _full_spec(cb)]
    return pl.pallas_call(
        functools.partial(kern, tm=tm, tiles_per_seq=tiles_per_seq),
        grid=(n_rows // tm,),
        in_specs=in_specs,
        out_specs=[pl.BlockSpec((tm, wd), lambda i: (i, 0)) for wd in out_widths],
        out_shape=[jax.ShapeDtypeStruct((n_rows, wd), BF16) for wd in out_widths],
        compiler_params=_cparams(("parallel",)),
        name=name,
    )(x2d, x2d, x2d, g2, scale, shift, w, b, cw, cb)


def _kp3_aliased_kernel(*refs):
    _kp3_kernel(*refs[:-5], *refs[-3:])


def _kp3_call(x2d, bsz, seq_len, kv_len, kv_row0, mods, g, lw, tabs, kv_bufs=None):
    n_rows, d = x2d.shape
    tm = min(512, seq_len)
    tiles_per_seq = seq_len // tm
    blk0 = kv_row0 // tm
    assert kv_row0 % tm == 0
    scale, shift = mods
    g2 = g.reshape(1, 1, d)
    cosq, sinq, cosk, sink = tabs
    consts = [lw['w3'], lw['b3'], lw['qg'], lw['kvg'], lw['wq'], lw['wqs'], lw['wk'], lw['wv'], lw['vb'],
              lw['e2']]

    def tab_spec(t):
        return pl.BlockSpec((tm, t.shape[1]), lambda i: (i % tiles_per_seq, 0))

    in_specs = [pl.BlockSpec((tm, d), lambda i: (i, 0)),
                _mod_spec(g2, tiles_per_seq), _mod_spec(scale, tiles_per_seq), _mod_spec(shift, tiles_per_seq)]
    in_specs += [_full_spec(a) for a in consts]
    in_specs += [tab_spec(t) for t in (cosq, sinq, cosk, sink)]
    args = [x2d, g2, scale, shift, *consts, cosq, sinq, cosk, sink]
    hq = N_HEADS * HEAD_PAD
    kv_spec = pl.BlockSpec((None, tm, hq), lambda i: (i // tiles_per_seq, blk0 + i % tiles_per_seq, 0))
    kv_shape = jax.ShapeDtypeStruct((bsz, kv_len, hq), BF16)
    aliases = {}
    kern = _kp3_kernel
    if kv_bufs is not None:
        kern = _kp3_aliased_kernel
        in_specs += [pl.BlockSpec(memory_space=pl.ANY)] * 2
        aliases = {len(args): 1, len(args) + 1: 2}
        args += list(kv_bufs)
    return pl.pallas_call(
        kern,
        grid=(n_rows // tm,),
        in_specs=in_specs,
        out_specs=[pl.BlockSpec((tm, hq), lambda i: (i, 0)), kv_spec, kv_spec],
        out_shape=[jax.ShapeDtypeStruct((n_rows, hq), BF16), kv_shape, kv_shape],
        input_output_aliases=aliases,
        compiler_params=_cparams(("parallel",)),
        name="kp3_mla" if kv_bufs is None else "kp3_mla_ctx",
    )(*args)


ATTN_CHAINS = 4


def _attn_kernel(q_ref, k_ref, v_ref, o_ref, s_ref, p_ref, m_ref, acc_ref):
    h = pl.program_id(2)
    ki = pl.program_id(3)

    @pl.when(ki == 0)
    def _():
        m_ref[...] = jnp.full(m_ref.shape, -jnp.inf, F32)
        acc_ref[...] = jnp.zeros(acc_ref.shape, F32)

    hq = q_ref.shape[0] // ATTN_CHAINS
    chains = [pl.ds(c * hq, hq) for c in range(ATTN_CHAINS)]
    for rows in chains:
        s_ref[rows, :] = lax.dot_general(q_ref[rows, :], k_ref[...], (((1,), (1,)), ((), ())),
                                         preferred_element_type=F32)
    alphas = []
    reps = s_ref.shape[1] // LANE
    for rows in chains:
        m_prev = m_ref[rows, :]
        m_new = jnp.maximum(m_prev, jnp.max(s_ref[rows, :], axis=-1, keepdims=True))
        alphas.append(jnp.exp2(m_prev - m_new))
        m_ref[rows, :] = m_new
        p_ref[rows, :] = jnp.exp2((s_ref[rows, :] - pltpu.repeat(m_new, reps, axis=1)).astype(BF16))
    for rows, alpha in zip(chains, alphas):
        acc_ref[rows, :] = acc_ref[rows, :] * alpha + jnp.dot(p_ref[rows, :], v_ref[...],
                                                              preferred_element_type=F32)

    @pl.when(ki == pl.num_programs(3) - 1)
    def _():
        acc = acc_ref[...]
        o = (acc[:, 0:V_HEAD] / acc[:, V_HEAD:V_HEAD + 1]).astype(o_ref.dtype)

        @pl.when(h % 2 == 0)
        def _():
            o_ref[:, 0:V_HEAD] = o

        @pl.when(h % 2 == 1)
        def _():
            o_ref[:, V_HEAD:2 * V_HEAD] = o


MXU_DIM = 256
Q_TILE = 2048
KV_TILE_CAP = 2816


def _kv_tile(lk):
    for step in (MXU_DIM, LANE):
        cands = [t for t in range(step, min(lk, KV_TILE_CAP) + 1, step) if lk % t == 0]
        if cands:
            return cands[-1]
    return lk


def _attention(q, k, v, kv_row0=0, kv_len=None):
    b, lq, _ = q.shape
    lk = k.shape[1] if kv_len is None else kv_len
    tq = min(Q_TILE, lq)
    tk = _kv_tile(lk)
    assert kv_row0 % tk == 0
    kb0 = kv_row0 // tk
    grid = (b, lq // tq, N_HEADS, lk // tk)
    return pl.pallas_call(
        _attn_kernel,
        grid=grid,
        in_specs=[pl.BlockSpec((None, tq, HEAD_PAD), lambda bi, qi, h, ki: (bi, qi, h)),
                  pl.BlockSpec((None, tk, HEAD_PAD), lambda bi, qi, h, ki: (bi, kb0 + ki, h)),
                  pl.BlockSpec((None, tk, HEAD_PAD), lambda bi, qi, h, ki: (bi, kb0 + ki, h))],
        out_specs=pl.BlockSpec((None, tq, 2 * V_HEAD), lambda bi, qi, h, ki: (bi, qi, h // 2)),
        out_shape=jax.ShapeDtypeStruct((b, lq, W_MLA), BF16),
        scratch_shapes=[pltpu.VMEM((tq, tk), F32), pltpu.VMEM((tq, tk), BF16),
                        pltpu.VMEM((tq, LANE), F32), pltpu.VMEM((tq, HEAD_PAD), F32)],
        compiler_params=_cparams(("parallel", "parallel", "arbitrary", "arbitrary")),
        name="mla_attention",
    )(q, k, v)


COL_BWD = HYENA_EMB
COL_DROP = HYENA_EMB + 1


def _filt_kernel(z_ref, w1_ref, b1_ref, w2_ref, b2_ref, fr_ref, w3_ref, dl_ref, k_ref, s_ref):
    i = pl.program_id(1)
    hp = lax.Precision.HIGHEST
    wc = HYENA_ORDER * W_HYENA
    z = z_ref[...]
    fr = fr_ref[...]
    hid = jnp.sin(fr * (jnp.dot(z, w1_ref[...], preferred_element_type=F32, precision=hp) + b1_ref[...]))
    hid = jnp.sin(fr * (jnp.dot(hid, w2_ref[...], preferred_element_type=F32, precision=hp) + b2_ref[...]))
    h2 = jnp.dot(hid.astype(BF16), w3_ref[...], preferred_element_type=F32)
    h = jnp.where(z[:, COL_BWD:COL_BWD + 1] > 0.5, h2[:, wc:], h2[:, :wc])
    t = z[:, 0:1]
    h = h * jnp.exp(-t * dl_ref[...])

    @pl.when(i == 0)
    def _():
        s_ref[...] = jnp.zeros(s_ref.shape, F32)

    s_ref[...] += jnp.sum(jnp.abs(h), axis=0, keepdims=True)
    k_ref[...] = jnp.where(z[:, COL_DROP:COL_DROP + 1] > 0.5, 0.0, h)


def _filter_tables(seq_len, permuted):
    t = jnp.linspace(0.0, 1.0, seq_len, dtype=F32)[:, None]
    w = 2.0 * math.pi * jnp.arange(seq_len, dtype=F32)[:, None] / seq_len
    f = jnp.linspace(1e-4, HYENA_BANDS - 1, HYENA_BANDS, dtype=F32)[None, :]
    z = jnp.concatenate([t, jnp.cos(f * w), -jnp.sin(f * w)], axis=-1)
    z2 = jnp.concatenate([z, z[0:1], jnp.flip(z[1:], axis=0)], axis=0)
    n = jnp.arange(2 * seq_len)
    flags = jnp.stack([n >= seq_len, n == seq_len], axis=-1).astype(F32)
    zz = jnp.concatenate([z2, flags], axis=-1)
    zz = jnp.pad(zz, ((0, 0), (0, LANE - zz.shape[1])))
    if permuted:
        zz = zz.reshape(2 * seq_len // N2, N2, LANE).transpose(1, 0, 2).reshape(2 * seq_len, LANE)
    return zz


def _hyena_deltas():
    max_decay = math.log(HYENA_TARGET) / HYENA_FAST_DECAY
    min_decay = math.log(HYENA_TARGET) / HYENA_SLOW_DECAY
    deltas = jnp.abs(jnp.linspace(min_decay, max_decay, W_HYENA, dtype=F32))
    return jnp.tile(deltas, HYENA_ORDER)[None, :]


def _filter_gen(seq_len, n_layers, permuted, hy_w1, hy_b1, hy_w2, hy_b2, hy_w3, hy_freq):
    zz = _filter_tables(seq_len, permuted)
    tmf = min(512, seq_len)
    tiles_half = seq_len // tmf
    wc = HYENA_ORDER * W_HYENA
    w1p = jnp.pad(hy_w1[:n_layers], ((0, 0), (0, LANE - HYENA_EMB), (0, 0)))
    w3r = hy_w3[:n_layers].astype(BF16)
    r1 = lambda a: a[:n_layers].reshape(n_layers, 1, HYENA_HID)
    return pl.pallas_call(
        _filt_kernel,
        grid=(n_layers, 2 * tiles_half),
        in_specs=[pl.BlockSpec((tmf, LANE), lambda l, i: (i, 0)),
                  pl.BlockSpec((None, LANE, HYENA_HID), lambda l, i: (l, 0, 0)),
                  pl.BlockSpec((None, 1, HYENA_HID), lambda l, i: (l, 0, 0)),
                  pl.BlockSpec((None, HYENA_HID, HYENA_HID), lambda l, i: (l, 0, 0)),
                  pl.BlockSpec((None, 1, HYENA_HID), lambda l, i: (l, 0, 0)),
                  pl.BlockSpec((None, 1, HYENA_HID), lambda l, i: (l, 0, 0)),
                  pl.BlockSpec((None, HYENA_HID, 2 * wc), lambda l, i: (l, 0, 0)),
                  pl.BlockSpec((1, wc), lambda l, i: (0, 0))],
        out_specs=[pl.BlockSpec((tmf, wc), lambda l, i: (i, l)),
                   pl.BlockSpec((None, 1, wc), lambda l, i: (l, 0, 0))],
        out_shape=[jax.ShapeDtypeStruct((2 * seq_len, n_layers * wc), F32),
                   jax.ShapeDtypeStruct((n_layers, 1, wc), F32)],
        compiler_params=_cparams(("parallel", "arbitrary")),
        name="hyena_filter",
    )(zz, w1p, r1(hy_b1), hy_w2[:n_layers], r1(hy_b2), r1(hy_freq), w3r, _hyena_deltas())


def _angles(a, b, n):
    m = (a * b) % n
    return m.astype(F32) * (2.0 * math.pi / n)


def _dft_tables(nh):
    n1 = 2 * nh
    n = n1 * N2
    k1 = jnp.arange(n1, dtype=jnp.int32)
    th = _angles(k1[:, None], jnp.arange(nh, dtype=jnp.int32)[None, :], n1)
    c, s = jnp.cos(th), jnp.sin(th)
    w1_pair = jnp.concatenate([jnp.concatenate([c, s], 1), jnp.concatenate([-s, c], 1)], 0)
    thf = _angles(k1[:, None], k1[None, :], n1)
    w1_real = jnp.concatenate([jnp.cos(thf), -jnp.sin(thf)], 0)
    g1 = jnp.concatenate([jnp.concatenate([c.T, -s.T], 1), jnp.concatenate([s.T, c.T], 1)], 0) / n
    n2 = jnp.arange(N2, dtype=jnp.int32)
    al = _angles(k1[:, None], n2[None, :], n)
    be = _angles(n2[:, None], n2[None, :], N2)
    ca, sa = jnp.cos(al)[:, None, :], jnp.sin(al)[:, None, :]
    cb, sb = jnp.cos(be)[None], jnp.sin(be)[None]
    cp, sp = ca * cb - sa * sb, sa * cb + ca * sb
    f2 = jnp.concatenate([jnp.concatenate([cp, sp], 2), jnp.concatenate([-sp, cp], 2)], 1)
    cpt, spt = cp.transpose(0, 2, 1), sp.transpose(0, 2, 1)
    g2 = jnp.concatenate([jnp.concatenate([cpt, -spt], 2), jnp.concatenate([spt, cpt], 2)], 1)
    g1 = g1.reshape(2 * nh, 2, n1).transpose(0, 2, 1).reshape(2 * nh, 2 * n1)
    f2 = f2.reshape(n1, 2 * N2, 2, N2).transpose(0, 1, 3, 2).reshape(n1, 2 * N2, 2 * N2)
    g2 = g2.reshape(n1, 2, N2, 2 * N2).transpose(0, 2, 1, 3).reshape(n1, 2 * N2, 2 * N2)
    w1_pair = w1_pair.reshape(2, n1, 2 * nh).transpose(1, 0, 2).reshape(2 * n1, 2 * nh)
    w1_real = w1_real.reshape(2, n1, n1).transpose(1, 0, 2).reshape(2 * n1, n1)
    return dict(w1_pair=w1_pair.astype(BF16), w1_real=w1_real.astype(BF16), g1=g1.astype(BF16),
                f2=f2.astype(BF16), g2=g2.astype(BF16))


def _strided_rows(ref, start, size):
    parts = [ref[cc, pl.ds(start, size, stride=SUBLANE), :] for cc in range(ref.shape[0])]
    return parts[0] if len(parts) == 1 else jnp.concatenate(parts, axis=1)


def _pack_pairs(x):
    return pltpu.bitcast(x.astype(BF16), jnp.uint32)


def _unpack_pairs(packed):
    return pltpu.bitcast(packed, BF16)


def _store_chunks(ref, row0, val):
    for cc in range(ref.shape[0]):
        ref[cc, pl.ds(row0, SUBLANE), :] = val[:, cc * LANE:(cc + 1) * LANE]


def _stage_a_kernel(*refs, n2t, n1, has_inv, has_fwd, has_div):
    it = iter(refs)
    y_ref = next(it) if has_inv else None
    g1_ref = next(it) if has_inv else None
    u_ref = next(it)
    mul_ref = next(it) if has_inv else None
    skip_ref = next(it) if has_inv else None
    div_ref = next(it) if has_div else None
    w1_ref = next(it) if has_fwd else None
    e_ref = next(it) if has_inv else None
    a_ref = next(it) if has_fwd else None
    yy_ref = next(it) if has_inv else None
    if has_inv:
        for j in range(n2t):
            nb, jl = j // SUBLANE, j % SUBLANE
            rhs = _unpack_pairs(_strided_rows(y_ref.at[nb], jl, n1))
            yy_ref[j] = jnp.dot(g1_ref[...], rhs, preferred_element_type=F32)
        for j in range(n2t):
            e = mul_ref[j].astype(F32) * (yy_ref[j] + u_ref[j].astype(F32) * skip_ref[...])
            e_ref[j] = e.astype(e_ref.dtype)
    if has_fwd:
        for j in range(n2t):
            if has_inv:
                src = e_ref[j]
            elif has_div:
                src = (u_ref[j] / div_ref[...]).astype(BF16)
            else:
                src = u_ref[j].astype(BF16)
            a = jnp.dot(w1_ref[...], src, preferred_element_type=F32)
            packed = _pack_pairs(a)
            for kt in range(n1 // SUBLANE):
                _store_chunks(a_ref.at[kt], j * SUBLANE, packed[kt * SUBLANE:(kt + 1) * SUBLANE])


def _stage_a(tabs, nh, u, *, y=None, mul=None, skip=None, div=None, w1=None, want_fwd=True,
             ct=256, n2t=32, e_dtype=BF16, c_off=0):
    p, _, rows, c = u.shape
    n1 = 2 * nh
    has_inv = y is not None
    has_div = div is not None
    ct = min(ct, c)
    nbk = n2t // SUBLANE
    grid = (p, c // ct, N2 // n2t)
    args, specs = [], []
    if has_inv:
        args += [y, tabs['g1']]
        specs += [pl.BlockSpec((None, nbk, ct // LANE, n1 * SUBLANE, LANE),
                               lambda pi, ci, ni: (pi, ni, ci, 0, 0)),
                  pl.BlockSpec(tabs['g1'].shape, lambda pi, ci, ni: (0, 0))]
    args.append(u)
    specs.append(pl.BlockSpec((None, n2t, rows, ct), lambda pi, ci, ni: (pi, ni, 0, ci)))
    if has_inv:
        args += [mul, skip]
        specs += [pl.BlockSpec((None, n2t, rows, ct), lambda pi, ci, ni: (pi, ni, 0, ci)),
                  pl.BlockSpec((1, ct), lambda pi, ci, ni: (0, ci + c_off // ct))]
    if has_div:
        args.append(div)
        specs.append(pl.BlockSpec((1, ct), lambda pi, ci, ni: (0, ci)))
    if want_fwd:
        args.append(w1)
        specs.append(pl.BlockSpec(w1.shape, lambda pi, ci, ni: (0, 0)))
    out_shape, out_specs = [], []
    if has_inv:
        out_shape.append(jax.ShapeDtypeStruct(u.shape, e_dtype))
        out_specs.append(pl.BlockSpec((None, n2t, rows, ct), lambda pi, ci, ni: (pi, ni, 0, ci)))
    if want_fwd:
        out_shape.append(jax.ShapeDtypeStruct((p, n1 // SUBLANE, c // LANE, N2 * SUBLANE, LANE), jnp.uint32))
        out_specs.append(pl.BlockSpec((None, n1 // SUBLANE, ct // LANE, n2t * SUBLANE, LANE),
                                      lambda pi, ci, ni: (pi, 0, ci, ni, 0)))
    outs = pl.pallas_call(
        functools.partial(_stage_a_kernel, n2t=n2t, n1=n1, has_inv=has_inv, has_fwd=want_fwd,
                          has_div=has_div),
        grid=grid, in_specs=specs, out_specs=out_specs, out_shape=out_shape,
        scratch_shapes=[pltpu.VMEM((n2t, rows, ct), F32)] if has_inv else [],
        compiler_params=_cparams(("parallel", "parallel", "parallel")),
        name="hyena_stage_a" + ("_inv" if has_inv else "") + ("_fwd" if want_fwd else ""),
    )(*args)
    return outs


def _stage_b_kernel(*refs, filt_only):
    if filt_only:
        a_ref, f2_ref, o_ref = refs
        for kl in range(SUBLANE):
            rhs = _unpack_pairs(_strided_rows(a_ref, kl, N2))
            o_ref[kl] = jnp.dot(f2_ref[kl], rhs, preferred_element_type=F32).astype(o_ref.dtype)
        return
    a_ref, f2_ref, g2_ref, kf_ref, o_ref, t_ref, z_ref = refs
    for kl in range(SUBLANE):
        rhs = _unpack_pairs(_strided_rows(a_ref, kl, N2))
        t_ref[kl] = jnp.dot(f2_ref[kl], rhs, preferred_element_type=F32)
    for kl in range(SUBLANE):
        kf = kf_ref[kl].astype(F32)
        tre, tim = t_ref[kl, :N2], t_ref[kl, N2:]
        kre, kim = kf[:N2], kf[N2:]
        z_ref[kl, :N2] = (tre * kre - tim * kim).astype(BF16)
        z_ref[kl, N2:] = (tre * kim + tim * kre).astype(BF16)
    for kl in range(SUBLANE):
        y = jnp.dot(g2_ref[kl], z_ref[kl], preferred_element_type=F32)
        packed = _pack_pairs(y)
        for nt in range(N2 // SUBLANE):
            _store_chunks(o_ref.at[nt], kl * SUBLANE, packed[nt * SUBLANE:(nt + 1) * SUBLANE])


def _stage_b(tabs, a, kf=None, *, c_off=0, ct=512):
    p, nkt, ncc, _, _ = a.shape
    c = ncc * LANE
    n1 = nkt * SUBLANE
    ct = min(ct, c)
    filt_only = kf is None
    grid = (nkt, c // ct, p)
    a_spec = pl.BlockSpec((None, None, ct // LANE, N2 * SUBLANE, LANE),
                          lambda kt, ci, pi: (pi, kt, ci, 0, 0))
    tab_spec = pl.BlockSpec((SUBLANE, 2 * N2, 2 * N2), lambda kt, ci, pi: (kt, 0, 0))
    if filt_only:
        return pl.pallas_call(
            functools.partial(_stage_b_kernel, filt_only=True),
            grid=grid, in_specs=[a_spec, tab_spec],
            out_specs=pl.BlockSpec((SUBLANE, 2 * N2, ct), lambda kt, ci, pi: (kt, 0, ci)),
            out_shape=jax.ShapeDtypeStruct((n1, 2 * N2, c), BF16),
            compiler_params=_cparams(("parallel", "parallel", "parallel")),
            name="hyena_stage_b_filter",
        )(a, tabs['f2'])
    return pl.pallas_call(
        functools.partial(_stage_b_kernel, filt_only=False),
        grid=grid,
        in_specs=[a_spec, tab_spec, tab_spec,
                  pl.BlockSpec((SUBLANE, 2 * N2, ct), lambda kt, ci, pi: (kt, 0, ci + c_off // ct))],
        out_specs=pl.BlockSpec((None, N2 // SUBLANE, ct // LANE, SUBLANE * SUBLANE, LANE),
                               lambda kt, ci, pi: (pi, 0, ci, kt, 0)),
        out_shape=jax.ShapeDtypeStruct((p, N2 // SUBLANE, c // LANE, n1 * SUBLANE, LANE), jnp.uint32),
        scratch_shapes=[pltpu.VMEM((SUBLANE, 2 * N2, ct), F32), pltpu.VMEM((SUBLANE, 2 * N2, ct), BF16)],
        compiler_params=_cparams(("parallel", "parallel", "parallel")),
        name="hyena_stage_b",
    )(a, tabs['f2'], tabs['g2'], kf)


def _permute_seq(a, nh):
    b, _, c = a.shape
    return a.reshape(b // 2, 2, nh, N2, c).transpose(0, 3, 1, 2, 4).reshape(b // 2, N2, 2 * nh, c)


def _unpermute_seq(a, nh):
    p, _, _, c = a.shape
    return a.reshape(p, N2, 2, nh, c).transpose(0, 2, 3, 1, 4).reshape(2 * p, nh * N2, c)


def _filter_spectrum(tabs, nh, k_un, ssum):
    n, call = k_un.shape
    kp = k_un.reshape(1, N2, 2 * nh, call)
    (a,) = _stage_a(tabs, nh, kp, div=ssum.reshape(1, call), w1=tabs['w1_real'], ct=512, n2t=16)
    return _stage_b(tabs, a)


def _hyena_long(tabs, nh, kf, layer, v, x1, m2, skip):
    vp, x1p, m2p = (_permute_seq(t, nh) for t in (v, x1, m2))
    c0 = layer * HYENA_ORDER * W_HYENA
    skip2 = skip.reshape(1, HYENA_ORDER * W_HYENA)
    (a1,) = _stage_a(tabs, nh, vp, w1=tabs['w1_pair'])
    y1 = _stage_b(tabs, a1, kf, c_off=c0)
    z, a2 = _stage_a(tabs, nh, vp, y=y1, mul=x1p, skip=skip2, w1=tabs['w1_pair'], c_off=0)
    y2 = _stage_b(tabs, a2, kf, c_off=c0 + W_HYENA)
    (gp,) = _stage_a(tabs, nh, z, y=y2, mul=m2p, skip=skip2, want_fwd=False, c_off=W_HYENA)
    return _unpermute_seq(gp, nh)


def _ctx_conv_kernel(v_ref, x1_ref, m2_ref, k_ref, s_ref, skip_ref, ff_ref, fk_ref, gi_ref, o_ref,
                     *, n):
    hp = lax.Precision.HIGHEST

    def conv(u, o):
        kfull = k_ref[:, o * W_HYENA:(o + 1) * W_HYENA] / s_ref[:, o * W_HYENA:(o + 1) * W_HYENA]
        kf = jnp.dot(fk_ref[...], kfull, preferred_element_type=F32, precision=hp)
        uf = jnp.dot(ff_ref[...], u, preferred_element_type=F32, precision=hp)
        ure, uim, kre, kim = uf[:n], uf[n:], kf[:n], kf[n:]
        z = jnp.concatenate([ure * kre - uim * kim, ure * kim + uim * kre], axis=0)
        y = jnp.dot(gi_ref[...], z, preferred_element_type=F32, precision=hp)
        return y + u * skip_ref[o:o + 1, :]

    v = v_ref[...].astype(F32)
    z1 = x1_ref[...].astype(F32) * conv(v, 0)
    o_ref[...] = (m2_ref[...].astype(F32) * conv(z1, 1)).astype(o_ref.dtype)


def _ctx_tables(lc):
    n = 2 * lc
    k = jnp.arange(n, dtype=jnp.int32)
    ph = _angles(k[:, None], k[None, :], n)
    c, s = jnp.cos(ph), jnp.sin(ph)
    fk = jnp.concatenate([c, -s], axis=0)
    ff = fk[:, :lc]
    gi = jnp.concatenate([c[:lc], -s[:lc]], axis=1) / n
    return ff, fk, gi


def _ctx_hyena(ctabs, k_un, ssum, layer, v, x1, m2, skip):
    b, lc, c = v.shape
    n = 2 * lc
    ff, fk, gi = ctabs
    wc = HYENA_ORDER * W_HYENA
    tok = pl.BlockSpec((None, lc, c), lambda bi: (bi, 0, 0))
    return pl.pallas_call(
        functools.partial(_ctx_conv_kernel, n=n),
        grid=(b,),
        in_specs=[tok, tok, tok,
                  pl.BlockSpec((n, wc), lambda bi: (0, layer)),
                  pl.BlockSpec((None, 1, wc), lambda bi: (layer, 0, 0)),
                  pl.BlockSpec((HYENA_ORDER, c), lambda bi: (0, 0)),
                  _full_spec(ff), _full_spec(fk), _full_spec(gi)],
        out_specs=tok,
        out_shape=jax.ShapeDtypeStruct((b, lc, c), BF16),
        compiler_params=_cparams(("parallel",)),
        name="ctx_hyena",
    )(v, x1, m2, k_un, ssum, skip, ff, fk, gi)


def _merge_kernel(xm_ref, xp_ref, xn_ref, gh_ref, att_ref, g_ref, sc_ref, sh_ref, gt_ref,
                  w1_ref, b1_ref, cw_ref, cb_ref, wg_ref, bg_ref, sco_ref, hyo_ref, mlo_ref, wo_ref,
                  fg_ref, o_ref, *, final, tm, tiles_per_seq):
    i = pl.program_id(0)
    first = (i % tiles_per_seq) == 0
    last = (i % tiles_per_seq) == tiles_per_seq - 1
    x = xm_ref[...]
    d = x.shape[-1]
    x_ext = jnp.concatenate([xp_ref[...], x, xn_ref[...]], axis=0)
    hb_ext = _mod_norm(x_ext, g_ref[...], sc_ref[...], sh_ref[...]).astype(BF16)
    p = jnp.dot(hb_ext, w1_ref[...], preferred_element_type=F32) + b1_ref[...]
    prod = p[:, 2 * W_CONV:3 * W_CONV] * p[:, 0:W_CONV]
    conv = _dwconv3_ext(prod, cw_ref[...], cb_ref[...], first, last, tm)
    pm = p[SUBLANE:SUBLANE + tm]
    a = (_silu(pm[:, 3 * W_CONV:4 * W_CONV]) * (pm[:, W_CONV:2 * W_CONV] * conv)).astype(BF16)
    sm = _silu(pm[:, 4 * W_CONV:4 * W_CONV + W_MLA])
    hb = hb_ext[SUBLANE:SUBLANE + tm]
    gates = jax.nn.sigmoid(jnp.dot(hb, wg_ref[...], preferred_element_type=F32) + bg_ref[...])
    ya = jnp.dot(a, sco_ref[...], preferred_element_type=F32)
    yh = jnp.dot(gh_ref[...], hyo_ref[...], preferred_element_type=F32)
    am = (sm * att_ref[...].astype(F32)).astype(BF16)
    ym = jnp.dot(am, mlo_ref[...], preferred_element_type=F32)
    y = gates[:, 0:d] * ya + gates[:, d:2 * d] * yh + gates[:, 2 * d:3 * d] * ym
    o = jnp.dot(y.astype(BF16), wo_ref[...], preferred_element_type=F32)
    xn = x + gt_ref[...] * o
    if final:
        xn = _rms(xn, fg_ref[...])
    o_ref[...] = xn


def _merge_call(x2d, seq_len, mods, g, lw, cw, cb, gh, att, final_g, final):
    n_rows, d = x2d.shape
    tm = min(512, seq_len)
    tiles_per_seq = seq_len // tm
    scale, shift, gate = mods
    g2 = g.reshape(1, 1, d)
    fg = final_g.reshape(1, d)
    tok = lambda wd: pl.BlockSpec((tm, wd), lambda i: (i, 0))
    consts = [lw['w1'], lw['b1'], cw, cb, lw['wg'], lw['bg'], lw['sc_out'], lw['hy_out'], lw['mla_out'],
              lw['w_o'], fg]
    in_specs = _halo_specs(n_rows, tm, d) + [
        tok(W_HYENA), tok(W_MLA),
        _mod_spec(g2, tiles_per_seq), _mod_spec(scale, tiles_per_seq),
        _mod_spec(shift, tiles_per_seq), _mod_spec(gate, tiles_per_seq)]
    in_specs += [_full_spec(c) for c in consts]
    return pl.pallas_call(
        functools.partial(_merge_kernel, final=final, tm=tm, tiles_per_seq=tiles_per_seq),
        grid=(n_rows // tm,),
        in_specs=in_specs,
        out_specs=tok(d),
        out_shape=jax.ShapeDtypeStruct((n_rows, d), F32),
        compiler_params=_cparams(("parallel",)),
        name="merge_final" if final else "merge",
    )(x2d, x2d, x2d, gh, att, g2, scale, shift, gate, *consts)


def _rope_swap_cols(w):
    half = QK_ROPE // 2
    return jnp.concatenate([-w[..., half:], w[..., :half]], axis=-1)


def _layer_weights(i, w_in, b_in, mla_q_norm, mla_w_uq, mla_kv_norm, mla_w_ukv, sc_out, hy_out,
                   mla_out, w_o):
    wi, bi = w_in[i], b_in[i]
    d = wi.shape[0]
    lw = {}
    lw['w1'] = jnp.concatenate([wi[:, O_XIN:O_HPROJ], wi[:, O_ZM:O_GATES]], axis=1).astype(BF16)
    lw['b1'] = jnp.concatenate([bi[O_XIN:O_HPROJ], bi[O_ZM:O_GATES]])[None, :]
    lw['w2'] = wi[:, O_HPROJ:O_CQ].astype(BF16)
    lw['b2'] = bi[O_HPROJ:O_CQ][None, :]
    wkr, bkr = wi[:, O_KR:O_ZM], bi[O_KR:O_ZM]
    zw = lambda n: jnp.zeros((d, n), F32)
    zb = lambda n: jnp.zeros((n,), F32)
    lw['w3'] = jnp.concatenate([wi[:, O_CQ:O_KR], wkr, zw(96), _rope_swap_cols(wkr), zw(96)], axis=1).astype(BF16)
    lw['b3'] = jnp.concatenate([bi[O_CQ:O_KR], bkr, zb(96), _rope_swap_cols(bkr), zb(96)])[None, :]
    lw['qg'] = mla_q_norm[i][None, :]
    lw['kvg'] = mla_kv_norm[i][None, :]
    wuq = mla_w_uq[i].reshape(Q_LORA, N_HEADS, QK_NOPE + QK_ROPE)
    pad = jnp.zeros((Q_LORA, N_HEADS, HEAD_PAD - QK_NOPE - QK_ROPE), F32)
    lw['wq'] = jnp.concatenate([wuq, pad], axis=-1).reshape(Q_LORA, N_HEADS * HEAD_PAD).astype(BF16)
    zn = jnp.zeros((Q_LORA, N_HEADS, QK_NOPE), F32)
    lw['wqs'] = jnp.concatenate([zn, _rope_swap_cols(wuq[..., QK_NOPE:]), pad], axis=-1).reshape(
        Q_LORA, N_HEADS * HEAD_PAD).astype(BF16)
    wukv = mla_w_ukv[i].reshape(KV_LORA, N_HEADS, QK_NOPE + V_HEAD)
    padk = jnp.zeros((KV_LORA, N_HEADS, HEAD_PAD - QK_NOPE), F32)
    lw['wk'] = jnp.concatenate([wukv[..., :QK_NOPE], padk], axis=-1).reshape(
        KV_LORA, N_HEADS * HEAD_PAD).astype(BF16)
    padv = jnp.zeros((KV_LORA, N_HEADS, HEAD_PAD - V_HEAD), F32)
    lw['wv'] = jnp.concatenate([wukv[..., QK_NOPE:], padv], axis=-1).reshape(
        KV_LORA, N_HEADS * HEAD_PAD).astype(BF16)
    lw['vb'] = jnp.zeros((N_HEADS, HEAD_PAD), F32).at[:, V_HEAD].set(1.0).reshape(1, N_HEADS * HEAD_PAD)
    e2 = jnp.zeros((QK_ROPE, N_HEADS, HEAD_PAD), F32).at[:, :, QK_NOPE:QK_NOPE + QK_ROPE].set(
        jnp.eye(QK_ROPE, dtype=F32)[:, None, :])
    lw['e2'] = e2.reshape(QK_ROPE, N_HEADS * HEAD_PAD).astype(BF16)
    lw['wg'] = wi[:, O_GATES:].astype(BF16)
    lw['bg'] = bi[O_GATES:][None, :]
    lw['sc_out'] = sc_out[i].astype(BF16)
    lw['hy_out'] = hy_out[i].astype(BF16)
    lw['mla_out'] = mla_out[i].astype(BF16)
    lw['w_o'] = w_o[i].astype(BF16)
    return lw


def _rope_tables(seq_len, use_rope):
    scale = (QK_NOPE + QK_ROPE) ** -0.5 * math.log2(math.e)
    if use_rope:
        rows = seq_len // GRID_W
        row = jnp.broadcast_to(jnp.arange(rows, dtype=F32)[:, None], (rows, GRID_W)).reshape(seq_len)
        col = jnp.broadcast_to(jnp.arange(GRID_W, dtype=F32)[None, :], (rows, GRID_W)).reshape(seq_len)
        n_f = QK_ROPE // 4
        inv = ROPE_BASE ** (-jnp.arange(n_f, dtype=F32) / n_f)
        ang = jnp.concatenate([row[:, None] * inv, col[:, None] * inv], axis=-1)
        cos, sin = jnp.cos(ang), jnp.sin(ang)
    else:
        cos = jnp.ones((seq_len, QK_ROPE // 2), F32)
        sin = jnp.zeros((seq_len, QK_ROPE // 2), F32)
    cosk = jnp.concatenate([cos, cos], axis=-1)
    sink = jnp.concatenate([sin, sin], axis=-1)
    ones = jnp.ones((seq_len, QK_NOPE), F32)
    zeros = jnp.zeros((seq_len, QK_NOPE), F32)
    tail = jnp.zeros((seq_len, HEAD_PAD - QK_NOPE - QK_ROPE), F32)
    cosq = jnp.tile(jnp.concatenate([ones, cosk, tail], axis=-1), (1, N_HEADS)) * scale
    sinq = jnp.tile(jnp.concatenate([zeros, sink, tail], axis=-1), (1, N_HEADS)) * scale
    return cosq, sinq, cosk, sink


def kernel(x, c, ctx, c_ctx, ada_w, ada_b, norm_g, w_in, b_in, sc_conv_w, sc_conv_b, sc_out, hy_conv_w,
           hy_conv_b, hy_w1, hy_b1, hy_w2, hy_b2, hy_w3, hy_freq, hy_skip, hy_out, mla_q_norm, mla_w_uq,
           mla_kv_norm, mla_w_ukv, mla_out, w_o, final_g):
    bsz, seq, d = x.shape
    lc = ctx.shape[1]
    depth = ada_w.shape[0]
    nh = seq // N2
    assert d == D_MODEL and bsz % 2 == 0 and nh % SUBLANE == 0 and bsz <= 7

    cc = jnp.zeros((8, d), F32).at[:bsz].set(c).at[bsz].set(c_ctx)
    mods = _ada_mods(cc, ada_w, ada_b)

    tabs = _dft_tables(nh)
    k_un, ssum = _filter_gen(seq, depth, True, hy_w1, hy_b1, hy_w2, hy_b2, hy_w3, hy_freq)
    kf = _filter_spectrum(tabs, nh, k_un, ssum)
    if depth > 1:
        kc_un, sc_sum = _filter_gen(lc, depth - 1, False, hy_w1, hy_b1, hy_w2, hy_b2, hy_w3, hy_freq)
        ctabs = _ctx_tables(lc)

    rope_l = _rope_tables(seq, True)
    rope_c = _rope_tables(lc, False)

    x_lat = x.reshape(bsz * seq, d)
    x_ctx = ctx.reshape(bsz * lc, d)
    for i in range(depth):
        last = i == depth - 1
        lw = _layer_weights(i, w_in, b_in, mla_q_norm, mla_w_uq, mla_kv_norm, mla_w_ukv, sc_out,
                            hy_out, mla_out, w_o)
        m = mods[i]
        split = lambda r: tuple(r[:, None, j * d:(j + 1) * d] for j in range(3))
        shift_l, scale_l, gate_l = split(m[:bsz])
        shift_c, scale_c, gate_c = split(m[bsz:bsz + 1])
        g = norm_g[i]
        scw, scb = sc_conv_w[i], sc_conv_b[i][None, :]
        hcw, hcb = hy_conv_w[i], hy_conv_b[i][None, :]

        hq = N_HEADS * HEAD_PAD
        q_l, k_all, v_all = _kp3_call(x_lat, bsz, seq, seq + lc, 0, (scale_l, shift_l), g, lw, rope_l)
        q_c, k_all, v_all = _kp3_call(x_ctx, bsz, lc, seq + lc, seq, (scale_c, shift_c), g, lw, rope_c,
                                      kv_bufs=(k_all, v_all))
        att_l = _attention(q_l.reshape(bsz, seq, hq), k_all, v_all).reshape(bsz * seq, W_MLA)

        v_h, x1_h, m2_h = _kp_conv_call(_kp2_kernel, "kp2_hyena", x_lat, seq, (scale_l, shift_l), g,
                                        lw['w2'], lw['b2'], hcw, hcb, (W_HYENA,) * 3)
        r3 = lambda t: t.reshape(bsz, seq, W_HYENA)
        gh_l = _hyena_long(tabs, nh, kf, i, r3(v_h), r3(x1_h), r3(m2_h), hy_skip[i])
        new_lat = _merge_call(x_lat, seq, (scale_l, shift_l, gate_l), g, lw, scw, scb,
                              gh_l.reshape(bsz * seq, W_HYENA), att_l, final_g, last)
        if not last:
            att_c = _attention(q_c.reshape(bsz, lc, hq), k_all, v_all, kv_row0=seq,
                               kv_len=lc).reshape(bsz * lc, W_MLA)
            v_hc, x1_hc, m2_hc = _kp_conv_call(_kp2_kernel, "kp2_hyena_ctx", x_ctx, lc,
                                               (scale_c, shift_c), g, lw['w2'], lw['b2'], hcw, hcb,
                                               (W_HYENA,) * 3)
            rc = lambda t: t.reshape(bsz, lc, W_HYENA)
            gh_c = _ctx_hyena(ctabs, kc_un, sc_sum, i, rc(v_hc), rc(x1_hc), rc(m2_hc), hy_skip[i])
            x_ctx = _merge_call(x_ctx, lc, (scale_c, shift_c, gate_c), g, lw, scw, scb,
                                gh_c.reshape(bsz * lc, W_HYENA), att_c, final_g, False)
        x_lat = new_lat
    return x_lat.reshape(bsz, seq, d)
```

```python
import functools
import math

import jax
import jax.numpy as jnp
from jax import lax
from jax.experimental import pallas as pl
from jax.experimental.pallas import tpu as pltpu

F32 = jnp.float32
BF16 = jnp.bfloat16

D_MODEL = 1024
DEPTH = 4
GRID_W = 64
W_CONV = 512
W_HYENA = 512
HYENA_ORDER = 2
HYENA_EMB = 33
HYENA_BANDS = (HYENA_EMB - 1) // 2
HYENA_HID = 64
HYENA_FAST_DECAY = 0.3
HYENA_SLOW_DECAY = 1.5
HYENA_TARGET = 1e-2
N_HEADS = 8
QK_NOPE = 64
QK_ROPE = 32
V_HEAD = 64
Q_LORA = 384
KV_LORA = 256
W_MLA = N_HEADS * V_HEAD
ROPE_BASE = 10000.0
N_BRANCH = 3
EPS = 1e-6

O_XIN, O_GB, O_GC, O_ZA = 0, 512, 1024, 1536
O_HPROJ, O_ZH = 2048, 3584
O_CQ, O_CKV, O_KR, O_ZM, O_GATES = 4096, 4480, 4736, 4768, 5280
N_IN = 8352

LANE = 128
SUBLANE = 8
HEAD_PAD = 128
N2 = 128
VMEM_LIMIT = 56 * 1024 * 1024


def _cparams(sem):
    return pltpu.CompilerParams(dimension_semantics=sem, vmem_limit_bytes=VMEM_LIMIT)


def _ada_kernel(c_ref, w_ref, b_ref, o_ref):
    c = c_ref[...]
    s = c * jax.nn.sigmoid(c)
    o_ref[...] = jnp.dot(s, w_ref[...], preferred_element_type=F32,
                         precision=lax.Precision.HIGHEST) + b_ref[...]


def _ada_mods(cc, ada_w, ada_b):
    depth = ada_w.shape[0]
    d = cc.shape[1]
    return pl.pallas_call(
        _ada_kernel,
        grid=(depth, 3),
        in_specs=[pl.BlockSpec((8, d), lambda l, j: (0, 0)),
                  pl.BlockSpec((None, d, d), lambda l, j: (l, 0, j)),
                  pl.BlockSpec((None, 1, d), lambda l, j: (l, 0, j))],
        out_specs=pl.BlockSpec((None, 8, d), lambda l, j: (l, 0, j)),
        out_shape=jax.ShapeDtypeStruct((depth, 8, 3 * d), F32),
        compiler_params=_cparams(("parallel", "parallel")),
        name="ada_mods",
    )(cc, ada_w, ada_b.reshape(depth, 1, 3 * d))


def _mod_norm(x, g, scale, shift):
    y = x * lax.rsqrt(jnp.mean(x * x, axis=-1, keepdims=True) + EPS)
    return (y * g) * (1.0 + scale) + shift


def _rms(x, g):
    return x * lax.rsqrt(jnp.mean(x * x, axis=-1, keepdims=True) + EPS) * g


def _silu(x):
    return x * jax.nn.sigmoid(x)


def _dwconv3_ext(u_ext, w, b, first, last, tm):
    n = tm + 2 * SUBLANE
    prev = pltpu.roll(u_ext, 1, 0)[SUBLANE:SUBLANE + tm]
    nxt = pltpu.roll(u_ext, n - 1, 0)[SUBLANE:SUBLANE + tm]
    row = lax.broadcasted_iota(jnp.int32, (tm, 1), 0)
    prev = jnp.where(jnp.logical_and(first, row == 0), 0.0, prev)
    nxt = jnp.where(jnp.logical_and(last, row == tm - 1), 0.0, nxt)
    return prev * w[0:1] + u_ext[SUBLANE:SUBLANE + tm] * w[1:2] + nxt * w[2:3] + b


def _proj_ext(xm_ref, xp_ref, xn_ref, g_ref, sc_ref, sh_ref, w_ref, b_ref):
    x_ext = jnp.concatenate([xp_ref[...], xm_ref[...], xn_ref[...]], axis=0)
    h = _mod_norm(x_ext, g_ref[...], sc_ref[...], sh_ref[...])
    return jnp.dot(h.astype(BF16), w_ref[...], preferred_element_type=F32) + b_ref[...]


def _kp2_kernel(xm_ref, xp_ref, xn_ref, g_ref, sc_ref, sh_ref, w_ref, b_ref, cw_ref, cb_ref,
                v_ref, x1_ref, m2_ref, *, tm, tiles_per_seq):
    i = pl.program_id(0)
    first = (i % tiles_per_seq) == 0
    last = (i % tiles_per_seq) == tiles_per_seq - 1
    p = _proj_ext(xm_ref, xp_ref, xn_ref, g_ref, sc_ref, sh_ref, w_ref, b_ref)
    u = _dwconv3_ext(p[:, 0:3 * W_HYENA], cw_ref[...], cb_ref[...], first, last, tm)
    zh = p[SUBLANE:SUBLANE + tm, 3 * W_HYENA:4 * W_HYENA]
    v_ref[...] = u[:, 0:W_HYENA].astype(v_ref.dtype)
    x1_ref[...] = u[:, W_HYENA:2 * W_HYENA].astype(x1_ref.dtype)
    m2_ref[...] = (_silu(zh) * u[:, 2 * W_HYENA:3 * W_HYENA]).astype(m2_ref.dtype)


def _kp3_kernel(xm_ref, g_ref, sc_ref, sh_ref, w_ref, b_ref, qg_ref, kvg_ref, wq_ref, wqs_ref,
                wk_ref, wv_ref, vb_ref, e2_ref, cosq_ref, sinq_ref, cosk_ref, sink_ref,
                q_ref, k_ref, v_ref):
    h = _mod_norm(xm_ref[...], g_ref[...], sc_ref[...], sh_ref[...])
    p = jnp.dot(h.astype(BF16), w_ref[...], preferred_element_type=F32) + b_ref[...]
    cq = _rms(p[:, 0:Q_LORA], qg_ref[...]).astype(BF16)
    ckv = _rms(p[:, Q_LORA:Q_LORA + KV_LORA], kvg_ref[...]).astype(BF16)
    kr = p[:, 640:640 + QK_ROPE]
    krs = p[:, 768:768 + QK_ROPE]
    qa = jnp.dot(cq, wq_ref[...], preferred_element_type=F32)
    qb = jnp.dot(cq, wqs_ref[...], preferred_element_type=F32)
    q_ref[...] = (qa * cosq_ref[...] + qb * sinq_ref[...]).astype(q_ref.dtype)
    kr_rot = (kr * cosk_ref[...] + krs * sink_ref[...]).astype(BF16)
    kn = jnp.dot(ckv, wk_ref[...], preferred_element_type=F32)
    krp = jnp.dot(kr_rot, e2_ref[...], preferred_element_type=F32)
    k_ref[...] = (kn + krp).astype(k_ref.dtype)
    v_ref[...] = (jnp.dot(ckv, wv_ref[...], preferred_element_type=F32) + vb_ref[...]).astype(v_ref.dtype)


def _mod_spec(mod, tiles_per_seq):
    d = mod.shape[-1]
    if mod.shape[0] == 1:
        return pl.BlockSpec((None, 1, d), lambda i: (0, 0, 0))
    return pl.BlockSpec((None, 1, d), lambda i: (i // tiles_per_seq, 0, 0))


def _full_spec(a):
    nd = a.ndim
    return pl.BlockSpec(a.shape, lambda i: (0,) * nd)


def _halo_specs(n_rows, tm, d):
    r = tm // SUBLANE
    nb = n_rows // SUBLANE
    return [pl.BlockSpec((tm, d), lambda i: (i, 0)),
            pl.BlockSpec((SUBLANE, d), lambda i: (jnp.maximum(i * r - 1, 0), 0)),
            pl.BlockSpec((SUBLANE, d), lambda i: (jnp.minimum((i + 1) * r, nb - 1), 0))]


def _kp_conv_call(kern, name, x2d, seq_len, mods, g, w, b, cw, cb, out_widths):
    n_rows, d = x2d.shape
    tm = min(512, seq_len)
    tiles_per_seq = seq_len // tm
    scale, shift = mods
    g2 = g.reshape(1, 1, d)
    in_specs = _halo_specs(n_rows, tm, d) + [
        _mod_spec(g2, tiles_per_seq), _mod_spec(scale, tiles_per_seq), _mod_spec(shift, tiles_per_seq),
        _full_spec(w), _full_spec(b), _full_spec(cw), _full_spec(cb)]
    return pl.pallas_call(
        functools.partial(kern, tm=tm, tiles_per_seq=tiles_per_seq),
        grid=(n_rows // tm,),
        in_specs=in_specs,
        out_specs=[pl.BlockSpec((tm, wd), lambda i: (i, 0)) for wd in out_widths],
        out_shape=[jax.ShapeDtypeStruct((n_rows, wd), BF16) for wd in out_widths],
        compiler_params=_cparams(("parallel",)),
        name=name,
    )(x2d, x2d, x2d, g2, scale, shift, w, b, cw, cb)


def _kp3_aliased_kernel(*refs):
    _kp3_kernel(*refs[:-5], *refs[-3:])


def _kp3_call(x2d, bsz, seq_len, kv_len, kv_row0, mods, g, lw, tabs, kv_bufs=None):
    n_rows, d = x2d.shape
    tm = min(512, seq_len)
    tiles_per_seq = seq_len // tm
    blk0 = kv_row0 // tm
    assert kv_row0 % tm == 0
    scale, shift = mods
    g2 = g.reshape(1, 1, d)
    cosq, sinq, cosk, sink = tabs
    consts = [lw['w3'], lw['b3'], lw['qg'], lw['kvg'], lw['wq'], lw['wqs'], lw['wk'], lw['wv'], lw['vb'],
              lw['e2']]

    def tab_spec(t):
        return pl.BlockSpec((tm, t.shape[1]), lambda i: (i % tiles_per_seq, 0))

    in_specs = [pl.BlockSpec((tm, d), lambda i: (i, 0)),
                _mod_spec(g2, tiles_per_seq), _mod_spec(scale, tiles_per_seq), _mod_spec(shift, tiles_per_seq)]
    in_specs += [_full_spec(a) for a in consts]
    in_specs += [tab_spec(t) for t in (cosq, sinq, cosk, sink)]
    args = [x2d, g2, scale, shift, *consts, cosq, sinq, cosk, sink]
    hq = N_HEADS * HEAD_PAD
    kv_spec = pl.BlockSpec((None, tm, hq), lambda i: (i // tiles_per_seq, blk0 + i % tiles_per_seq, 0))
    kv_shape = jax.ShapeDtypeStruct((bsz, kv_len, hq), BF16)
    aliases = {}
    kern = _kp3_kernel
    if kv_bufs is not None:
        kern = _kp3_aliased_kernel
        in_specs += [pl.BlockSpec(memory_space=pl.ANY)] * 2
        aliases = {len(args): 1, len(args) + 1: 2}
        args += list(kv_bufs)
    return pl.pallas_call(
        kern,
        grid=(n_rows // tm,),
        in_specs=in_specs,
        out_specs=[pl.BlockSpec((tm, hq), lambda i: (i, 0)), kv_spec, kv_spec],
        out_shape=[jax.ShapeDtypeStruct((n_rows, hq), BF16), kv_shape, kv_shape],
        input_output_aliases=aliases,
        compiler_params=_cparams(("parallel",)),
        name="kp3_mla" if kv_bufs is None else "kp3_mla_ctx",
    )(*args)


ATTN_CHAINS = 4


def _attn_kernel(q_ref, k_ref, v_ref, o_ref, s_ref, p_ref, m_ref, acc_ref):
    h = pl.program_id(2)
    ki = pl.program_id(3)

    @pl.when(ki == 0)
    def _():
        m_ref[...] = jnp.full(m_ref.shape, -jnp.inf, F32)
        acc_ref[...] = jnp.zeros(acc_ref.shape, F32)

    hq = q_ref.shape[0] // ATTN_CHAINS
    chains = [pl.ds(c * hq, hq) for c in range(ATTN_CHAINS)]
    for rows in chains:
        s_ref[rows, :] = lax.dot_general(q_ref[rows, :], k_ref[...], (((1,), (1,)), ((), ())),
                                         preferred_element_type=F32)
    alphas = []
    reps = s_ref.shape[1] // LANE
    for rows in chains:
        m_prev = m_ref[rows, :]
        m_new = jnp.maximum(m_prev, jnp.max(s_ref[rows, :], axis=-1, keepdims=True))
        alphas.append(jnp.exp2(m_prev - m_new))
        m_ref[rows, :] = m_new
        p_ref[rows, :] = jnp.exp2((s_ref[rows, :] - jnp.tile(m_new, (1, reps))).astype(BF16))
    for rows, alpha in zip(chains, alphas):
        acc_ref[rows, :] = acc_ref[rows, :] * alpha + jnp.dot(p_ref[rows, :], v_ref[...],
                                                              preferred_element_type=F32)

    @pl.when(ki == pl.num_programs(3) - 1)
    def _():
        acc = acc_ref[...]
        o = (acc[:, 0:V_HEAD] / acc[:, V_HEAD:V_HEAD + 1]).astype(o_ref.dtype)

        @pl.when(h % 2 == 0)
        def _():
            o_ref[:, 0:V_HEAD] = o

        @pl.when(h % 2 == 1)
        def _():
            o_ref[:, V_HEAD:2 * V_HEAD] = o


MXU_DIM = 256
Q_TILE = 2048
KV_TILE_CAP = 2816


def _kv_tile(lk):
    for step in (MXU_DIM, LANE):
        cands = [t for t in range(step, min(lk, KV_TILE_CAP) + 1, step) if lk % t == 0]
        if cands:
            return cands[-1]
    return lk


def _attention(q, k, v, kv_row0=0, kv_len=None):
    b, lq, _ = q.shape
    lk = k.shape[1] if kv_len is None else kv_len
    tq = min(Q_TILE, lq)
    tk = _kv_tile(lk)
    assert kv_row0 % tk == 0
    kb0 = kv_row0 // tk
    grid = (b, lq // tq, N_HEADS, lk // tk)
    return pl.pallas_call(
        _attn_kernel,
        grid=grid,
        in_specs=[pl.BlockSpec((None, tq, HEAD_PAD), lambda bi, qi, h, ki: (bi, qi, h)),
                  pl.BlockSpec((None, tk, HEAD_PAD), lambda bi, qi, h, ki: (bi, kb0 + ki, h)),
                  pl.BlockSpec((None, tk, HEAD_PAD), lambda bi, qi, h, ki: (bi, kb0 + ki, h))],
        out_specs=pl.BlockSpec((None, tq, 2 * V_HEAD), lambda bi, qi, h, ki: (bi, qi, h // 2)),
        out_shape=jax.ShapeDtypeStruct((b, lq, W_MLA), BF16),
        scratch_shapes=[pltpu.VMEM((tq, tk), F32), pltpu.VMEM((tq, tk), BF16),
                        pltpu.VMEM((tq, LANE), F32), pltpu.VMEM((tq, HEAD_PAD), F32)],
        compiler_params=_cparams(("parallel", "parallel", "arbitrary", "arbitrary")),
        name="mla_attention",
    )(q, k, v)


COL_BWD = HYENA_EMB
COL_DROP = HYENA_EMB + 1


def _dot_bf16x3(a, b):
    a_hi = a.astype(BF16)
    b_hi = b.astype(BF16)
    a_lo = (a - a_hi.astype(F32)).astype(BF16)
    b_lo = (b - b_hi.astype(F32)).astype(BF16)
    dot = functools.partial(jnp.dot, preferred_element_type=F32)
    return dot(a_hi, b_hi) + (dot(a_lo, b_hi) + dot(a_hi, b_lo))


def _filt_kernel(z_ref, w1a_ref, w1b_ref, b1_ref, w2_ref, b2_ref, fr_ref, w3a_ref, w3b_ref, dl_ref,
                 k_ref, s_ref):
    i = pl.program_id(1)
    hp = lax.Precision.HIGHEST
    wc = HYENA_ORDER * W_HYENA
    z = z_ref[...]
    half = z.shape[0] // 2
    fr = fr_ref[...]
    pre = _dot_bf16x3(z[:half], w1a_ref[...]) + _dot_bf16x3(z[half:], w1b_ref[...])
    hid = jnp.sin(fr * (pre + b1_ref[...]))
    hid = jnp.sin(fr * (jnp.dot(hid, w2_ref[...], preferred_element_type=F32, precision=hp) + b2_ref[...]))
    hb = hid.astype(BF16)
    h2 = jnp.concatenate([jnp.dot(hb, w3a_ref[...], preferred_element_type=F32),
                          jnp.dot(hb, w3b_ref[...], preferred_element_type=F32)], axis=0)
    h = jnp.where(z[:, COL_BWD:COL_BWD + 1] > 0.5, h2[:, wc:], h2[:, :wc])
    t = z[:, 0:1]
    h = h * jnp.exp(-t * dl_ref[...])

    @pl.when(i == 0)
    def _():
        s_ref[...] = jnp.zeros(s_ref.shape, F32)

    s_ref[...] += jnp.sum(jnp.abs(h), axis=0, keepdims=True)
    k_ref[...] = jnp.where(z[:, COL_DROP:COL_DROP + 1] > 0.5, 0.0, h)


def _filter_tables(seq_len, permuted):
    t = jnp.linspace(0.0, 1.0, seq_len, dtype=F32)[:, None]
    w = 2.0 * math.pi * jnp.arange(seq_len, dtype=F32)[:, None] / seq_len
    f = jnp.linspace(1e-4, HYENA_BANDS - 1, HYENA_BANDS, dtype=F32)[None, :]
    z = jnp.concatenate([t, jnp.cos(f * w), -jnp.sin(f * w)], axis=-1)
    z2 = jnp.concatenate([z, z[0:1], jnp.flip(z[1:], axis=0)], axis=0)
    n = jnp.arange(2 * seq_len)
    flags = jnp.stack([n >= seq_len, n == seq_len], axis=-1).astype(F32)
    zz = jnp.concatenate([z2, flags], axis=-1)
    zz = jnp.pad(zz, ((0, 0), (0, LANE - zz.shape[1])))
    if permuted:
        zz = zz.reshape(2 * seq_len // N2, N2, LANE).transpose(1, 0, 2).reshape(2 * seq_len, LANE)
    return zz


def _hyena_deltas():
    max_decay = math.log(HYENA_TARGET) / HYENA_FAST_DECAY
    min_decay = math.log(HYENA_TARGET) / HYENA_SLOW_DECAY
    deltas = jnp.abs(jnp.linspace(min_decay, max_decay, W_HYENA, dtype=F32))
    return jnp.tile(deltas, HYENA_ORDER)[None, :]


def _filter_gen(seq_len, n_layers, permuted, hy_w1, hy_b1, hy_w2, hy_b2, hy_w3, hy_freq):
    zz = _filter_tables(seq_len, permuted)
    tmf = min(512, seq_len)
    tiles_half = seq_len // tmf
    wc = HYENA_ORDER * W_HYENA
    hh = HYENA_HID
    zpad = lambda a, lo, hi, ax: jnp.pad(a, [(lo, hi) if d == ax else (0, 0) for d in range(a.ndim)])
    w1p = jnp.pad(hy_w1[:n_layers], ((0, 0), (0, LANE - HYENA_EMB), (0, 0)))
    w1a, w1b = zpad(w1p, 0, hh, 2), zpad(w1p, hh, 0, 2)
    w2 = hy_w2[:n_layers]
    w2bd = jnp.concatenate([zpad(w2, 0, hh, 2), zpad(w2, hh, 0, 2)], axis=1)
    w3 = hy_w3[:n_layers].astype(BF16)
    w3a, w3b = zpad(w3, 0, hh, 1), zpad(w3, hh, 0, 1)
    r1 = lambda a: jnp.tile(a[:n_layers].reshape(n_layers, 1, hh), (1, 1, 2))
    per_layer = lambda r, c: pl.BlockSpec((None, r, c), lambda l, i: (l, 0, 0))
    return pl.pallas_call(
        _filt_kernel,
        grid=(n_layers, 2 * tiles_half),
        in_specs=[pl.BlockSpec((tmf, LANE), lambda l, i: (i, 0)),
                  per_layer(LANE, LANE), per_layer(LANE, LANE), per_layer(1, LANE),
                  per_layer(LANE, LANE), per_layer(1, LANE), per_layer(1, LANE),
                  per_layer(LANE, 2 * wc), per_layer(LANE, 2 * wc),
                  pl.BlockSpec((1, wc), lambda l, i: (0, 0))],
        out_specs=[pl.BlockSpec((tmf, wc), lambda l, i: (i, l)),
                   pl.BlockSpec((None, 1, wc), lambda l, i: (l, 0, 0))],
        out_shape=[jax.ShapeDtypeStruct((2 * seq_len, n_layers * wc), F32),
                   jax.ShapeDtypeStruct((n_layers, 1, wc), F32)],
        compiler_params=_cparams(("parallel", "arbitrary")),
        name="hyena_filter",
    )(zz, w1a, w1b, r1(hy_b1), w2bd, r1(hy_b2), r1(hy_freq), w3a, w3b, _hyena_deltas())


def _angles(a, b, n):
    m = (a * b) % n
    return m.astype(F32) * (2.0 * math.pi / n)


def _dft_tables(nh):
    n1 = 2 * nh
    n = n1 * N2
    k1 = jnp.arange(n1, dtype=jnp.int32)
    th = _angles(k1[:, None], jnp.arange(nh, dtype=jnp.int32)[None, :], n1)
    c, s = jnp.cos(th), jnp.sin(th)
    w1_pair = jnp.concatenate([jnp.concatenate([c, s], 1), jnp.concatenate([-s, c], 1)], 0)
    thf = _angles(k1[:, None], k1[None, :], n1)
    w1_real = jnp.concatenate([jnp.cos(thf), -jnp.sin(thf)], 0)
    g1 = jnp.concatenate([jnp.concatenate([c.T, -s.T], 1), jnp.concatenate([s.T, c.T], 1)], 0) / n
    n2 = jnp.arange(N2, dtype=jnp.int32)
    al = _angles(k1[:, None], n2[None, :], n)
    be = _angles(n2[:, None], n2[None, :], N2)
    ca, sa = jnp.cos(al)[:, None, :], jnp.sin(al)[:, None, :]
    cb, sb = jnp.cos(be)[None], jnp.sin(be)[None]
    cp, sp = ca * cb - sa * sb, sa * cb + ca * sb
    f2 = jnp.concatenate([jnp.concatenate([cp, sp], 2), jnp.concatenate([-sp, cp], 2)], 1)
    cpt, spt = cp.transpose(0, 2, 1), sp.transpose(0, 2, 1)
    g2 = jnp.concatenate([jnp.concatenate([cpt, -spt], 2), jnp.concatenate([spt, cpt], 2)], 1)
    g1 = g1.reshape(2 * nh, 2, n1).transpose(0, 2, 1).reshape(2 * nh, 2 * n1)
    f2 = f2.reshape(n1, 2 * N2, 2, N2).transpose(0, 1, 3, 2).reshape(n1, 2 * N2, 2 * N2)
    g2 = g2.reshape(n1, 2, N2, 2 * N2).transpose(0, 2, 1, 3).reshape(n1, 2 * N2, 2 * N2)
    w1_pair = w1_pair.reshape(2, n1, 2 * nh).transpose(1, 0, 2).reshape(2 * n1, 2 * nh)
    w1_real = w1_real.reshape(2, n1, n1).transpose(1, 0, 2).reshape(2 * n1, n1)
    return dict(w1_pair=w1_pair.astype(BF16), w1_real=w1_real.astype(BF16), g1=g1.astype(BF16),
                f2=f2.astype(BF16), g2=g2.astype(BF16))


def _strided_rows(ref, start, size):
    parts = [ref[cc, pl.ds(start, size, stride=SUBLANE), :] for cc in range(ref.shape[0])]
    return parts[0] if len(parts) == 1 else jnp.concatenate(parts, axis=1)


def _pack_pairs(x):
    return pltpu.bitcast(x.astype(BF16), jnp.uint32)


def _unpack_pairs(packed):
    return pltpu.bitcast(packed, BF16)


def _store_chunks(ref, row0, val):
    for cc in range(ref.shape[0]):
        ref[cc, pl.ds(row0, SUBLANE), :] = val[:, cc * LANE:(cc + 1) * LANE]


def _stage_a_kernel(*refs, n2t, n1, has_inv, has_fwd, has_div):
    it = iter(refs)
    y_ref = next(it) if has_inv else None
    g1_ref = next(it) if has_inv else None
    u_ref = next(it)
    mul_ref = next(it) if has_inv else None
    skip_ref = next(it) if has_inv else None
    div_ref = next(it) if has_div else None
    w1_ref = next(it) if has_fwd else None
    e_ref = next(it) if has_inv else None
    a_ref = next(it) if has_fwd else None
    yy_ref = next(it) if has_inv else None
    if has_inv:
        for j in range(n2t):
            nb, jl = j // SUBLANE, j % SUBLANE
            rhs = _unpack_pairs(_strided_rows(y_ref.at[nb], jl, n1))
            yy_ref[j] = jnp.dot(g1_ref[...], rhs, preferred_element_type=F32)
        for j in range(n2t):
            e = mul_ref[j].astype(F32) * (yy_ref[j] + u_ref[j].astype(F32) * skip_ref[...])
            e_ref[j] = e.astype(e_ref.dtype)
    if has_fwd:
        for j in range(n2t):
            if has_inv:
                src = e_ref[j]
            elif has_div:
                src = (u_ref[j] / div_ref[...]).astype(BF16)
            else:
                src = u_ref[j].astype(BF16)
            a = jnp.dot(w1_ref[...], src, preferred_element_type=F32)
            packed = _pack_pairs(a)
            for kt in range(n1 // SUBLANE):
                _store_chunks(a_ref.at[kt], j * SUBLANE, packed[kt * SUBLANE:(kt + 1) * SUBLANE])


def _stage_a(tabs, nh, u, *, y=None, mul=None, skip=None, div=None, w1=None, want_fwd=True,
             ct=256, n2t=32, e_dtype=BF16, c_off=0):
    p, _, rows, c = u.shape
    n1 = 2 * nh
    has_inv = y is not None
    has_div = div is not None
    ct = min(ct, c)
    nbk = n2t // SUBLANE
    grid = (p, c // ct, N2 // n2t)
    args, specs = [], []
    if has_inv:
        args += [y, tabs['g1']]
        specs += [pl.BlockSpec((None, nbk, ct // LANE, n1 * SUBLANE, LANE),
                               lambda pi, ci, ni: (pi, ni, ci, 0, 0)),
                  pl.BlockSpec(tabs['g1'].shape, lambda pi, ci, ni: (0, 0))]
    args.append(u)
    specs.append(pl.BlockSpec((None, n2t, rows, ct), lambda pi, ci, ni: (pi, ni, 0, ci)))
    if has_inv:
        args += [mul, skip]
        specs += [pl.BlockSpec((None, n2t, rows, ct), lambda pi, ci, ni: (pi, ni, 0, ci)),
                  pl.BlockSpec((1, ct), lambda pi, ci, ni: (0, ci + c_off // ct))]
    if has_div:
        args.append(div)
        specs.append(pl.BlockSpec((1, ct), lambda pi, ci, ni: (0, ci)))
    if want_fwd:
        args.append(w1)
        specs.append(pl.BlockSpec(w1.shape, lambda pi, ci, ni: (0, 0)))
    out_shape, out_specs = [], []
    if has_inv:
        out_shape.append(jax.ShapeDtypeStruct(u.shape, e_dtype))
        out_specs.append(pl.BlockSpec((None, n2t, rows, ct), lambda pi, ci, ni: (pi, ni, 0, ci)))
    if want_fwd:
        out_shape.append(jax.ShapeDtypeStruct((p, n1 // SUBLANE, c // LANE, N2 * SUBLANE, LANE), jnp.uint32))
        out_specs.append(pl.BlockSpec((None, n1 // SUBLANE, ct // LANE, n2t * SUBLANE, LANE),
                                      lambda pi, ci, ni: (pi, 0, ci, ni, 0)))
    outs = pl.pallas_call(
        functools.partial(_stage_a_kernel, n2t=n2t, n1=n1, has_inv=has_inv, has_fwd=want_fwd,
                          has_div=has_div),
        grid=grid, in_specs=specs, out_specs=out_specs, out_shape=out_shape,
        scratch_shapes=[pltpu.VMEM((n2t, rows, ct), F32)] if has_inv else [],
        compiler_params=_cparams(("parallel", "parallel", "parallel")),
        name="hyena_stage_a" + ("_inv" if has_inv else "") + ("_fwd" if want_fwd else ""),
    )(*args)
    return outs


def _stage_b_kernel(*refs, filt_only):
    if filt_only:
        a_ref, f2_ref, o_ref = refs
        for kl in range(SUBLANE):
            rhs = _unpack_pairs(_strided_rows(a_ref, kl, N2))
            o_ref[kl] = jnp.dot(f2_ref[kl], rhs, preferred_element_type=F32).astype(o_ref.dtype)
        return
    a_ref, f2_ref, g2_ref, kf_ref, o_ref, t_ref, z_ref = refs
    for kl in range(SUBLANE):
        rhs = _unpack_pairs(_strided_rows(a_ref, kl, N2))
        t_ref[kl] = jnp.dot(f2_ref[kl], rhs, preferred_element_type=F32)
    for kl in range(SUBLANE):
        kf = kf_ref[kl].astype(F32)
        tre, tim = t_ref[kl, :N2], t_ref[kl, N2:]
        kre, kim = kf[:N2], kf[N2:]
        z_ref[kl, :N2] = (tre * kre - tim * kim).astype(BF16)
        z_ref[kl, N2:] = (tre * kim + tim * kre).astype(BF16)
    for kl in range(SUBLANE):
        y = jnp.dot(g2_ref[kl], z_ref[kl], preferred_element_type=F32)
        packed = _pack_pairs(y)
        for nt in range(N2 // SUBLANE):
            _store_chunks(o_ref.at[nt], kl * SUBLANE, packed[nt * SUBLANE:(nt + 1) * SUBLANE])


def _stage_b(tabs, a, kf=None, *, c_off=0, ct=512):
    p, nkt, ncc, _, _ = a.shape
    c = ncc * LANE
    n1 = nkt * SUBLANE
    ct = min(ct, c)
    filt_only = kf is None
    grid = (nkt, c // ct, p)
    a_spec = pl.BlockSpec((None, None, ct // LANE, N2 * SUBLANE, LANE),
                          lambda kt, ci, pi: (pi, kt, ci, 0, 0))
    tab_spec = pl.BlockSpec((SUBLANE, 2 * N2, 2 * N2), lambda kt, ci, pi: (kt, 0, 0))
    if filt_only:
        return pl.pallas_call(
            functools.partial(_stage_b_kernel, filt_only=True),
            grid=grid, in_specs=[a_spec, tab_spec],
            out_specs=pl.BlockSpec((SUBLANE, 2 * N2, ct), lambda kt, ci, pi: (kt, 0, ci)),
            out_shape=jax.ShapeDtypeStruct((n1, 2 * N2, c), BF16),
            compiler_params=_cparams(("parallel", "parallel", "parallel")),
            name="hyena_stage_b_filter",
        )(a, tabs['f2'])
    return pl.pallas_call(
        functools.partial(_stage_b_kernel, filt_only=False),
        grid=grid,
        in_specs=[a_spec, tab_spec, tab_spec,
                  pl.BlockSpec((SUBLANE, 2 * N2, ct), lambda kt, ci, pi: (kt, 0, ci + c_off // ct))],
        out_specs=pl.BlockSpec((None, N2 // SUBLANE, ct // LANE, SUBLANE * SUBLANE, LANE),
                               lambda kt, ci, pi: (pi, 0, ci, kt, 0)),
        out_shape=jax.ShapeDtypeStruct((p, N2 // SUBLANE, c // LANE, n1 * SUBLANE, LANE), jnp.uint32),
        scratch_shapes=[pltpu.VMEM((SUBLANE, 2 * N2, ct), F32), pltpu.VMEM((SUBLANE, 2 * N2, ct), BF16)],
        compiler_params=_cparams(("parallel", "parallel", "parallel")),
        name="hyena_stage_b",
    )(a, tabs['f2'], tabs['g2'], kf)


def _permute_seq(a, nh):
    b, _, c = a.shape
    return a.reshape(b // 2, 2, nh, N2, c).transpose(0, 3, 1, 2, 4).reshape(b // 2, N2, 2 * nh, c)


def _unpermute_seq(a, nh):
    p, _, _, c = a.shape
    return a.reshape(p, N2, 2, nh, c).transpose(0, 2, 3, 1, 4).reshape(2 * p, nh * N2, c)


def _filter_spectrum(tabs, nh, k_un, ssum):
    n, call = k_un.shape
    kp = k_un.reshape(1, N2, 2 * nh, call)
    (a,) = _stage_a(tabs, nh, kp, div=ssum.reshape(1, call), w1=tabs['w1_real'], ct=512, n2t=16)
    return _stage_b(tabs, a)


def _hyena_long(tabs, nh, kf, layer, v, x1, m2, skip):
    vp, x1p, m2p = (_permute_seq(t, nh) for t in (v, x1, m2))
    c0 = layer * HYENA_ORDER * W_HYENA
    skip2 = skip.reshape(1, HYENA_ORDER * W_HYENA)
    (a1,) = _stage_a(tabs, nh, vp, w1=tabs['w1_pair'])
    y1 = _stage_b(tabs, a1, kf, c_off=c0)
    z, a2 = _stage_a(tabs, nh, vp, y=y1, mul=x1p, skip=skip2, w1=tabs['w1_pair'], c_off=0)
    y2 = _stage_b(tabs, a2, kf, c_off=c0 + W_HYENA)
    (gp,) = _stage_a(tabs, nh, z, y=y2, mul=m2p, skip=skip2, want_fwd=False, c_off=W_HYENA)
    return _unpermute_seq(gp, nh)


def _ctx_conv_kernel(v_ref, x1_ref, m2_ref, k_ref, s_ref, skip_ref, ff_ref, fk_ref, gi_ref, o_ref,
                     *, n):
    def conv(u, o):
        kfull = k_ref[:, o * W_HYENA:(o + 1) * W_HYENA] / s_ref[:, o * W_HYENA:(o + 1) * W_HYENA]
        kf = jnp.dot(fk_ref[...], kfull.astype(BF16), preferred_element_type=F32)
        uf = jnp.dot(ff_ref[...], u.astype(BF16), preferred_element_type=F32)
        ure, uim, kre, kim = uf[:n], uf[n:], kf[:n], kf[n:]
        z = jnp.concatenate([ure * kre - uim * kim, ure * kim + uim * kre], axis=0).astype(BF16)
        y = jnp.dot(gi_ref[...], z, preferred_element_type=F32)
        return y + u * skip_ref[o:o + 1, :]

    v = v_ref[...].astype(F32)
    z1 = x1_ref[...].astype(F32) * conv(v, 0)
    o_ref[...] = (m2_ref[...].astype(F32) * conv(z1, 1)).astype(o_ref.dtype)


def _ctx_tables(lc):
    n = 2 * lc
    k = jnp.arange(n, dtype=jnp.int32)
    ph = _angles(k[:, None], k[None, :], n)
    c, s = jnp.cos(ph), jnp.sin(ph)
    fk = jnp.concatenate([c, -s], axis=0)
    ff = fk[:, :lc]
    gi = jnp.concatenate([c[:lc], -s[:lc]], axis=1) / n
    return ff.astype(BF16), fk.astype(BF16), gi.astype(BF16)


def _ctx_hyena(ctabs, k_un, ssum, layer, v, x1, m2, skip):
    b, lc, c = v.shape
    n = 2 * lc
    ff, fk, gi = ctabs
    wc = HYENA_ORDER * W_HYENA
    tok = pl.BlockSpec((None, lc, c), lambda bi: (bi, 0, 0))
    return pl.pallas_call(
        functools.partial(_ctx_conv_kernel, n=n),
        grid=(b,),
        in_specs=[tok, tok, tok,
                  pl.BlockSpec((n, wc), lambda bi: (0, layer)),
                  pl.BlockSpec((None, 1, wc), lambda bi: (layer, 0, 0)),
                  pl.BlockSpec((HYENA_ORDER, c), lambda bi: (0, 0)),
                  _full_spec(ff), _full_spec(fk), _full_spec(gi)],
        out_specs=tok,
        out_shape=jax.ShapeDtypeStruct((b, lc, c), BF16),
        compiler_params=_cparams(("parallel",)),
        name="ctx_hyena",
    )(v, x1, m2, k_un, ssum, skip, ff, fk, gi)


def _merge_kernel(xm_ref, xp_ref, xn_ref, gh_ref, att_ref, g_ref, sc_ref, sh_ref, gt_ref,
                  w1_ref, b1_ref, cw_ref, cb_ref, wg_ref, bg_ref, sco_ref, hyo_ref, mlo_ref, wo_ref,
                  fg_ref, o_ref, *, final, tm, tiles_per_seq):
    i = pl.program_id(0)
    first = (i % tiles_per_seq) == 0
    last = (i % tiles_per_seq) == tiles_per_seq - 1
    x = xm_ref[...]
    d = x.shape[-1]
    x_ext = jnp.concatenate([xp_ref[...], x, xn_ref[...]], axis=0)
    hb_ext = _mod_norm(x_ext, g_ref[...], sc_ref[...], sh_ref[...]).astype(BF16)
    p = jnp.dot(hb_ext, w1_ref[...], preferred_element_type=F32) + b1_ref[...]
    prod = p[:, 2 * W_CONV:3 * W_CONV] * p[:, 0:W_CONV]
    conv = _dwconv3_ext(prod, cw_ref[...], cb_ref[...], first, last, tm)
    pm = p[SUBLANE:SUBLANE + tm]
    a = (_silu(pm[:, 3 * W_CONV:4 * W_CONV]) * (pm[:, W_CONV:2 * W_CONV] * conv)).astype(BF16)
    sm = _silu(pm[:, 4 * W_CONV:4 * W_CONV + W_MLA])
    hb = hb_ext[SUBLANE:SUBLANE + tm]
    gates = jax.nn.sigmoid(jnp.dot(hb, wg_ref[...], preferred_element_type=F32) + bg_ref[...])
    ya = jnp.dot(a, sco_ref[...], preferred_element_type=F32)
    yh = jnp.dot(gh_ref[...], hyo_ref[...], preferred_element_type=F32)
    am = (sm * att_ref[...].astype(F32)).astype(BF16)
    ym = jnp.dot(am, mlo_ref[...], preferred_element_type=F32)
    y = gates[:, 0:d] * ya + gates[:, d:2 * d] * yh + gates[:, 2 * d:3 * d] * ym
    o = jnp.dot(y.astype(BF16), wo_ref[...], preferred_element_type=F32)
    xn = x + gt_ref[...] * o
    if final:
        xn = _rms(xn, fg_ref[...])
    o_ref[...] = xn


def _merge_call(x2d, seq_len, mods, g, lw, cw, cb, gh, att, final_g, final):
    n_rows, d = x2d.shape
    tm = min(512, seq_len)
    tiles_per_seq = seq_len // tm
    scale, shift, gate = mods
    g2 = g.reshape(1, 1, d)
    fg = final_g.reshape(1, d)
    tok = lambda wd: pl.BlockSpec((tm, wd), lambda i: (i, 0))
    consts = [lw['w1'], lw['b1'], cw, cb, lw['wg'], lw['bg'], lw['sc_out'], lw['hy_out'], lw['mla_out'],
              lw['w_o'], fg]
    in_specs = _halo_specs(n_rows, tm, d) + [
        tok(W_HYENA), tok(W_MLA),
        _mod_spec(g2, tiles_per_seq), _mod_spec(scale, tiles_per_seq),
        _mod_spec(shift, tiles_per_seq), _mod_spec(gate, tiles_per_seq)]
    in_specs += [_full_spec(c) for c in consts]
    return pl.pallas_call(
        functools.partial(_merge_kernel, final=final, tm=tm, tiles_per_seq=tiles_per_seq),
        grid=(n_rows // tm,),
        in_specs=in_specs,
        out_specs=tok(d),
        out_shape=jax.ShapeDtypeStruct((n_rows, d), F32),
        compiler_params=_cparams(("parallel",)),
        name="merge_final" if final else "merge",
    )(x2d, x2d, x2d, gh, att, g2, scale, shift, gate, *consts)


def _rope_swap_cols(w):
    half = QK_ROPE // 2
    return jnp.concatenate([-w[..., half:], w[..., :half]], axis=-1)


def _layer_weights(i, w_in, b_in, mla_q_norm, mla_w_uq, mla_kv_norm, mla_w_ukv, sc_out, hy_out,
                   mla_out, w_o):
    wi, bi = w_in[i], b_in[i]
    d = wi.shape[0]
    lw = {}
    lw['w1'] = jnp.concatenate([wi[:, O_XIN:O_HPROJ], wi[:, O_ZM:O_GATES]], axis=1).astype(BF16)
    lw['b1'] = jnp.concatenate([bi[O_XIN:O_HPROJ], bi[O_ZM:O_GATES]])[None, :]
    lw['w2'] = wi[:, O_HPROJ:O_CQ].astype(BF16)
    lw['b2'] = bi[O_HPROJ:O_CQ][None, :]
    wkr, bkr = wi[:, O_KR:O_ZM], bi[O_KR:O_ZM]
    zw = lambda n: jnp.zeros((d, n), F32)
    zb = lambda n: jnp.zeros((n,), F32)
    lw['w3'] = jnp.concatenate([wi[:, O_CQ:O_KR], wkr, zw(96), _rope_swap_cols(wkr), zw(96)], axis=1).astype(BF16)
    lw['b3'] = jnp.concatenate([bi[O_CQ:O_KR], bkr, zb(96), _rope_swap_cols(bkr), zb(96)])[None, :]
    lw['qg'] = mla_q_norm[i][None, :]
    lw['kvg'] = mla_kv_norm[i][None, :]
    wuq = mla_w_uq[i].reshape(Q_LORA, N_HEADS, QK_NOPE + QK_ROPE)
    pad = jnp.zeros((Q_LORA, N_HEADS, HEAD_PAD - QK_NOPE - QK_ROPE), F32)
    lw['wq'] = jnp.concatenate([wuq, pad], axis=-1).reshape(Q_LORA, N_HEADS * HEAD_PAD).astype(BF16)
    zn = jnp.zeros((Q_LORA, N_HEADS, QK_NOPE), F32)
    lw['wqs'] = jnp.concatenate([zn, _rope_swap_cols(wuq[..., QK_NOPE:]), pad], axis=-1).reshape(
        Q_LORA, N_HEADS * HEAD_PAD).astype(BF16)
    wukv = mla_w_ukv[i].reshape(KV_LORA, N_HEADS, QK_NOPE + V_HEAD)
    padk = jnp.zeros((KV_LORA, N_HEADS, HEAD_PAD - QK_NOPE), F32)
    lw['wk'] = jnp.concatenate([wukv[..., :QK_NOPE], padk], axis=-1).reshape(
        KV_LORA, N_HEADS * HEAD_PAD).astype(BF16)
    padv = jnp.zeros((KV_LORA, N_HEADS, HEAD_PAD - V_HEAD), F32)
    lw['wv'] = jnp.concatenate([wukv[..., QK_NOPE:], padv], axis=-1).reshape(
        KV_LORA, N_HEADS * HEAD_PAD).astype(BF16)
    lw['vb'] = jnp.zeros((N_HEADS, HEAD_PAD), F32).at[:, V_HEAD].set(1.0).reshape(1, N_HEADS * HEAD_PAD)
    e2 = jnp.zeros((QK_ROPE, N_HEADS, HEAD_PAD), F32).at[:, :, QK_NOPE:QK_NOPE + QK_ROPE].set(
        jnp.eye(QK_ROPE, dtype=F32)[:, None, :])
    lw['e2'] = e2.reshape(QK_ROPE, N_HEADS * HEAD_PAD).astype(BF16)
    lw['wg'] = wi[:, O_GATES:].astype(BF16)
    lw['bg'] = bi[O_GATES:][None, :]
    lw['sc_out'] = sc_out[i].astype(BF16)
    lw['hy_out'] = hy_out[i].astype(BF16)
    lw['mla_out'] = mla_out[i].astype(BF16)
    lw['w_o'] = w_o[i].astype(BF16)
    return lw


def _rope_tables(seq_len, use_rope):
    scale = (QK_NOPE + QK_ROPE) ** -0.5 * math.log2(math.e)
    if use_rope:
        rows = seq_len // GRID_W
        row = jnp.broadcast_to(jnp.arange(rows, dtype=F32)[:, None], (rows, GRID_W)).reshape(seq_len)
        col = jnp.broadcast_to(jnp.arange(GRID_W, dtype=F32)[None, :], (rows, GRID_W)).reshape(seq_len)
        n_f = QK_ROPE // 4
        inv = ROPE_BASE ** (-jnp.arange(n_f, dtype=F32) / n_f)
        ang = jnp.concatenate([row[:, None] * inv, col[:, None] * inv], axis=-1)
        cos, sin = jnp.cos(ang), jnp.sin(ang)
    else:
        cos = jnp.ones((seq_len, QK_ROPE // 2), F32)
        sin = jnp.zeros((seq_len, QK_ROPE // 2), F32)
    cosk = jnp.concatenate([cos, cos], axis=-1)
    sink = jnp.concatenate([sin, sin], axis=-1)
    ones = jnp.ones((seq_len, QK_NOPE), F32)
    zeros = jnp.zeros((seq_len, QK_NOPE), F32)
    tail = jnp.zeros((seq_len, HEAD_PAD - QK_NOPE - QK_ROPE), F32)
    cosq = jnp.tile(jnp.concatenate([ones, cosk, tail], axis=-1), (1, N_HEADS)) * scale
    sinq = jnp.tile(jnp.concatenate([zeros, sink, tail], axis=-1), (1, N_HEADS)) * scale
    return cosq, sinq, cosk, sink


def kernel(x, c, ctx, c_ctx, ada_w, ada_b, norm_g, w_in, b_in, sc_conv_w, sc_conv_b, sc_out, hy_conv_w,
           hy_conv_b, hy_w1, hy_b1, hy_w2, hy_b2, hy_w3, hy_freq, hy_skip, hy_out, mla_q_norm, mla_w_uq,
           mla_kv_norm, mla_w_ukv, mla_out, w_o, final_g):
    bsz, seq, d = x.shape
    lc = ctx.shape[1]
    depth = ada_w.shape[0]
    nh = seq // N2
    assert d == D_MODEL and bsz % 2 == 0 and nh % SUBLANE == 0 and bsz <= 7

    cc = jnp.zeros((8, d), F32).at[:bsz].set(c).at[bsz].set(c_ctx)
    mods = _ada_mods(cc, ada_w, ada_b)

    tabs = _dft_tables(nh)
    k_un, ssum = _filter_gen(seq, depth, True, hy_w1, hy_b1, hy_w2, hy_b2, hy_w3, hy_freq)
    kf = _filter_spectrum(tabs, nh, k_un, ssum)
    if depth > 1:
        kc_un, sc_sum = _filter_gen(lc, depth - 1, False, hy_w1, hy_b1, hy_w2, hy_b2, hy_w3, hy_freq)
        ctabs = _ctx_tables(lc)

    rope_l = _rope_tables(seq, True)
    rope_c = _rope_tables(lc, False)

    x_lat = x.reshape(bsz * seq, d)
    x_ctx = ctx.reshape(bsz * lc, d)
    for i in range(depth):
        last = i == depth - 1
        lw = _layer_weights(i, w_in, b_in, mla_q_norm, mla_w_uq, mla_kv_norm, mla_w_ukv, sc_out,
                            hy_out, mla_out, w_o)
        m = mods[i]
        split = lambda r: tuple(r[:, None, j * d:(j + 1) * d] for j in range(3))
        shift_l, scale_l, gate_l = split(m[:bsz])
        shift_c, scale_c, gate_c = split(m[bsz:bsz + 1])
        g = norm_g[i]
        scw, scb = sc_conv_w[i], sc_conv_b[i][None, :]
        hcw, hcb = hy_conv_w[i], hy_conv_b[i][None, :]

        hq = N_HEADS * HEAD_PAD
        q_l, k_all, v_all = _kp3_call(x_lat, bsz, seq, seq + lc, 0, (scale_l, shift_l), g, lw, rope_l)
        q_c, k_all, v_all = _kp3_call(x_ctx, bsz, lc, seq + lc, seq, (scale_c, shift_c), g, lw, rope_c,
                                      kv_bufs=(k_all, v_all))
        att_l = _attention(q_l.reshape(bsz, seq, hq), k_all, v_all).reshape(bsz * seq, W_MLA)

        v_h, x1_h, m2_h = _kp_conv_call(_kp2_kernel, "kp2_hyena", x_lat, seq, (scale_l, shift_l), g,
                                        lw['w2'], lw['b2'], hcw, hcb, (W_HYENA,) * 3)
        r3 = lambda t: t.reshape(bsz, seq, W_HYENA)
        gh_l = _hyena_long(tabs, nh, kf, i, r3(v_h), r3(x1_h), r3(m2_h), hy_skip[i])
        new_lat = _merge_call(x_lat, seq, (scale_l, shift_l, gate_l), g, lw, scw, scb,
                              gh_l.reshape(bsz * seq, W_HYENA), att_l, final_g, last)
        if not last:
            att_c = _attention(q_c.reshape(bsz, lc, hq), k_all, v_all, kv_row0=seq,
                               kv_len=lc).reshape(bsz * lc, W_MLA)
            v_hc, x1_hc, m2_hc = _kp_conv_call(_kp2_kernel, "kp2_hyena_ctx", x_ctx, lc,
                                               (scale_c, shift_c), g, lw['w2'], lw['b2'], hcw, hcb,
                                               (W_HYENA,) * 3)
            rc = lambda t: t.reshape(bsz, lc, W_HYENA)
            gh_c = _ctx_hyena(ctabs, kc_un, sc_sum, i, rc(v_hc), rc(x1_hc), rc(m2_hc), hy_skip[i])
            x_ctx = _merge_call(x_ctx, lc, (scale_c, shift_c, gate_c), g, lw, scw, scb,
                                gh_c.reshape(bsz * lc, W_HYENA), att_c, final_g, False)
        x_lat = new_lat
    return x_lat.reshape(bsz, seq, d)
```

```python
import functools
import math

import jax
import jax.numpy as jnp
from jax import lax
from jax.experimental import pallas as pl
from jax.experimental.pallas import tpu as pltpu

F32 = jnp.float32
BF16 = jnp.bfloat16

D_MODEL = 1024
GRID_W = 64
W_CONV = 512
W_HYENA = 512
HYENA_ORDER = 2
HYENA_EMB = 33
HYENA_BANDS = (HYENA_EMB - 1) // 2
HYENA_HID = 64
HYENA_FAST_DECAY = 0.3
HYENA_SLOW_DECAY = 1.5
HYENA_TARGET = 1e-2
N_HEADS = 8
QK_NOPE = 64
QK_ROPE = 32
V_HEAD = 64
Q_LORA = 384
KV_LORA = 256
W_MLA = N_HEADS * V_HEAD
ROPE_BASE = 10000.0
N_BRANCH = 3
EPS = 1e-6

O_XIN, O_GB, O_GC, O_ZA = 0, 512, 1024, 1536
O_HPROJ, O_ZH = 2048, 3584
O_CQ, O_CKV, O_KR, O_ZM, O_GATES = 4096, 4480, 4736, 4768, 5280

LANE = 128
SUBLANE = 8
HEAD_PAD = 128
N2 = 128
VMEM_CAPACITY = 64 * 1024 * 1024
VMEM_LIMIT = VMEM_CAPACITY - 8 * 1024 * 1024


def _cparams(sem):
    return pltpu.CompilerParams(dimension_semantics=sem, vmem_limit_bytes=VMEM_LIMIT)


def _ada_kernel(c_ref, w_ref, b_ref, o_ref):
    c = c_ref[...]
    s = c * jax.nn.sigmoid(c)
    o_ref[...] = jnp.dot(s, w_ref[...], preferred_element_type=F32,
                         precision=lax.Precision.HIGHEST) + b_ref[...]


def _ada_mods(cc, ada_w, ada_b):
    depth = ada_w.shape[0]
    d = cc.shape[1]
    return pl.pallas_call(
        _ada_kernel,
        grid=(depth, 3),
        in_specs=[pl.BlockSpec((8, d), lambda l, j: (0, 0)),
                  pl.BlockSpec((None, d, d), lambda l, j: (l, 0, j)),
                  pl.BlockSpec((None, 1, d), lambda l, j: (l, 0, j))],
        out_specs=pl.BlockSpec((None, 8, d), lambda l, j: (l, 0, j)),
        out_shape=jax.ShapeDtypeStruct((depth, 8, 3 * d), F32),
        compiler_params=_cparams(("parallel", "parallel")),
        name="ada_mods",
    )(cc, ada_w, ada_b.reshape(depth, 1, 3 * d))


def _mod_norm(x, g, scale, shift):
    y = x * lax.rsqrt(jnp.mean(x * x, axis=-1, keepdims=True) + EPS)
    return (y * g) * (1.0 + scale) + shift


def _rms(x, g):
    return x * lax.rsqrt(jnp.mean(x * x, axis=-1, keepdims=True) + EPS) * g


def _silu(x):
    return x * jax.nn.sigmoid(x)


def _dwconv3_ext(u_ext, w, b, first, last, tm):
    n = tm + 2 * SUBLANE
    prev = pltpu.roll(u_ext, 1, 0)[SUBLANE:SUBLANE + tm]
    nxt = pltpu.roll(u_ext, n - 1, 0)[SUBLANE:SUBLANE + tm]
    row = lax.broadcasted_iota(jnp.int32, (tm, 1), 0)
    prev = jnp.where(jnp.logical_and(first, row == 0), 0.0, prev)
    nxt = jnp.where(jnp.logical_and(last, row == tm - 1), 0.0, nxt)
    return prev * w[0:1] + u_ext[SUBLANE:SUBLANE + tm] * w[1:2] + nxt * w[2:3] + b


def _kp_kernel(xm_ref, xp_ref, xn_ref, g_ref, sc_ref, sh_ref, w2_ref, b2_ref, cw_ref, cb_ref,
               w3_ref, b3_ref, qg_ref, kvg_ref, wq_ref, wqs_ref, wk_ref, wv_ref, vb_ref, e2_ref,
               cosq_ref, sinq_ref, cosk_ref, sink_ref, *rest, tm, tiles_per_seq):
    vh_ref, x1_ref, m2_ref, q_ref, k_ref, v_ref = rest[-6:]
    i = pl.program_id(0)
    first = (i % tiles_per_seq) == 0
    last = (i % tiles_per_seq) == tiles_per_seq - 1
    x_ext = jnp.concatenate([xp_ref[...], xm_ref[...], xn_ref[...]], axis=0)
    hb_ext = _mod_norm(x_ext, g_ref[...], sc_ref[...], sh_ref[...]).astype(BF16)

    p2 = jnp.dot(hb_ext, w2_ref[...], preferred_element_type=F32) + b2_ref[...]
    u = _dwconv3_ext(p2[:, 0:3 * W_HYENA], cw_ref[...], cb_ref[...], first, last, tm)
    zh = p2[SUBLANE:SUBLANE + tm, 3 * W_HYENA:4 * W_HYENA]
    vh_ref[...] = u[:, 0:W_HYENA].astype(vh_ref.dtype)
    x1_ref[...] = u[:, W_HYENA:2 * W_HYENA].astype(x1_ref.dtype)
    m2_ref[...] = (_silu(zh) * u[:, 2 * W_HYENA:3 * W_HYENA]).astype(m2_ref.dtype)

    p = jnp.dot(hb_ext[SUBLANE:SUBLANE + tm], w3_ref[...], preferred_element_type=F32) + b3_ref[...]
    cq = _rms(p[:, 0:Q_LORA], qg_ref[...]).astype(BF16)
    ckv = _rms(p[:, Q_LORA:Q_LORA + KV_LORA], kvg_ref[...]).astype(BF16)
    kr = p[:, 640:640 + QK_ROPE]
    krs = p[:, 768:768 + QK_ROPE]
    qa = jnp.dot(cq, wq_ref[...], preferred_element_type=F32)
    qb = jnp.dot(cq, wqs_ref[...], preferred_element_type=F32)
    q_ref[...] = (qa * cosq_ref[...] + qb * sinq_ref[...]).astype(q_ref.dtype)
    kr_rot = (kr * cosk_ref[...] + krs * sink_ref[...]).astype(BF16)
    kn = jnp.dot(ckv, wk_ref[...], preferred_element_type=F32)
    krp = jnp.dot(kr_rot, e2_ref[...], preferred_element_type=F32)
    k_ref[...] = (kn + krp).astype(k_ref.dtype)
    v_ref[...] = (jnp.dot(ckv, wv_ref[...], preferred_element_type=F32) + vb_ref[...]).astype(v_ref.dtype)


def _mod_spec(mod, tiles_per_seq):
    d = mod.shape[-1]
    if mod.shape[0] == 1:
        return pl.BlockSpec((None, 1, d), lambda i: (0, 0, 0))
    return pl.BlockSpec((None, 1, d), lambda i: (i // tiles_per_seq, 0, 0))


def _full_spec(a):
    nd = a.ndim
    return pl.BlockSpec(a.shape, lambda i: (0,) * nd)


def _halo_specs(n_rows, tm, d):
    r = tm // SUBLANE
    nb = n_rows // SUBLANE
    return [pl.BlockSpec((tm, d), lambda i: (i, 0)),
            pl.BlockSpec((SUBLANE, d), lambda i: (jnp.maximum(i * r - 1, 0), 0)),
            pl.BlockSpec((SUBLANE, d), lambda i: (jnp.minimum((i + 1) * r, nb - 1), 0))]


def _kp_call(x2d, bsz, seq_len, kv_len, kv_row0, mods, g, lw, cw, cb, tabs, kv_bufs=None):
    n_rows, d = x2d.shape
    tm = min(512, seq_len)
    tiles_per_seq = seq_len // tm
    blk0 = kv_row0 // tm
    assert kv_row0 % tm == 0
    scale, shift = mods
    g2 = g.reshape(1, 1, d)
    cosq, sinq, cosk, sink = tabs
    consts = [lw['w2'], lw['b2'], cw, cb, lw['w3'], lw['b3'], lw['qg'], lw['kvg'], lw['wq'], lw['wqs'],
              lw['wk'], lw['wv'], lw['vb'], lw['e2']]

    def tab_spec(t):
        return pl.BlockSpec((tm, t.shape[1]), lambda i: (i % tiles_per_seq, 0))

    in_specs = _halo_specs(n_rows, tm, d) + [
        _mod_spec(g2, tiles_per_seq), _mod_spec(scale, tiles_per_seq), _mod_spec(shift, tiles_per_seq)]
    in_specs += [_full_spec(a) for a in consts]
    in_specs += [tab_spec(t) for t in (cosq, sinq, cosk, sink)]
    args = [x2d, x2d, x2d, g2, scale, shift, *consts, cosq, sinq, cosk, sink]
    hq = N_HEADS * HEAD_PAD
    tok = lambda wd: pl.BlockSpec((tm, wd), lambda i: (i, 0))
    tok_shape = lambda wd: jax.ShapeDtypeStruct((n_rows, wd), BF16)
    kv_spec = pl.BlockSpec((None, tm, hq), lambda i: (i // tiles_per_seq, blk0 + i % tiles_per_seq, 0))
    kv_shape = jax.ShapeDtypeStruct((bsz, kv_len, hq), BF16)
    aliases = {}
    if kv_bufs is not None:
        in_specs += [pl.BlockSpec(memory_space=pl.ANY)] * 2
        aliases = {len(args): 4, len(args) + 1: 5}
        args += list(kv_bufs)
    return pl.pallas_call(
        functools.partial(_kp_kernel, tm=tm, tiles_per_seq=tiles_per_seq),
        grid=(n_rows // tm,),
        in_specs=in_specs,
        out_specs=[tok(W_HYENA), tok(W_HYENA), tok(W_HYENA), tok(hq), kv_spec, kv_spec],
        out_shape=[tok_shape(W_HYENA), tok_shape(W_HYENA), tok_shape(W_HYENA), tok_shape(hq),
                   kv_shape, kv_shape],
        input_output_aliases=aliases,
        compiler_params=_cparams(("parallel",)),
        name="kp_hyena_mla" if kv_bufs is None else "kp_hyena_mla_ctx",
    )(*args)


ATTN_CHAINS = 4


def _attn_kernel(q_ref, k_ref, v_ref, o_ref, s_ref, p_ref, m_ref, acc_ref):
    h = pl.program_id(2)
    ki = pl.program_id(3)

    @pl.when(ki == 0)
    def _():
        m_ref[...] = jnp.full(m_ref.shape, -jnp.inf, F32)
        acc_ref[...] = jnp.zeros(acc_ref.shape, F32)

    hq = q_ref.shape[0] // ATTN_CHAINS
    chains = [pl.ds(c * hq, hq) for c in range(ATTN_CHAINS)]
    for rows in chains:
        s_ref[rows, :] = lax.dot_general(q_ref[rows, :], k_ref[...], (((1,), (1,)), ((), ())),
                                         preferred_element_type=F32)
    alphas = []
    reps = s_ref.shape[1] // LANE
    for rows in chains:
        m_prev = m_ref[rows, :]
        m_new = jnp.maximum(m_prev, jnp.max(s_ref[rows, :], axis=-1, keepdims=True))
        alphas.append(jnp.exp2(m_prev - m_new))
        m_ref[rows, :] = m_new
        p_ref[rows, :] = jnp.exp2((s_ref[rows, :] - jnp.tile(m_new, (1, reps))).astype(BF16))
    for rows, alpha in zip(chains, alphas):
        acc_ref[rows, :] = acc_ref[rows, :] * alpha + jnp.dot(p_ref[rows, :], v_ref[...],
                                                              preferred_element_type=F32)

    @pl.when(ki == pl.num_programs(3) - 1)
    def _():
        acc = acc_ref[...]
        o = (acc[:, 0:V_HEAD] / acc[:, V_HEAD:V_HEAD + 1]).astype(o_ref.dtype)

        @pl.when(h % 2 == 0)
        def _():
            o_ref[:, 0:V_HEAD] = o

        @pl.when(h % 2 == 1)
        def _():
            o_ref[:, V_HEAD:2 * V_HEAD] = o


MXU_DIM = 256
Q_TILE = 2048
KV_TILE_CAP = 2816


def _kv_tile(lk):
    for step in (MXU_DIM, LANE):
        cands = [t for t in range(step, min(lk, KV_TILE_CAP) + 1, step) if lk % t == 0]
        if cands:
            return cands[-1]
    return lk


def _attention(q, k, v, kv_row0=0, kv_len=None):
    b, lq, _ = q.shape
    lk = k.shape[1] if kv_len is None else kv_len
    tq = min(Q_TILE, lq)
    tk = _kv_tile(lk)
    assert kv_row0 % tk == 0
    kb0 = kv_row0 // tk
    grid = (b, lq // tq, N_HEADS, lk // tk)
    return pl.pallas_call(
        _attn_kernel,
        grid=grid,
        in_specs=[pl.BlockSpec((None, tq, HEAD_PAD), lambda bi, qi, h, ki: (bi, qi, h)),
                  pl.BlockSpec((None, tk, HEAD_PAD), lambda bi, qi, h, ki: (bi, kb0 + ki, h)),
                  pl.BlockSpec((None, tk, HEAD_PAD), lambda bi, qi, h, ki: (bi, kb0 + ki, h))],
        out_specs=pl.BlockSpec((None, tq, 2 * V_HEAD), lambda bi, qi, h, ki: (bi, qi, h // 2)),
        out_shape=jax.ShapeDtypeStruct((b, lq, W_MLA), BF16),
        scratch_shapes=[pltpu.VMEM((tq, tk), F32), pltpu.VMEM((tq, tk), BF16),
                        pltpu.VMEM((tq, LANE), F32), pltpu.VMEM((tq, HEAD_PAD), F32)],
        compiler_params=_cparams(("parallel", "parallel", "arbitrary", "arbitrary")),
        name="mla_attention",
    )(q, k, v)


COL_BWD = HYENA_EMB
COL_DROP = HYENA_EMB + 1


def _dot_bf16x3(a, b):
    a_hi = a.astype(BF16)
    b_hi = b.astype(BF16)
    a_lo = (a - a_hi.astype(F32)).astype(BF16)
    b_lo = (b - b_hi.astype(F32)).astype(BF16)
    dot = functools.partial(jnp.dot, preferred_element_type=F32)
    return dot(a_hi, b_hi) + (dot(a_lo, b_hi) + dot(a_hi, b_lo))


def _filt_kernel(z_ref, w1a_ref, w1b_ref, b1_ref, w2_ref, b2_ref, fr_ref, w3a_ref, w3b_ref, dl_ref,
                 k_ref, s_ref):
    i = pl.program_id(1)
    hp = lax.Precision.HIGHEST
    wc = HYENA_ORDER * W_HYENA
    z = z_ref[...]
    half = z.shape[0] // 2
    fr = fr_ref[...]
    pre = _dot_bf16x3(z[:half], w1a_ref[...]) + _dot_bf16x3(z[half:], w1b_ref[...])
    hid = jnp.sin(fr * (pre + b1_ref[...]))
    hid = jnp.sin(fr * (jnp.dot(hid, w2_ref[...], preferred_element_type=F32, precision=hp) + b2_ref[...]))
    hb = hid.astype(BF16)
    h2 = jnp.concatenate([jnp.dot(hb, w3a_ref[...], preferred_element_type=F32),
                          jnp.dot(hb, w3b_ref[...], preferred_element_type=F32)], axis=0)
    h = jnp.where(z[:, COL_BWD:COL_BWD + 1] > 0.5, h2[:, wc:], h2[:, :wc])
    t = z[:, 0:1]
    h = h * jnp.exp(-t * dl_ref[...])

    @pl.when(i == 0)
    def _():
        s_ref[...] = jnp.zeros(s_ref.shape, F32)

    s_ref[...] += jnp.sum(jnp.abs(h), axis=0, keepdims=True)
    k_ref[...] = jnp.where(z[:, COL_DROP:COL_DROP + 1] > 0.5, 0.0, h)


def _filter_tables(seq_len, permuted):
    t = jnp.linspace(0.0, 1.0, seq_len, dtype=F32)[:, None]
    w = 2.0 * math.pi * jnp.arange(seq_len, dtype=F32)[:, None] / seq_len
    f = jnp.linspace(1e-4, HYENA_BANDS - 1, HYENA_BANDS, dtype=F32)[None, :]
    z = jnp.concatenate([t, jnp.cos(f * w), -jnp.sin(f * w)], axis=-1)
    z2 = jnp.concatenate([z, z[0:1], jnp.flip(z[1:], axis=0)], axis=0)
    n = jnp.arange(2 * seq_len)
    flags = jnp.stack([n >= seq_len, n == seq_len], axis=-1).astype(F32)
    zz = jnp.concatenate([z2, flags], axis=-1)
    zz = jnp.pad(zz, ((0, 0), (0, LANE - zz.shape[1])))
    if permuted:
        zz = zz.reshape(2 * seq_len // N2, N2, LANE).transpose(1, 0, 2).reshape(2 * seq_len, LANE)
    return zz


def _hyena_deltas():
    max_decay = math.log(HYENA_TARGET) / HYENA_FAST_DECAY
    min_decay = math.log(HYENA_TARGET) / HYENA_SLOW_DECAY
    deltas = jnp.abs(jnp.linspace(min_decay, max_decay, W_HYENA, dtype=F32))
    return jnp.tile(deltas, HYENA_ORDER)[None, :]


def _filter_gen(seq_len, n_layers, permuted, hy_w1, hy_b1, hy_w2, hy_b2, hy_w3, hy_freq):
    zz = _filter_tables(seq_len, permuted)
    tmf = min(512, seq_len)
    tiles_half = seq_len // tmf
    wc = HYENA_ORDER * W_HYENA
    hh = HYENA_HID
    zpad = lambda a, lo, hi, ax: jnp.pad(a, [(lo, hi) if d == ax else (0, 0) for d in range(a.ndim)])
    w1p = jnp.pad(hy_w1[:n_layers], ((0, 0), (0, LANE - HYENA_EMB), (0, 0)))
    w1a, w1b = zpad(w1p, 0, hh, 2), zpad(w1p, hh, 0, 2)
    w2 = hy_w2[:n_layers]
    w2bd = jnp.concatenate([zpad(w2, 0, hh, 2), zpad(w2, hh, 0, 2)], axis=1)
    w3 = hy_w3[:n_layers].astype(BF16)
    w3a, w3b = zpad(w3, 0, hh, 1), zpad(w3, hh, 0, 1)
    r1 = lambda a: jnp.tile(a[:n_layers].reshape(n_layers, 1, hh), (1, 1, 2))
    per_layer = lambda r, c: pl.BlockSpec((None, r, c), lambda l, i: (l, 0, 0))
    return pl.pallas_call(
        _filt_kernel,
        grid=(n_layers, 2 * tiles_half),
        in_specs=[pl.BlockSpec((tmf, LANE), lambda l, i: (i, 0)),
                  per_layer(LANE, LANE), per_layer(LANE, LANE), per_layer(1, LANE),
                  per_layer(LANE, LANE), per_layer(1, LANE), per_layer(1, LANE),
                  per_layer(LANE, 2 * wc), per_layer(LANE, 2 * wc),
                  pl.BlockSpec((1, wc), lambda l, i: (0, 0))],
        out_specs=[pl.BlockSpec((tmf, wc), lambda l, i: (i, l)),
                   pl.BlockSpec((None, 1, wc), lambda l, i: (l, 0, 0))],
        out_shape=[jax.ShapeDtypeStruct((2 * seq_len, n_layers * wc), F32),
                   jax.ShapeDtypeStruct((n_layers, 1, wc), F32)],
        compiler_params=_cparams(("parallel", "arbitrary")),
        name="hyena_filter",
    )(zz, w1a, w1b, r1(hy_b1), w2bd, r1(hy_b2), r1(hy_freq), w3a, w3b, _hyena_deltas())


def _angles(a, b, n):
    m = (a * b) % n
    return m.astype(F32) * (2.0 * math.pi / n)


def _dft_tables(nh):
    n1 = 2 * nh
    n = n1 * N2
    k1 = jnp.arange(n1, dtype=jnp.int32)
    th = _angles(k1[:, None], jnp.arange(nh, dtype=jnp.int32)[None, :], n1)
    c, s = jnp.cos(th), jnp.sin(th)
    w1_pair = jnp.concatenate([jnp.concatenate([c, s], 1), jnp.concatenate([-s, c], 1)], 0)
    thf = _angles(k1[:, None], k1[None, :], n1)
    w1_real = jnp.concatenate([jnp.cos(thf), -jnp.sin(thf)], 0)
    g1 = jnp.concatenate([jnp.concatenate([c.T, -s.T], 1), jnp.concatenate([s.T, c.T], 1)], 0) / n
    n2 = jnp.arange(N2, dtype=jnp.int32)
    al = _angles(k1[:, None], n2[None, :], n)
    be = _angles(n2[:, None], n2[None, :], N2)
    ca, sa = jnp.cos(al)[:, None, :], jnp.sin(al)[:, None, :]
    cb, sb = jnp.cos(be)[None], jnp.sin(be)[None]
    cp, sp = ca * cb - sa * sb, sa * cb + ca * sb
    f2 = jnp.concatenate([jnp.concatenate([cp, sp], 2), jnp.concatenate([-sp, cp], 2)], 1)
    cpt, spt = cp.transpose(0, 2, 1), sp.transpose(0, 2, 1)
    g2 = jnp.concatenate([jnp.concatenate([cpt, -spt], 2), jnp.concatenate([spt, cpt], 2)], 1)
    g1 = g1.reshape(2 * nh, 2, n1).transpose(0, 2, 1).reshape(2 * nh, 2 * n1)
    f2 = f2.reshape(n1, 2 * N2, 2, N2).transpose(0, 1, 3, 2).reshape(n1, 2 * N2, 2 * N2)
    g2 = g2.reshape(n1, 2, N2, 2 * N2).transpose(0, 2, 1, 3).reshape(n1, 2 * N2, 2 * N2)
    w1_pair = w1_pair.reshape(2, n1, 2 * nh).transpose(1, 0, 2).reshape(2 * n1, 2 * nh)
    w1_real = w1_real.reshape(2, n1, n1).transpose(1, 0, 2).reshape(2 * n1, n1)
    return dict(w1_pair=w1_pair.astype(BF16), w1_real=w1_real.astype(BF16), g1=g1.astype(BF16),
                f2=f2.astype(BF16), g2=g2.astype(BF16))


def _strided_rows(ref, start, size):
    parts = [ref[cc, pl.ds(start, size, stride=SUBLANE), :] for cc in range(ref.shape[0])]
    return parts[0] if len(parts) == 1 else jnp.concatenate(parts, axis=1)


def _pack_pairs(x):
    return pltpu.bitcast(x.astype(BF16), jnp.uint32)


def _unpack_pairs(packed):
    return pltpu.bitcast(packed, BF16)


def _store_chunks(ref, row0, val):
    for cc in range(ref.shape[0]):
        ref[cc, pl.ds(row0, SUBLANE), :] = val[:, cc * LANE:(cc + 1) * LANE]


def _stage_a_kernel(*refs, n2t, n1, has_inv, has_fwd, has_div):
    it = iter(refs)
    y_ref = next(it) if has_inv else None
    g1_ref = next(it) if has_inv else None
    u_ref = next(it)
    mul_ref = next(it) if has_inv else None
    skip_ref = next(it) if has_inv else None
    div_ref = next(it) if has_div else None
    w1_ref = next(it) if has_fwd else None
    e_ref = next(it) if has_inv else None
    a_ref = next(it) if has_fwd else None
    yy_ref = next(it) if has_inv else None
    if has_inv:
        for j in range(n2t):
            nb, jl = j // SUBLANE, j % SUBLANE
            rhs = _unpack_pairs(_strided_rows(y_ref.at[nb], jl, n1))
            yy_ref[j] = jnp.dot(g1_ref[...], rhs, preferred_element_type=F32)
        for j in range(n2t):
            e = mul_ref[j].astype(F32) * (yy_ref[j] + u_ref[j].astype(F32) * skip_ref[...])
            e_ref[j] = e.astype(e_ref.dtype)
    if has_fwd:
        for j in range(n2t):
            if has_inv:
                src = e_ref[j]
            elif has_div:
                src = (u_ref[j] / div_ref[...]).astype(BF16)
            else:
                src = u_ref[j].astype(BF16)
            a = jnp.dot(w1_ref[...], src, preferred_element_type=F32)
            packed = _pack_pairs(a)
            for kt in range(n1 // SUBLANE):
                _store_chunks(a_ref.at[kt], j * SUBLANE, packed[kt * SUBLANE:(kt + 1) * SUBLANE])


def _stage_a(tabs, nh, u, *, y=None, mul=None, skip=None, div=None, w1=None, want_fwd=True,
             ct=256, n2t=32, e_dtype=BF16, c_off=0):
    p, _, rows, c = u.shape
    n1 = 2 * nh
    has_inv = y is not None
    has_div = div is not None
    ct = min(ct, c)
    nbk = n2t // SUBLANE
    grid = (p, c // ct, N2 // n2t)
    args, specs = [], []
    if has_inv:
        args += [y, tabs['g1']]
        specs += [pl.BlockSpec((None, nbk, ct // LANE, n1 * SUBLANE, LANE),
                               lambda pi, ci, ni: (pi, ni, ci, 0, 0)),
                  pl.BlockSpec(tabs['g1'].shape, lambda pi, ci, ni: (0, 0))]
    args.append(u)
    specs.append(pl.BlockSpec((None, n2t, rows, ct), lambda pi, ci, ni: (pi, ni, 0, ci)))
    if has_inv:
        args += [mul, skip]
        specs += [pl.BlockSpec((None, n2t, rows, ct), lambda pi, ci, ni: (pi, ni, 0, ci)),
                  pl.BlockSpec((1, ct), lambda pi, ci, ni: (0, ci + c_off // ct))]
    if has_div:
        args.append(div)
        specs.append(pl.BlockSpec((1, ct), lambda pi, ci, ni: (0, ci)))
    if want_fwd:
        args.append(w1)
        specs.append(pl.BlockSpec(w1.shape, lambda pi, ci, ni: (0, 0)))
    out_shape, out_specs = [], []
    if has_inv:
        out_shape.append(jax.ShapeDtypeStruct(u.shape, e_dtype))
        out_specs.append(pl.BlockSpec((None, n2t, rows, ct), lambda pi, ci, ni: (pi, ni, 0, ci)))
    if want_fwd:
        out_shape.append(jax.ShapeDtypeStruct((p, n1 // SUBLANE, c // LANE, N2 * SUBLANE, LANE), jnp.uint32))
        out_specs.append(pl.BlockSpec((None, n1 // SUBLANE, ct // LANE, n2t * SUBLANE, LANE),
                                      lambda pi, ci, ni: (pi, 0, ci, ni, 0)))
    outs = pl.pallas_call(
        functools.partial(_stage_a_kernel, n2t=n2t, n1=n1, has_inv=has_inv, has_fwd=want_fwd,
                          has_div=has_div),
        grid=grid, in_specs=specs, out_specs=out_specs, out_shape=out_shape,
        scratch_shapes=[pltpu.VMEM((n2t, rows, ct), F32)] if has_inv else [],
        compiler_params=_cparams(("parallel", "parallel", "parallel")),
        name="hyena_stage_a" + ("_inv" if has_inv else "") + ("_fwd" if want_fwd else ""),
    )(*args)
    return outs


def _stage_b_kernel(*refs, filt_only):
    if filt_only:
        a_ref, f2_ref, o_ref = refs
        for kl in range(SUBLANE):
            rhs = _unpack_pairs(_strided_rows(a_ref, kl, N2))
            o_ref[kl] = jnp.dot(f2_ref[kl], rhs, preferred_element_type=F32).astype(o_ref.dtype)
        return
    a_ref, f2_ref, g2_ref, kf_ref, o_ref, t_ref, z_ref = refs
    for kl in range(SUBLANE):
        rhs = _unpack_pairs(_strided_rows(a_ref, kl, N2))
        t_ref[kl] = jnp.dot(f2_ref[kl], rhs, preferred_element_type=F32)
    for kl in range(SUBLANE):
        kf = kf_ref[kl].astype(F32)
        tre, tim = t_ref[kl, :N2], t_ref[kl, N2:]
        kre, kim = kf[:N2], kf[N2:]
        z_ref[kl, :N2] = (tre * kre - tim * kim).astype(BF16)
        z_ref[kl, N2:] = (tre * kim + tim * kre).astype(BF16)
    for kl in range(SUBLANE):
        y = jnp.dot(g2_ref[kl], z_ref[kl], preferred_element_type=F32)
        packed = _pack_pairs(y)
        for nt in range(N2 // SUBLANE):
            _store_chunks(o_ref.at[nt], kl * SUBLANE, packed[nt * SUBLANE:(nt + 1) * SUBLANE])


def _stage_b(tabs, a, kf=None, *, c_off=0, ct=512):
    p, nkt, ncc, _, _ = a.shape
    c = ncc * LANE
    n1 = nkt * SUBLANE
    ct = min(ct, c)
    filt_only = kf is None
    grid = (nkt, c // ct, p)
    a_spec = pl.BlockSpec((None, None, ct // LANE, N2 * SUBLANE, LANE),
                          lambda kt, ci, pi: (pi, kt, ci, 0, 0))
    tab_spec = pl.BlockSpec((SUBLANE, 2 * N2, 2 * N2), lambda kt, ci, pi: (kt, 0, 0))
    if filt_only:
        return pl.pallas_call(
            functools.partial(_stage_b_kernel, filt_only=True),
            grid=grid, in_specs=[a_spec, tab_spec],
            out_specs=pl.BlockSpec((SUBLANE, 2 * N2, ct), lambda kt, ci, pi: (kt, 0, ci)),
            out_shape=jax.ShapeDtypeStruct((n1, 2 * N2, c), BF16),
            compiler_params=_cparams(("parallel", "parallel", "parallel")),
            name="hyena_stage_b_filter",
        )(a, tabs['f2'])
    return pl.pallas_call(
        functools.partial(_stage_b_kernel, filt_only=False),
        grid=grid,
        in_specs=[a_spec, tab_spec, tab_spec,
                  pl.BlockSpec((SUBLANE, 2 * N2, ct), lambda kt, ci, pi: (kt, 0, ci + c_off // ct))],
        out_specs=pl.BlockSpec((None, N2 // SUBLANE, ct // LANE, SUBLANE * SUBLANE, LANE),
                               lambda kt, ci, pi: (pi, 0, ci, kt, 0)),
        out_shape=jax.ShapeDtypeStruct((p, N2 // SUBLANE, c // LANE, n1 * SUBLANE, LANE), jnp.uint32),
        scratch_shapes=[pltpu.VMEM((SUBLANE, 2 * N2, ct), F32), pltpu.VMEM((SUBLANE, 2 * N2, ct), BF16)],
        compiler_params=_cparams(("parallel", "parallel", "parallel")),
        name="hyena_stage_b",
    )(a, tabs['f2'], tabs['g2'], kf)


def _permute_seq(a, nh):
    b, _, c = a.shape
    return a.reshape(b // 2, 2, nh, N2, c).transpose(0, 3, 1, 2, 4).reshape(b // 2, N2, 2 * nh, c)


def _unpermute_seq(a, nh):
    p, _, _, c = a.shape
    return a.reshape(p, N2, 2, nh, c).transpose(0, 2, 3, 1, 4).reshape(2 * p, nh * N2, c)


def _filter_spectrum(tabs, nh, k_un, ssum):
    n, call = k_un.shape
    kp = k_un.reshape(1, N2, 2 * nh, call)
    (a,) = _stage_a(tabs, nh, kp, div=ssum.reshape(1, call), w1=tabs['w1_real'], ct=512, n2t=16)
    return _stage_b(tabs, a)


def _hyena_long(tabs, nh, kf, layer, v, x1, m2, skip):
    vp, x1p, m2p = (_permute_seq(t, nh) for t in (v, x1, m2))
    c0 = layer * HYENA_ORDER * W_HYENA
    skip2 = skip.reshape(1, HYENA_ORDER * W_HYENA)
    (a1,) = _stage_a(tabs, nh, vp, w1=tabs['w1_pair'])
    y1 = _stage_b(tabs, a1, kf, c_off=c0)
    z, a2 = _stage_a(tabs, nh, vp, y=y1, mul=x1p, skip=skip2, w1=tabs['w1_pair'], c_off=0)
    y2 = _stage_b(tabs, a2, kf, c_off=c0 + W_HYENA)
    (gp,) = _stage_a(tabs, nh, z, y=y2, mul=m2p, skip=skip2, want_fwd=False, c_off=W_HYENA)
    return _unpermute_seq(gp, nh)


def _ctx_conv_kernel(v_ref, x1_ref, m2_ref, k_ref, s_ref, skip_ref, ff_ref, fk_ref, gi_ref, o_ref,
                     *, n):
    def conv(u, o):
        kfull = k_ref[:, o * W_HYENA:(o + 1) * W_HYENA] / s_ref[:, o * W_HYENA:(o + 1) * W_HYENA]
        kf = jnp.dot(fk_ref[...], kfull.astype(BF16), preferred_element_type=F32)
        uf = jnp.dot(ff_ref[...], u.astype(BF16), preferred_element_type=F32)
        ure, uim, kre, kim = uf[:n], uf[n:], kf[:n], kf[n:]
        z = jnp.concatenate([ure * kre - uim * kim, ure * kim + uim * kre], axis=0).astype(BF16)
        y = jnp.dot(gi_ref[...], z, preferred_element_type=F32)
        return y + u * skip_ref[o:o + 1, :]

    v = v_ref[...].astype(F32)
    z1 = x1_ref[...].astype(F32) * conv(v, 0)
    o_ref[...] = (m2_ref[...].astype(F32) * conv(z1, 1)).astype(o_ref.dtype)


def _ctx_tables(lc):
    n = 2 * lc
    k = jnp.arange(n, dtype=jnp.int32)
    ph = _angles(k[:, None], k[None, :], n)
    c, s = jnp.cos(ph), jnp.sin(ph)
    fk = jnp.concatenate([c, -s], axis=0)
    ff = fk[:, :lc]
    gi = jnp.concatenate([c[:lc], -s[:lc]], axis=1) / n
    return ff.astype(BF16), fk.astype(BF16), gi.astype(BF16)


def _ctx_hyena(ctabs, k_un, ssum, layer, v, x1, m2, skip):
    b, lc, c = v.shape
    n = 2 * lc
    ff, fk, gi = ctabs
    wc = HYENA_ORDER * W_HYENA
    tok = pl.BlockSpec((None, lc, c), lambda bi: (bi, 0, 0))
    return pl.pallas_call(
        functools.partial(_ctx_conv_kernel, n=n),
        grid=(b,),
        in_specs=[tok, tok, tok,
                  pl.BlockSpec((n, wc), lambda bi: (0, layer)),
                  pl.BlockSpec((None, 1, wc), lambda bi: (layer, 0, 0)),
                  pl.BlockSpec((HYENA_ORDER, c), lambda bi: (0, 0)),
                  _full_spec(ff), _full_spec(fk), _full_spec(gi)],
        out_specs=tok,
        out_shape=jax.ShapeDtypeStruct((b, lc, c), BF16),
        compiler_params=_cparams(("parallel",)),
        name="ctx_hyena",
    )(v, x1, m2, k_un, ssum, skip, ff, fk, gi)


def _merge_kernel(xm_ref, xp_ref, xn_ref, gh_ref, att_ref, g_ref, sc_ref, sh_ref, gt_ref,
                  w1_ref, b1_ref, cw_ref, cb_ref, wg_ref, bg_ref, sco_ref, hyo_ref, mlo_ref, wo_ref,
                  fg_ref, o_ref, *, final, tm, tiles_per_seq):
    i = pl.program_id(0)
    first = (i % tiles_per_seq) == 0
    last = (i % tiles_per_seq) == tiles_per_seq - 1
    x = xm_ref[...]
    d = x.shape[-1]
    x_ext = jnp.concatenate([xp_ref[...], x, xn_ref[...]], axis=0)
    hb_ext = _mod_norm(x_ext, g_ref[...], sc_ref[...], sh_ref[...]).astype(BF16)
    p = jnp.dot(hb_ext, w1_ref[...], preferred_element_type=F32) + b1_ref[...]
    prod = p[:, 2 * W_CONV:3 * W_CONV] * p[:, 0:W_CONV]
    conv = _dwconv3_ext(prod, cw_ref[...], cb_ref[...], first, last, tm)
    pm = p[SUBLANE:SUBLANE + tm]
    a = (_silu(pm[:, 3 * W_CONV:4 * W_CONV]) * (pm[:, W_CONV:2 * W_CONV] * conv)).astype(BF16)
    sm = _silu(pm[:, 4 * W_CONV:4 * W_CONV + W_MLA])
    hb = hb_ext[SUBLANE:SUBLANE + tm]
    gates = jax.nn.sigmoid(jnp.dot(hb, wg_ref[...], preferred_element_type=F32) + bg_ref[...])
    ya = jnp.dot(a, sco_ref[...], preferred_element_type=F32)
    yh = jnp.dot(gh_ref[...], hyo_ref[...], preferred_element_type=F32)
    am = (sm * att_ref[...].astype(F32)).astype(BF16)
    ym = jnp.dot(am, mlo_ref[...], preferred_element_type=F32)
    y = gates[:, 0:d] * ya + gates[:, d:2 * d] * yh + gates[:, 2 * d:3 * d] * ym
    o = jnp.dot(y.astype(BF16), wo_ref[...], preferred_element_type=F32)
    xn = x + gt_ref[...] * o
    if final:
        xn = _rms(xn, fg_ref[...])
    o_ref[...] = xn


def _merge_call(x2d, seq_len, mods, g, lw, cw, cb, gh, att, final_g, final):
    n_rows, d = x2d.shape
    tm = min(512, seq_len)
    tiles_per_seq = seq_len // tm
    scale, shift, gate = mods
    g2 = g.reshape(1, 1, d)
    fg = final_g.reshape(1, d)
    tok = lambda wd: pl.BlockSpec((tm, wd), lambda i: (i, 0))
    consts = [lw['w1'], lw['b1'], cw, cb, lw['wg'], lw['bg'], lw['sc_out'], lw['hy_out'], lw['mla_out'],
              lw['w_o'], fg]
    in_specs = _halo_specs(n_rows, tm, d) + [
        tok(W_HYENA), tok(W_MLA),
        _mod_spec(g2, tiles_per_seq), _mod_spec(scale, tiles_per_seq),
        _mod_spec(shift, tiles_per_seq), _mod_spec(gate, tiles_per_seq)]
    in_specs += [_full_spec(c) for c in consts]
    return pl.pallas_call(
        functools.partial(_merge_kernel, final=final, tm=tm, tiles_per_seq=tiles_per_seq),
        grid=(n_rows // tm,),
        in_specs=in_specs,
        out_specs=tok(d),
        out_shape=jax.ShapeDtypeStruct((n_rows, d), F32),
        compiler_params=_cparams(("parallel",)),
        name="merge_final" if final else "merge",
    )(x2d, x2d, x2d, gh, att, g2, scale, shift, gate, *consts)


def _rope_swap_cols(w):
    half = QK_ROPE // 2
    return jnp.concatenate([-w[..., half:], w[..., :half]], axis=-1)


def _layer_weights(i, w_in, b_in, mla_q_norm, mla_w_uq, mla_kv_norm, mla_w_ukv, sc_out, hy_out,
                   mla_out, w_o):
    wi, bi = w_in[i], b_in[i]
    d = wi.shape[0]
    lw = {}
    lw['w1'] = jnp.concatenate([wi[:, O_XIN:O_HPROJ], wi[:, O_ZM:O_GATES]], axis=1).astype(BF16)
    lw['b1'] = jnp.concatenate([bi[O_XIN:O_HPROJ], bi[O_ZM:O_GATES]])[None, :]
    lw['w2'] = wi[:, O_HPROJ:O_CQ].astype(BF16)
    lw['b2'] = bi[O_HPROJ:O_CQ][None, :]
    wkr, bkr = wi[:, O_KR:O_ZM], bi[O_KR:O_ZM]
    zw = lambda n: jnp.zeros((d, n), F32)
    zb = lambda n: jnp.zeros((n,), F32)
    lw['w3'] = jnp.concatenate([wi[:, O_CQ:O_KR], wkr, zw(96), _rope_swap_cols(wkr), zw(96)], axis=1).astype(BF16)
    lw['b3'] = jnp.concatenate([bi[O_CQ:O_KR], bkr, zb(96), _rope_swap_cols(bkr), zb(96)])[None, :]
    lw['qg'] = mla_q_norm[i][None, :]
    lw['kvg'] = mla_kv_norm[i][None, :]
    wuq = mla_w_uq[i].reshape(Q_LORA, N_HEADS, QK_NOPE + QK_ROPE)
    pad = jnp.zeros((Q_LORA, N_HEADS, HEAD_PAD - QK_NOPE - QK_ROPE), F32)
    lw['wq'] = jnp.concatenate([wuq, pad], axis=-1).reshape(Q_LORA, N_HEADS * HEAD_PAD).astype(BF16)
    zn = jnp.zeros((Q_LORA, N_HEADS, QK_NOPE), F32)
    lw['wqs'] = jnp.concatenate([zn, _rope_swap_cols(wuq[..., QK_NOPE:]), pad], axis=-1).reshape(
        Q_LORA, N_HEADS * HEAD_PAD).astype(BF16)
    wukv = mla_w_ukv[i].reshape(KV_LORA, N_HEADS, QK_NOPE + V_HEAD)
    padk = jnp.zeros((KV_LORA, N_HEADS, HEAD_PAD - QK_NOPE), F32)
    lw['wk'] = jnp.concatenate([wukv[..., :QK_NOPE], padk], axis=-1).reshape(
        KV_LORA, N_HEADS * HEAD_PAD).astype(BF16)
    padv = jnp.zeros((KV_LORA, N_HEADS, HEAD_PAD - V_HEAD), F32)
    lw['wv'] = jnp.concatenate([wukv[..., QK_NOPE:], padv], axis=-1).reshape(
        KV_LORA, N_HEADS * HEAD_PAD).astype(BF16)
    lw['vb'] = jnp.zeros((N_HEADS, HEAD_PAD), F32).at[:, V_HEAD].set(1.0).reshape(1, N_HEADS * HEAD_PAD)
    e2 = jnp.zeros((QK_ROPE, N_HEADS, HEAD_PAD), F32).at[:, :, QK_NOPE:QK_NOPE + QK_ROPE].set(
        jnp.eye(QK_ROPE, dtype=F32)[:, None, :])
    lw['e2'] = e2.reshape(QK_ROPE, N_HEADS * HEAD_PAD).astype(BF16)
    lw['wg'] = wi[:, O_GATES:].astype(BF16)
    lw['bg'] = bi[O_GATES:][None, :]
    lw['sc_out'] = sc_out[i].astype(BF16)
    lw['hy_out'] = hy_out[i].astype(BF16)
    lw['mla_out'] = mla_out[i].astype(BF16)
    lw['w_o'] = w_o[i].astype(BF16)
    return lw


def _rope_tables(seq_len, use_rope):
    scale = (QK_NOPE + QK_ROPE) ** -0.5 * math.log2(math.e)
    if use_rope:
        rows = seq_len // GRID_W
        row = jnp.broadcast_to(jnp.arange(rows, dtype=F32)[:, None], (rows, GRID_W)).reshape(seq_len)
        col = jnp.broadcast_to(jnp.arange(GRID_W, dtype=F32)[None, :], (rows, GRID_W)).reshape(seq_len)
        n_f = QK_ROPE // 4
        inv = ROPE_BASE ** (-jnp.arange(n_f, dtype=F32) / n_f)
        ang = jnp.concatenate([row[:, None] * inv, col[:, None] * inv], axis=-1)
        cos, sin = jnp.cos(ang), jnp.sin(ang)
    else:
        cos = jnp.ones((seq_len, QK_ROPE // 2), F32)
        sin = jnp.zeros((seq_len, QK_ROPE // 2), F32)
    cosk = jnp.concatenate([cos, cos], axis=-1)
    sink = jnp.concatenate([sin, sin], axis=-1)
    ones = jnp.ones((seq_len, QK_NOPE), F32)
    zeros = jnp.zeros((seq_len, QK_NOPE), F32)
    tail = jnp.zeros((seq_len, HEAD_PAD - QK_NOPE - QK_ROPE), F32)
    cosq = jnp.tile(jnp.concatenate([ones, cosk, tail], axis=-1), (1, N_HEADS)) * scale
    sinq = jnp.tile(jnp.concatenate([zeros, sink, tail], axis=-1), (1, N_HEADS)) * scale
    return cosq, sinq, cosk, sink


def kernel(x, c, ctx, c_ctx, ada_w, ada_b, norm_g, w_in, b_in, sc_conv_w, sc_conv_b, sc_out, hy_conv_w,
           hy_conv_b, hy_w1, hy_b1, hy_w2, hy_b2, hy_w3, hy_freq, hy_skip, hy_out, mla_q_norm, mla_w_uq,
           mla_kv_norm, mla_w_ukv, mla_out, w_o, final_g):
    bsz, seq, d = x.shape
    lc = ctx.shape[1]
    depth = ada_w.shape[0]
    nh = seq // N2
    assert d == D_MODEL and bsz % 2 == 0 and nh % SUBLANE == 0 and bsz <= 7

    cc = jnp.zeros((8, d), F32).at[:bsz].set(c).at[bsz].set(c_ctx)
    mods = _ada_mods(cc, ada_w, ada_b)

    tabs = _dft_tables(nh)
    k_un, ssum = _filter_gen(seq, depth, True, hy_w1, hy_b1, hy_w2, hy_b2, hy_w3, hy_freq)
    kf = _filter_spectrum(tabs, nh, k_un, ssum)
    if depth > 1:
        kc_un, sc_sum = _filter_gen(lc, depth - 1, False, hy_w1, hy_b1, hy_w2, hy_b2, hy_w3, hy_freq)
        ctabs = _ctx_tables(lc)

    rope_l = _rope_tables(seq, True)
    rope_c = _rope_tables(lc, False)

    x_lat = x.reshape(bsz * seq, d)
    x_ctx = ctx.reshape(bsz * lc, d)
    for i in range(depth):
        last = i == depth - 1
        lw = _layer_weights(i, w_in, b_in, mla_q_norm, mla_w_uq, mla_kv_norm, mla_w_ukv, sc_out,
                            hy_out, mla_out, w_o)
        m = mods[i]
        split = lambda r: tuple(r[:, None, j * d:(j + 1) * d] for j in range(3))
        shift_l, scale_l, gate_l = split(m[:bsz])
        shift_c, scale_c, gate_c = split(m[bsz:bsz + 1])
        g = norm_g[i]
        scw, scb = sc_conv_w[i], sc_conv_b[i][None, :]
        hcw, hcb = hy_conv_w[i], hy_conv_b[i][None, :]

        hq = N_HEADS * HEAD_PAD
        v_h, x1_h, m2_h, q_l, k_all, v_all = _kp_call(
            x_lat, bsz, seq, seq + lc, 0, (scale_l, shift_l), g, lw, hcw, hcb, rope_l)
        v_hc, x1_hc, m2_hc, q_c, k_all, v_all = _kp_call(
            x_ctx, bsz, lc, seq + lc, seq, (scale_c, shift_c), g, lw, hcw, hcb, rope_c,
            kv_bufs=(k_all, v_all))
        att_l = _attention(q_l.reshape(bsz, seq, hq), k_all, v_all).reshape(bsz * seq, W_MLA)
        r3 = lambda t: t.reshape(bsz, seq, W_HYENA)
        gh_l = _hyena_long(tabs, nh, kf, i, r3(v_h), r3(x1_h), r3(m2_h), hy_skip[i])
        new_lat = _merge_call(x_lat, seq, (scale_l, shift_l, gate_l), g, lw, scw, scb,
                              gh_l.reshape(bsz * seq, W_HYENA), att_l, final_g, last)
        if not last:
            att_c = _attention(q_c.reshape(bsz, lc, hq), k_all, v_all, kv_row0=seq,
                               kv_len=lc).reshape(bsz * lc, W_MLA)
            rc = lambda t: t.reshape(bsz, lc, W_HYENA)
            gh_c = _ctx_hyena(ctabs, kc_un, sc_sum, i, rc(v_hc), rc(x1_hc), rc(m2_hc), hy_skip[i])
            x_ctx = _merge_call(x_ctx, lc, (scale_c, shift_c, gate_c), g, lw, scw, scb,
                                gh_c.reshape(bsz * lc, W_HYENA), att_c, final_g, False)
        x_lat = new_lat
    return x_lat.reshape(bsz, seq, d)
```

```python
import functools
import math

import jax
import jax.numpy as jnp
from jax import lax
from jax.experimental import pallas as pl
from jax.experimental.pallas import tpu as pltpu

F32 = jnp.float32
BF16 = jnp.bfloat16

D_MODEL = 1024
GRID_W = 64
W_CONV = 512
W_HYENA = 512
HYENA_ORDER = 2
HYENA_EMB = 33
HYENA_BANDS = (HYENA_EMB - 1) // 2
HYENA_HID = 64
HYENA_FAST_DECAY = 0.3
HYENA_SLOW_DECAY = 1.5
HYENA_TARGET = 1e-2
N_HEADS = 8
QK_NOPE = 64
QK_ROPE = 32
V_HEAD = 64
Q_LORA = 384
KV_LORA = 256
W_MLA = N_HEADS * V_HEAD
ROPE_BASE = 10000.0
N_BRANCH = 3
EPS = 1e-6

O_XIN, O_GB, O_GC, O_ZA = 0, 512, 1024, 1536
O_HPROJ, O_ZH = 2048, 3584
O_CQ, O_CKV, O_KR, O_ZM, O_GATES = 4096, 4480, 4736, 4768, 5280

LANE = 128
SUBLANE = 8
HEAD_PAD = 128
N2 = 128
VMEM_CAPACITY = 64 * 1024 * 1024
VMEM_LIMIT = VMEM_CAPACITY - 8 * 1024 * 1024


def _cparams(sem):
    return pltpu.CompilerParams(dimension_semantics=sem, vmem_limit_bytes=VMEM_LIMIT)


def _ada_kernel(c_ref, w_ref, b_ref, o_ref):
    c = c_ref[...]
    s = c * jax.nn.sigmoid(c)
    o_ref[...] = jnp.dot(s, w_ref[...], preferred_element_type=F32,
                         precision=lax.Precision.HIGHEST) + b_ref[...]


def _ada_mods(cc, ada_w, ada_b):
    depth = ada_w.shape[0]
    d = cc.shape[1]
    return pl.pallas_call(
        _ada_kernel,
        grid=(depth, 3),
        in_specs=[pl.BlockSpec((8, d), lambda l, j: (0, 0)),
                  pl.BlockSpec((None, d, d), lambda l, j: (l, 0, j)),
                  pl.BlockSpec((None, 1, d), lambda l, j: (l, 0, j))],
        out_specs=pl.BlockSpec((None, 8, d), lambda l, j: (l, 0, j)),
        out_shape=jax.ShapeDtypeStruct((depth, 8, 3 * d), F32),
        compiler_params=_cparams(("parallel", "parallel")),
        name="ada_mods",
    )(cc, ada_w, ada_b.reshape(depth, 1, 3 * d))


def _mod_norm(x, g, scale, shift):
    y = x * lax.rsqrt(jnp.mean(x * x, axis=-1, keepdims=True) + EPS)
    return (y * g) * (1.0 + scale) + shift


def _rms(x, g):
    return x * lax.rsqrt(jnp.mean(x * x, axis=-1, keepdims=True) + EPS) * g


def _silu(x):
    return x * jax.nn.sigmoid(x)


def _dwconv3_ext(u_ext, w, b, first, last, tm):
    n = tm + 2 * SUBLANE
    prev = pltpu.roll(u_ext, 1, 0)[SUBLANE:SUBLANE + tm]
    nxt = pltpu.roll(u_ext, n - 1, 0)[SUBLANE:SUBLANE + tm]
    row = lax.broadcasted_iota(jnp.int32, (tm, 1), 0)
    prev = jnp.where(jnp.logical_and(first, row == 0), 0.0, prev)
    nxt = jnp.where(jnp.logical_and(last, row == tm - 1), 0.0, nxt)
    return prev * w[0:1] + u_ext[SUBLANE:SUBLANE + tm] * w[1:2] + nxt * w[2:3] + b


def _kp_kernel(xm_ref, xp_ref, xn_ref, g_ref, sc_ref, sh_ref, w2_ref, b2_ref, cw_ref, cb_ref,
               w3_ref, b3_ref, qg_ref, kvg_ref, wq_ref, wqs_ref, wk_ref, wv_ref, vb_ref, e2_ref,
               cosq_ref, sinq_ref, cosk_ref, sink_ref, *rest, tm, tiles_per_seq):
    vh_ref, x1_ref, m2_ref, q_ref, k_ref, v_ref = rest[-6:]
    i = pl.program_id(0)
    first = (i % tiles_per_seq) == 0
    last = (i % tiles_per_seq) == tiles_per_seq - 1
    x_ext = jnp.concatenate([xp_ref[...], xm_ref[...], xn_ref[...]], axis=0)
    hb_ext = _mod_norm(x_ext, g_ref[...], sc_ref[...], sh_ref[...]).astype(BF16)

    p2 = jnp.dot(hb_ext, w2_ref[...], preferred_element_type=F32) + b2_ref[...]
    u = _dwconv3_ext(p2[:, 0:3 * W_HYENA], cw_ref[...], cb_ref[...], first, last, tm)
    zh = p2[SUBLANE:SUBLANE + tm, 3 * W_HYENA:4 * W_HYENA]
    vh_ref[...] = u[:, 0:W_HYENA].astype(vh_ref.dtype)
    x1_ref[...] = u[:, W_HYENA:2 * W_HYENA].astype(x1_ref.dtype)
    m2_ref[...] = (_silu(zh) * u[:, 2 * W_HYENA:3 * W_HYENA]).astype(m2_ref.dtype)

    p = jnp.dot(hb_ext[SUBLANE:SUBLANE + tm], w3_ref[...], preferred_element_type=F32) + b3_ref[...]
    cq = _rms(p[:, 0:Q_LORA], qg_ref[...]).astype(BF16)
    ckv = _rms(p[:, Q_LORA:Q_LORA + KV_LORA], kvg_ref[...]).astype(BF16)
    kr = p[:, 640:640 + QK_ROPE]
    krs = p[:, 768:768 + QK_ROPE]
    qa = jnp.dot(cq, wq_ref[...], preferred_element_type=F32)
    qb = jnp.dot(cq, wqs_ref[...], preferred_element_type=F32)
    q_ref[...] = (qa * cosq_ref[...] + qb * sinq_ref[...]).astype(q_ref.dtype)
    kr_rot = (kr * cosk_ref[...] + krs * sink_ref[...]).astype(BF16)
    kn = jnp.dot(ckv, wk_ref[...], preferred_element_type=F32)
    krp = jnp.dot(kr_rot, e2_ref[...], preferred_element_type=F32)
    k_ref[...] = (kn + krp).astype(k_ref.dtype)
    v_ref[...] = (jnp.dot(ckv, wv_ref[...], preferred_element_type=F32) + vb_ref[...]).astype(v_ref.dtype)


def _mod_spec(mod, tiles_per_seq):
    d = mod.shape[-1]
    if mod.shape[0] == 1:
        return pl.BlockSpec((None, 1, d), lambda i: (0, 0, 0))
    return pl.BlockSpec((None, 1, d), lambda i: (i // tiles_per_seq, 0, 0))


def _full_spec(a):
    nd = a.ndim
    return pl.BlockSpec(a.shape, lambda i: (0,) * nd)


def _halo_specs(n_rows, tm, d):
    r = tm // SUBLANE
    nb = n_rows // SUBLANE
    return [pl.BlockSpec((tm, d), lambda i: (i, 0)),
            pl.BlockSpec((SUBLANE, d), lambda i: (jnp.maximum(i * r - 1, 0), 0)),
            pl.BlockSpec((SUBLANE, d), lambda i: (jnp.minimum((i + 1) * r, nb - 1), 0))]


def _kp_call(x2d, bsz, seq_len, kv_len, kv_row0, mods, g, lw, cw, cb, tabs, kv_bufs=None):
    n_rows, d = x2d.shape
    tm = min(512, seq_len)
    tiles_per_seq = seq_len // tm
    blk0 = kv_row0 // tm
    assert kv_row0 % tm == 0
    scale, shift = mods
    g2 = g.reshape(1, 1, d)
    cosq, sinq, cosk, sink = tabs
    consts = [lw['w2'], lw['b2'], cw, cb, lw['w3'], lw['b3'], lw['qg'], lw['kvg'], lw['wq'], lw['wqs'],
              lw['wk'], lw['wv'], lw['vb'], lw['e2']]

    def tab_spec(t):
        return pl.BlockSpec((tm, t.shape[1]), lambda i: (i % tiles_per_seq, 0))

    in_specs = _halo_specs(n_rows, tm, d) + [
        _mod_spec(g2, tiles_per_seq), _mod_spec(scale, tiles_per_seq), _mod_spec(shift, tiles_per_seq)]
    in_specs += [_full_spec(a) for a in consts]
    in_specs += [tab_spec(t) for t in (cosq, sinq, cosk, sink)]
    args = [x2d, x2d, x2d, g2, scale, shift, *consts, cosq, sinq, cosk, sink]
    hq = N_HEADS * HEAD_PAD
    tok = lambda wd: pl.BlockSpec((tm, wd), lambda i: (i, 0))
    tok_shape = lambda wd: jax.ShapeDtypeStruct((n_rows, wd), BF16)
    kv_spec = pl.BlockSpec((None, tm, hq), lambda i: (i // tiles_per_seq, blk0 + i % tiles_per_seq, 0))
    kv_shape = jax.ShapeDtypeStruct((bsz, kv_len, hq), BF16)
    aliases = {}
    if kv_bufs is not None:
        in_specs += [pl.BlockSpec(memory_space=pl.ANY)] * 2
        aliases = {len(args): 4, len(args) + 1: 5}
        args += list(kv_bufs)
    return pl.pallas_call(
        functools.partial(_kp_kernel, tm=tm, tiles_per_seq=tiles_per_seq),
        grid=(n_rows // tm,),
        in_specs=in_specs,
        out_specs=[tok(W_HYENA), tok(W_HYENA), tok(W_HYENA), tok(hq), kv_spec, kv_spec],
        out_shape=[tok_shape(W_HYENA), tok_shape(W_HYENA), tok_shape(W_HYENA), tok_shape(hq),
                   kv_shape, kv_shape],
        input_output_aliases=aliases,
        compiler_params=_cparams(("parallel",)),
        name="kp_hyena_mla" if kv_bufs is None else "kp_hyena_mla_ctx",
    )(*args)


ATTN_CHAIN_ROWS = 512
ATTN_RING = 3


def _attn_kernel(q_ref, k_ref, v_ref, o_ref, s_ref, p_ref, m_ref, acc_ref):
    h = pl.program_id(2)
    ki = pl.program_id(3)

    @pl.when(ki == 0)
    def _():
        m_ref[...] = jnp.full(m_ref.shape, -jnp.inf, F32)
        acc_ref[...] = jnp.zeros(acc_ref.shape, F32)

    cr = s_ref.shape[1]
    n_chains = q_ref.shape[0] // cr
    chains = [pl.ds(c * cr, cr) for c in range(n_chains)]
    reps = s_ref.shape[2] // LANE
    alphas = {}
    for step in range(n_chains + 2):
        if step < n_chains:
            s_ref[step % ATTN_RING] = lax.dot_general(q_ref[chains[step], :], k_ref[...],
                                                      (((1,), (1,)), ((), ())),
                                                      preferred_element_type=F32)
        c = step - 1
        if 0 <= c < n_chains:
            s = s_ref.at[c % ATTN_RING]
            m_prev = m_ref[chains[c], :]
            m_new = jnp.maximum(m_prev, jnp.max(s[...], axis=-1, keepdims=True))
            alphas[c] = jnp.exp2(m_prev - m_new)
            m_ref[chains[c], :] = m_new
            p_ref[c % ATTN_RING] = jnp.exp2((s[...] - jnp.tile(m_new, (1, reps))).astype(BF16))
        c = step - 2
        if 0 <= c < n_chains:
            acc_ref[chains[c], :] = acc_ref[chains[c], :] * alphas.pop(c) + jnp.dot(
                p_ref[c % ATTN_RING], v_ref[...], preferred_element_type=F32)

    @pl.when(ki == pl.num_programs(3) - 1)
    def _():
        acc = acc_ref[...]
        o = (acc[:, 0:V_HEAD] / acc[:, V_HEAD:V_HEAD + 1]).astype(o_ref.dtype)

        @pl.when(h % 2 == 0)
        def _():
            o_ref[:, 0:V_HEAD] = o

        @pl.when(h % 2 == 1)
        def _():
            o_ref[:, V_HEAD:2 * V_HEAD] = o


MXU_DIM = 256
Q_TILE = 8192
KV_TILE_CAP = 2816


def _kv_tile(lk):
    for step in (MXU_DIM, LANE):
        cands = [t for t in range(step, min(lk, KV_TILE_CAP) + 1, step) if lk % t == 0]
        if cands:
            return cands[-1]
    return lk


def _attention(q, k, v, kv_row0=0, kv_len=None):
    b, lq, _ = q.shape
    lk = k.shape[1] if kv_len is None else kv_len
    tq = min(Q_TILE, lq)
    cr = min(ATTN_CHAIN_ROWS, tq)
    tk = _kv_tile(lk)
    assert kv_row0 % tk == 0
    kb0 = kv_row0 // tk
    grid = (b, lq // tq, N_HEADS, lk // tk)
    return pl.pallas_call(
        _attn_kernel,
        grid=grid,
        in_specs=[pl.BlockSpec((None, tq, HEAD_PAD), lambda bi, qi, h, ki: (bi, qi, h)),
                  pl.BlockSpec((None, tk, HEAD_PAD), lambda bi, qi, h, ki: (bi, kb0 + ki, h)),
                  pl.BlockSpec((None, tk, HEAD_PAD), lambda bi, qi, h, ki: (bi, kb0 + ki, h))],
        out_specs=pl.BlockSpec((None, tq, 2 * V_HEAD), lambda bi, qi, h, ki: (bi, qi, h // 2)),
        out_shape=jax.ShapeDtypeStruct((b, lq, W_MLA), BF16),
        scratch_shapes=[pltpu.VMEM((ATTN_RING, cr, tk), F32), pltpu.VMEM((ATTN_RING, cr, tk), BF16),
                        pltpu.VMEM((tq, LANE), F32), pltpu.VMEM((tq, HEAD_PAD), F32)],
        compiler_params=_cparams(("parallel", "parallel", "arbitrary", "arbitrary")),
        name="mla_attention",
    )(q, k, v)


COL_BWD = HYENA_EMB
COL_DROP = HYENA_EMB + 1


def _dot_bf16x3(a, b):
    a_hi = a.astype(BF16)
    b_hi = b.astype(BF16)
    a_lo = (a - a_hi.astype(F32)).astype(BF16)
    b_lo = (b - b_hi.astype(F32)).astype(BF16)
    dot = functools.partial(jnp.dot, preferred_element_type=F32)
    return dot(a_hi, b_hi) + (dot(a_lo, b_hi) + dot(a_hi, b_lo))


def _filt_kernel(z_ref, w1a_ref, w1b_ref, b1_ref, w2_ref, b2_ref, fr_ref, w3a_ref, w3b_ref, dl_ref,
                 k_ref, s_ref):
    i = pl.program_id(1)
    hp = lax.Precision.HIGHEST
    wc = HYENA_ORDER * W_HYENA
    z = z_ref[...]
    half = z.shape[0] // 2
    fr = fr_ref[...]
    pre = _dot_bf16x3(z[:half], w1a_ref[...]) + _dot_bf16x3(z[half:], w1b_ref[...])
    hid = jnp.sin(fr * (pre + b1_ref[...]))
    hid = jnp.sin(fr * (jnp.dot(hid, w2_ref[...], preferred_element_type=F32, precision=hp) + b2_ref[...]))
    hb = hid.astype(BF16)
    h2 = jnp.concatenate([jnp.dot(hb, w3a_ref[...], preferred_element_type=F32),
                          jnp.dot(hb, w3b_ref[...], preferred_element_type=F32)], axis=0)
    h = jnp.where(z[:, COL_BWD:COL_BWD + 1] > 0.5, h2[:, wc:], h2[:, :wc])
    t = z[:, 0:1]
    h = h * jnp.exp(-t * dl_ref[...])

    @pl.when(i == 0)
    def _():
        s_ref[...] = jnp.zeros(s_ref.shape, F32)

    s_ref[...] += jnp.sum(jnp.abs(h), axis=0, keepdims=True)
    k_ref[...] = jnp.where(z[:, COL_DROP:COL_DROP + 1] > 0.5, 0.0, h)


def _filter_tables(seq_len, permuted):
    t = jnp.linspace(0.0, 1.0, seq_len, dtype=F32)[:, None]
    w = 2.0 * math.pi * jnp.arange(seq_len, dtype=F32)[:, None] / seq_len
    f = jnp.linspace(1e-4, HYENA_BANDS - 1, HYENA_BANDS, dtype=F32)[None, :]
    z = jnp.concatenate([t, jnp.cos(f * w), -jnp.sin(f * w)], axis=-1)
    z2 = jnp.concatenate([z, z[0:1], jnp.flip(z[1:], axis=0)], axis=0)
    n = jnp.arange(2 * seq_len)
    flags = jnp.stack([n >= seq_len, n == seq_len], axis=-1).astype(F32)
    zz = jnp.concatenate([z2, flags], axis=-1)
    zz = jnp.pad(zz, ((0, 0), (0, LANE - zz.shape[1])))
    if permuted:
        zz = zz.reshape(2 * seq_len // N2, N2, LANE).transpose(1, 0, 2).reshape(2 * seq_len, LANE)
    return zz


def _hyena_deltas():
    max_decay = math.log(HYENA_TARGET) / HYENA_FAST_DECAY
    min_decay = math.log(HYENA_TARGET) / HYENA_SLOW_DECAY
    deltas = jnp.abs(jnp.linspace(min_decay, max_decay, W_HYENA, dtype=F32))
    return jnp.tile(deltas, HYENA_ORDER)[None, :]


def _filter_gen(seq_len, n_layers, permuted, hy_w1, hy_b1, hy_w2, hy_b2, hy_w3, hy_freq):
    zz = _filter_tables(seq_len, permuted)
    tmf = min(512, seq_len)
    tiles_half = seq_len // tmf
    wc = HYENA_ORDER * W_HYENA
    hh = HYENA_HID
    zpad = lambda a, lo, hi, ax: jnp.pad(a, [(lo, hi) if d == ax else (0, 0) for d in range(a.ndim)])
    w1p = jnp.pad(hy_w1[:n_layers], ((0, 0), (0, LANE - HYENA_EMB), (0, 0)))
    w1a, w1b = zpad(w1p, 0, hh, 2), zpad(w1p, hh, 0, 2)
    w2 = hy_w2[:n_layers]
    w2bd = jnp.concatenate([zpad(w2, 0, hh, 2), zpad(w2, hh, 0, 2)], axis=1)
    w3 = hy_w3[:n_layers].astype(BF16)
    w3a, w3b = zpad(w3, 0, hh, 1), zpad(w3, hh, 0, 1)
    r1 = lambda a: jnp.tile(a[:n_layers].reshape(n_layers, 1, hh), (1, 1, 2))
    per_layer = lambda r, c: pl.BlockSpec((None, r, c), lambda l, i: (l, 0, 0))
    return pl.pallas_call(
        _filt_kernel,
        grid=(n_layers, 2 * tiles_half),
        in_specs=[pl.BlockSpec((tmf, LANE), lambda l, i: (i, 0)),
                  per_layer(LANE, LANE), per_layer(LANE, LANE), per_layer(1, LANE),
                  per_layer(LANE, LANE), per_layer(1, LANE), per_layer(1, LANE),
                  per_layer(LANE, 2 * wc), per_layer(LANE, 2 * wc),
                  pl.BlockSpec((1, wc), lambda l, i: (0, 0))],
        out_specs=[pl.BlockSpec((tmf, wc), lambda l, i: (i, l)),
                   pl.BlockSpec((None, 1, wc), lambda l, i: (l, 0, 0))],
        out_shape=[jax.ShapeDtypeStruct((2 * seq_len, n_layers * wc), F32),
                   jax.ShapeDtypeStruct((n_layers, 1, wc), F32)],
        compiler_params=_cparams(("parallel", "arbitrary")),
        name="hyena_filter",
    )(zz, w1a, w1b, r1(hy_b1), w2bd, r1(hy_b2), r1(hy_freq), w3a, w3b, _hyena_deltas())


def _angles(a, b, n):
    m = (a * b) % n
    return m.astype(F32) * (2.0 * math.pi / n)


def _dft_tables(nh):
    n1 = 2 * nh
    n = n1 * N2
    k1 = jnp.arange(n1, dtype=jnp.int32)
    th = _angles(k1[:, None], jnp.arange(nh, dtype=jnp.int32)[None, :], n1)
    c, s = jnp.cos(th), jnp.sin(th)
    w1_pair = jnp.concatenate([jnp.concatenate([c, s], 1), jnp.concatenate([-s, c], 1)], 0)
    thf = _angles(k1[:, None], k1[None, :], n1)
    w1_real = jnp.concatenate([jnp.cos(thf), -jnp.sin(thf)], 0)
    g1 = jnp.concatenate([jnp.concatenate([c.T, -s.T], 1), jnp.concatenate([s.T, c.T], 1)], 0) / n
    n2 = jnp.arange(N2, dtype=jnp.int32)
    al = _angles(k1[:, None], n2[None, :], n)
    be = _angles(n2[:, None], n2[None, :], N2)
    ca, sa = jnp.cos(al)[:, None, :], jnp.sin(al)[:, None, :]
    cb, sb = jnp.cos(be)[None], jnp.sin(be)[None]
    cp, sp = ca * cb - sa * sb, sa * cb + ca * sb
    f2 = jnp.concatenate([jnp.concatenate([cp, sp], 2), jnp.concatenate([-sp, cp], 2)], 1)
    cpt, spt = cp.transpose(0, 2, 1), sp.transpose(0, 2, 1)
    g2 = jnp.concatenate([jnp.concatenate([cpt, -spt], 2), jnp.concatenate([spt, cpt], 2)], 1)
    g1 = g1.reshape(2 * nh, 2, n1).transpose(0, 2, 1).reshape(2 * nh, 2 * n1)
    f2 = f2.reshape(n1, 2 * N2, 2, N2).transpose(0, 1, 3, 2).reshape(n1, 2 * N2, 2 * N2)
    g2 = g2.reshape(n1, 2, N2, 2 * N2).transpose(0, 2, 1, 3).reshape(n1, 2 * N2, 2 * N2)
    w1_pair = w1_pair.reshape(2, n1, 2 * nh).transpose(1, 0, 2).reshape(2 * n1, 2 * nh)
    w1_real = w1_real.reshape(2, n1, n1).transpose(1, 0, 2).reshape(2 * n1, n1)
    return dict(w1_pair=w1_pair.astype(BF16), w1_real=w1_real.astype(BF16), g1=g1.astype(BF16),
                f2=f2.astype(BF16), g2=g2.astype(BF16))


def _strided_rows(ref, start, size):
    parts = [ref[cc, pl.ds(start, size, stride=SUBLANE), :] for cc in range(ref.shape[0])]
    return parts[0] if len(parts) == 1 else jnp.concatenate(parts, axis=1)


def _pack_pairs(x):
    return pltpu.bitcast(x.astype(BF16), jnp.uint32)


def _unpack_pairs(packed):
    return pltpu.bitcast(packed, BF16)


def _store_chunks(ref, row0, val):
    for cc in range(ref.shape[0]):
        ref[cc, pl.ds(row0, SUBLANE), :] = val[:, cc * LANE:(cc + 1) * LANE]


def _stage_a_kernel(*refs, n2t, n1, has_inv, has_fwd, has_div):
    it = iter(refs)
    y_ref = next(it) if has_inv else None
    g1_ref = next(it) if has_inv else None
    u_ref = next(it)
    mul_ref = next(it) if has_inv else None
    skip_ref = next(it) if has_inv else None
    div_ref = next(it) if has_div else None
    w1_ref = next(it) if has_fwd else None
    e_ref = next(it) if has_inv else None
    a_ref = next(it) if has_fwd else None
    yy_ref = next(it) if has_inv else None
    if has_inv:
        for j in range(n2t):
            nb, jl = j // SUBLANE, j % SUBLANE
            rhs = _unpack_pairs(_strided_rows(y_ref.at[nb], jl, n1))
            yy_ref[j] = jnp.dot(g1_ref[...], rhs, preferred_element_type=F32)
        for j in range(n2t):
            e = mul_ref[j].astype(F32) * (yy_ref[j] + u_ref[j].astype(F32) * skip_ref[...])
            e_ref[j] = e.astype(e_ref.dtype)
    if has_fwd:
        for j in range(n2t):
            if has_inv:
                src = e_ref[j]
            elif has_div:
                src = (u_ref[j] / div_ref[...]).astype(BF16)
            else:
                src = u_ref[j].astype(BF16)
            a = jnp.dot(w1_ref[...], src, preferred_element_type=F32)
            packed = _pack_pairs(a)
            for kt in range(n1 // SUBLANE):
                _store_chunks(a_ref.at[kt], j * SUBLANE, packed[kt * SUBLANE:(kt + 1) * SUBLANE])


def _stage_a(tabs, nh, u, *, y=None, mul=None, skip=None, div=None, w1=None, want_fwd=True,
             ct=256, n2t=32, e_dtype=BF16, c_off=0):
    p, _, rows, c = u.shape
    n1 = 2 * nh
    has_inv = y is not None
    has_div = div is not None
    ct = min(ct, c)
    nbk = n2t // SUBLANE
    grid = (p, c // ct, N2 // n2t)
    args, specs = [], []
    if has_inv:
        args += [y, tabs['g1']]
        specs += [pl.BlockSpec((None, nbk, ct // LANE, n1 * SUBLANE, LANE),
                               lambda pi, ci, ni: (pi, ni, ci, 0, 0)),
                  pl.BlockSpec(tabs['g1'].shape, lambda pi, ci, ni: (0, 0))]
    args.append(u)
    specs.append(pl.BlockSpec((None, n2t, rows, ct), lambda pi, ci, ni: (pi, ni, 0, ci)))
    if has_inv:
        args += [mul, skip]
        specs += [pl.BlockSpec((None, n2t, rows, ct), lambda pi, ci, ni: (pi, ni, 0, ci)),
                  pl.BlockSpec((1, ct), lambda pi, ci, ni: (0, ci + c_off // ct))]
    if has_div:
        args.append(div)
        specs.append(pl.BlockSpec((1, ct), lambda pi, ci, ni: (0, ci)))
    if want_fwd:
        args.append(w1)
        specs.append(pl.BlockSpec(w1.shape, lambda pi, ci, ni: (0, 0)))
    out_shape, out_specs = [], []
    if has_inv:
        out_shape.append(jax.ShapeDtypeStruct(u.shape, e_dtype))
        out_specs.append(pl.BlockSpec((None, n2t, rows, ct), lambda pi, ci, ni: (pi, ni, 0, ci)))
    if want_fwd:
        out_shape.append(jax.ShapeDtypeStruct((p, n1 // SUBLANE, c // LANE, N2 * SUBLANE, LANE), jnp.uint32))
        out_specs.append(pl.BlockSpec((None, n1 // SUBLANE, ct // LANE, n2t * SUBLANE, LANE),
                                      lambda pi, ci, ni: (pi, 0, ci, ni, 0)))
    outs = pl.pallas_call(
        functools.partial(_stage_a_kernel, n2t=n2t, n1=n1, has_inv=has_inv, has_fwd=want_fwd,
                          has_div=has_div),
        grid=grid, in_specs=specs, out_specs=out_specs, out_shape=out_shape,
        scratch_shapes=[pltpu.VMEM((n2t, rows, ct), F32)] if has_inv else [],
        compiler_params=_cparams(("parallel", "parallel", "parallel")),
        name="hyena_stage_a" + ("_inv" if has_inv else "") + ("_fwd" if want_fwd else ""),
    )(*args)
    return outs


def _stage_b_kernel(*refs, filt_only):
    if filt_only:
        a_ref, f2_ref, o_ref = refs
        for kl in range(SUBLANE):
            rhs = _unpack_pairs(_strided_rows(a_ref, kl, N2))
            o_ref[kl] = jnp.dot(f2_ref[kl], rhs, preferred_element_type=F32).astype(o_ref.dtype)
        return
    a_ref, f2_ref, g2_ref, kf_ref, o_ref, t_ref, z_ref = refs
    for kl in range(SUBLANE):
        rhs = _unpack_pairs(_strided_rows(a_ref, kl, N2))
        t_ref[kl] = jnp.dot(f2_ref[kl], rhs, preferred_element_type=F32)
    for kl in range(SUBLANE):
        kf = kf_ref[kl].astype(F32)
        tre, tim = t_ref[kl, :N2], t_ref[kl, N2:]
        kre, kim = kf[:N2], kf[N2:]
        z_ref[kl, :N2] = (tre * kre - tim * kim).astype(BF16)
        z_ref[kl, N2:] = (tre * kim + tim * kre).astype(BF16)
    for kl in range(SUBLANE):
        y = jnp.dot(g2_ref[kl], z_ref[kl], preferred_element_type=F32)
        packed = _pack_pairs(y)
        for nt in range(N2 // SUBLANE):
            _store_chunks(o_ref.at[nt], kl * SUBLANE, packed[nt * SUBLANE:(nt + 1) * SUBLANE])


def _stage_b(tabs, a, kf=None, *, c_off=0, ct=512):
    p, nkt, ncc, _, _ = a.shape
    c = ncc * LANE
    n1 = nkt * SUBLANE
    ct = min(ct, c)
    filt_only = kf is None
    grid = (nkt, c // ct, p)
    a_spec = pl.BlockSpec((None, None, ct // LANE, N2 * SUBLANE, LANE),
                          lambda kt, ci, pi: (pi, kt, ci, 0, 0))
    tab_spec = pl.BlockSpec((SUBLANE, 2 * N2, 2 * N2), lambda kt, ci, pi: (kt, 0, 0))
    if filt_only:
        return pl.pallas_call(
            functools.partial(_stage_b_kernel, filt_only=True),
            grid=grid, in_specs=[a_spec, tab_spec],
            out_specs=pl.BlockSpec((SUBLANE, 2 * N2, ct), lambda kt, ci, pi: (kt, 0, ci)),
            out_shape=jax.ShapeDtypeStruct((n1, 2 * N2, c), BF16),
            compiler_params=_cparams(("parallel", "parallel", "parallel")),
            name="hyena_stage_b_filter",
        )(a, tabs['f2'])
    return pl.pallas_call(
        functools.partial(_stage_b_kernel, filt_only=False),
        grid=grid,
        in_specs=[a_spec, tab_spec, tab_spec,
                  pl.BlockSpec((SUBLANE, 2 * N2, ct), lambda kt, ci, pi: (kt, 0, ci + c_off // ct))],
        out_specs=pl.BlockSpec((None, N2 // SUBLANE, ct // LANE, SUBLANE * SUBLANE, LANE),
                               lambda kt, ci, pi: (pi, 0, ci, kt, 0)),
        out_shape=jax.ShapeDtypeStruct((p, N2 // SUBLANE, c // LANE, n1 * SUBLANE, LANE), jnp.uint32),
        scratch_shapes=[pltpu.VMEM((SUBLANE, 2 * N2, ct), F32), pltpu.VMEM((SUBLANE, 2 * N2, ct), BF16)],
        compiler_params=_cparams(("parallel", "parallel", "parallel")),
        name="hyena_stage_b",
    )(a, tabs['f2'], tabs['g2'], kf)


def _permute_seq(a, nh):
    b, _, c = a.shape
    return a.reshape(b // 2, 2, nh, N2, c).transpose(0, 3, 1, 2, 4).reshape(b // 2, N2, 2 * nh, c)


def _unpermute_seq(a, nh):
    p, _, _, c = a.shape
    return a.reshape(p, N2, 2, nh, c).transpose(0, 2, 3, 1, 4).reshape(2 * p, nh * N2, c)


def _filter_spectrum(tabs, nh, k_un, ssum):
    n, call = k_un.shape
    kp = k_un.reshape(1, N2, 2 * nh, call)
    (a,) = _stage_a(tabs, nh, kp, div=ssum.reshape(1, call), w1=tabs['w1_real'], ct=512, n2t=16)
    return _stage_b(tabs, a)


def _hyena_long(tabs, nh, kf, layer, v, x1, m2, skip):
    vp, x1p, m2p = (_permute_seq(t, nh) for t in (v, x1, m2))
    c0 = layer * HYENA_ORDER * W_HYENA
    skip2 = skip.reshape(1, HYENA_ORDER * W_HYENA)
    (a1,) = _stage_a(tabs, nh, vp, w1=tabs['w1_pair'])
    y1 = _stage_b(tabs, a1, kf, c_off=c0)
    z, a2 = _stage_a(tabs, nh, vp, y=y1, mul=x1p, skip=skip2, w1=tabs['w1_pair'], c_off=0)
    y2 = _stage_b(tabs, a2, kf, c_off=c0 + W_HYENA)
    (gp,) = _stage_a(tabs, nh, z, y=y2, mul=m2p, skip=skip2, want_fwd=False, c_off=W_HYENA)
    return _unpermute_seq(gp, nh)


def _ctx_conv_kernel(v_ref, x1_ref, m2_ref, k_ref, s_ref, skip_ref, ff_ref, fk_ref, gi_ref, o_ref,
                     *, n):
    def conv(u, o):
        kfull = k_ref[:, o * W_HYENA:(o + 1) * W_HYENA] / s_ref[:, o * W_HYENA:(o + 1) * W_HYENA]
        kf = jnp.dot(fk_ref[...], kfull.astype(BF16), preferred_element_type=F32)
        uf = jnp.dot(ff_ref[...], u.astype(BF16), preferred_element_type=F32)
        ure, uim, kre, kim = uf[:n], uf[n:], kf[:n], kf[n:]
        z = jnp.concatenate([ure * kre - uim * kim, ure * kim + uim * kre], axis=0).astype(BF16)
        y = jnp.dot(gi_ref[...], z, preferred_element_type=F32)
        return y + u * skip_ref[o:o + 1, :]

    v = v_ref[...].astype(F32)
    z1 = x1_ref[...].astype(F32) * conv(v, 0)
    o_ref[...] = (m2_ref[...].astype(F32) * conv(z1, 1)).astype(o_ref.dtype)


def _ctx_tables(lc):
    n = 2 * lc
    k = jnp.arange(n, dtype=jnp.int32)
    ph = _angles(k[:, None], k[None, :], n)
    c, s = jnp.cos(ph), jnp.sin(ph)
    fk = jnp.concatenate([c, -s], axis=0)
    ff = fk[:, :lc]
    gi = jnp.concatenate([c[:lc], -s[:lc]], axis=1) / n
    return ff.astype(BF16), fk.astype(BF16), gi.astype(BF16)


def _ctx_hyena(ctabs, k_un, ssum, layer, v, x1, m2, skip):
    b, lc, c = v.shape
    n = 2 * lc
    ff, fk, gi = ctabs
    wc = HYENA_ORDER * W_HYENA
    tok = pl.BlockSpec((None, lc, c), lambda bi: (bi, 0, 0))
    return pl.pallas_call(
        functools.partial(_ctx_conv_kernel, n=n),
        grid=(b,),
        in_specs=[tok, tok, tok,
                  pl.BlockSpec((n, wc), lambda bi: (0, layer)),
                  pl.BlockSpec((None, 1, wc), lambda bi: (layer, 0, 0)),
                  pl.BlockSpec((HYENA_ORDER, c), lambda bi: (0, 0)),
                  _full_spec(ff), _full_spec(fk), _full_spec(gi)],
        out_specs=tok,
        out_shape=jax.ShapeDtypeStruct((b, lc, c), BF16),
        compiler_params=_cparams(("parallel",)),
        name="ctx_hyena",
    )(v, x1, m2, k_un, ssum, skip, ff, fk, gi)


def _merge_kernel(xm_ref, xp_ref, xn_ref, gh_ref, att_ref, g_ref, sc_ref, sh_ref, gt_ref,
                  w1_ref, b1_ref, cw_ref, cb_ref, wg_ref, bg_ref, sco_ref, hyo_ref, mlo_ref, wo_ref,
                  fg_ref, o_ref, *, final, tm, tiles_per_seq):
    i = pl.program_id(0)
    first = (i % tiles_per_seq) == 0
    last = (i % tiles_per_seq) == tiles_per_seq - 1
    x = xm_ref[...]
    d = x.shape[-1]
    x_ext = jnp.concatenate([xp_ref[...], x, xn_ref[...]], axis=0)
    hb_ext = _mod_norm(x_ext, g_ref[...], sc_ref[...], sh_ref[...]).astype(BF16)
    p = jnp.dot(hb_ext, w1_ref[...], preferred_element_type=F32) + b1_ref[...]
    prod = p[:, 2 * W_CONV:3 * W_CONV] * p[:, 0:W_CONV]
    conv = _dwconv3_ext(prod, cw_ref[...], cb_ref[...], first, last, tm)
    pm = p[SUBLANE:SUBLANE + tm]
    a = (_silu(pm[:, 3 * W_CONV:4 * W_CONV]) * (pm[:, W_CONV:2 * W_CONV] * conv)).astype(BF16)
    sm = _silu(pm[:, 4 * W_CONV:4 * W_CONV + W_MLA])
    hb = hb_ext[SUBLANE:SUBLANE + tm]
    gates = jax.nn.sigmoid(jnp.dot(hb, wg_ref[...], preferred_element_type=F32) + bg_ref[...])
    ya = jnp.dot(a, sco_ref[...], preferred_element_type=F32)
    yh = jnp.dot(gh_ref[...], hyo_ref[...], preferred_element_type=F32)
    am = (sm * att_ref[...].astype(F32)).astype(BF16)
    ym = jnp.dot(am, mlo_ref[...], preferred_element_type=F32)
    y = gates[:, 0:d] * ya + gates[:, d:2 * d] * yh + gates[:, 2 * d:3 * d] * ym
    o = jnp.dot(y.astype(BF16), wo_ref[...], preferred_element_type=F32)
    xn = x + gt_ref[...] * o
    if final:
        xn = _rms(xn, fg_ref[...])
    o_ref[...] = xn


def _merge_call(x2d, seq_len, mods, g, lw, cw, cb, gh, att, final_g, final):
    n_rows, d = x2d.shape
    tm = min(512, seq_len)
    tiles_per_seq = seq_len // tm
    scale, shift, gate = mods
    g2 = g.reshape(1, 1, d)
    fg = final_g.reshape(1, d)
    tok = lambda wd: pl.BlockSpec((tm, wd), lambda i: (i, 0))
    consts = [lw['w1'], lw['b1'], cw, cb, lw['wg'], lw['bg'], lw['sc_out'], lw['hy_out'], lw['mla_out'],
              lw['w_o'], fg]
    in_specs = _halo_specs(n_rows, tm, d) + [
        tok(W_HYENA), tok(W_MLA),
        _mod_spec(g2, tiles_per_seq), _mod_spec(scale, tiles_per_seq),
        _mod_spec(shift, tiles_per_seq), _mod_spec(gate, tiles_per_seq)]
    in_specs += [_full_spec(c) for c in consts]
    return pl.pallas_call(
        functools.partial(_merge_kernel, final=final, tm=tm, tiles_per_seq=tiles_per_seq),
        grid=(n_rows // tm,),
        in_specs=in_specs,
        out_specs=tok(d),
        out_shape=jax.ShapeDtypeStruct((n_rows, d), F32),
        compiler_params=_cparams(("parallel",)),
        name="merge_final" if final else "merge",
    )(x2d, x2d, x2d, gh, att, g2, scale, shift, gate, *consts)


def _rope_swap_cols(w):
    half = QK_ROPE // 2
    return jnp.concatenate([-w[..., half:], w[..., :half]], axis=-1)


def _layer_weights(i, w_in, b_in, mla_q_norm, mla_w_uq, mla_kv_norm, mla_w_ukv, sc_out, hy_out,
                   mla_out, w_o):
    wi, bi = w_in[i], b_in[i]
    d = wi.shape[0]
    lw = {}
    lw['w1'] = jnp.concatenate([wi[:, O_XIN:O_HPROJ], wi[:, O_ZM:O_GATES]], axis=1).astype(BF16)
    lw['b1'] = jnp.concatenate([bi[O_XIN:O_HPROJ], bi[O_ZM:O_GATES]])[None, :]
    lw['w2'] = wi[:, O_HPROJ:O_CQ].astype(BF16)
    lw['b2'] = bi[O_HPROJ:O_CQ][None, :]
    wkr, bkr = wi[:, O_KR:O_ZM], bi[O_KR:O_ZM]
    zw = lambda n: jnp.zeros((d, n), F32)
    zb = lambda n: jnp.zeros((n,), F32)
    lw['w3'] = jnp.concatenate([wi[:, O_CQ:O_KR], wkr, zw(96), _rope_swap_cols(wkr), zw(96)], axis=1).astype(BF16)
    lw['b3'] = jnp.concatenate([bi[O_CQ:O_KR], bkr, zb(96), _rope_swap_cols(bkr), zb(96)])[None, :]
    lw['qg'] = mla_q_norm[i][None, :]
    lw['kvg'] = mla_kv_norm[i][None, :]
    wuq = mla_w_uq[i].reshape(Q_LORA, N_HEADS, QK_NOPE + QK_ROPE)
    pad = jnp.zeros((Q_LORA, N_HEADS, HEAD_PAD - QK_NOPE - QK_ROPE), F32)
    lw['wq'] = jnp.concatenate([wuq, pad], axis=-1).reshape(Q_LORA, N_HEADS * HEAD_PAD).astype(BF16)
    zn = jnp.zeros((Q_LORA, N_HEADS, QK_NOPE), F32)
    lw['wqs'] = jnp.concatenate([zn, _rope_swap_cols(wuq[..., QK_NOPE:]), pad], axis=-1).reshape(
        Q_LORA, N_HEADS * HEAD_PAD).astype(BF16)
    wukv = mla_w_ukv[i].reshape(KV_LORA, N_HEADS, QK_NOPE + V_HEAD)
    padk = jnp.zeros((KV_LORA, N_HEADS, HEAD_PAD - QK_NOPE), F32)
    lw['wk'] = jnp.concatenate([wukv[..., :QK_NOPE], padk], axis=-1).reshape(
        KV_LORA, N_HEADS * HEAD_PAD).astype(BF16)
    padv = jnp.zeros((KV_LORA, N_HEADS, HEAD_PAD - V_HEAD), F32)
    lw['wv'] = jnp.concatenate([wukv[..., QK_NOPE:], padv], axis=-1).reshape(
        KV_LORA, N_HEADS * HEAD_PAD).astype(BF16)
    lw['vb'] = jnp.zeros((N_HEADS, HEAD_PAD), F32).at[:, V_HEAD].set(1.0).reshape(1, N_HEADS * HEAD_PAD)
    e2 = jnp.zeros((QK_ROPE, N_HEADS, HEAD_PAD), F32).at[:, :, QK_NOPE:QK_NOPE + QK_ROPE].set(
        jnp.eye(QK_ROPE, dtype=F32)[:, None, :])
    lw['e2'] = e2.reshape(QK_ROPE, N_HEADS * HEAD_PAD).astype(BF16)
    lw['wg'] = wi[:, O_GATES:].astype(BF16)
    lw['bg'] = bi[O_GATES:][None, :]
    lw['sc_out'] = sc_out[i].astype(BF16)
    lw['hy_out'] = hy_out[i].astype(BF16)
    lw['mla_out'] = mla_out[i].astype(BF16)
    lw['w_o'] = w_o[i].astype(BF16)
    return lw


def _rope_tables(seq_len, use_rope):
    scale = (QK_NOPE + QK_ROPE) ** -0.5 * math.log2(math.e)
    if use_rope:
        rows = seq_len // GRID_W
        row = jnp.broadcast_to(jnp.arange(rows, dtype=F32)[:, None], (rows, GRID_W)).reshape(seq_len)
        col = jnp.broadcast_to(jnp.arange(GRID_W, dtype=F32)[None, :], (rows, GRID_W)).reshape(seq_len)
        n_f = QK_ROPE // 4
        inv = ROPE_BASE ** (-jnp.arange(n_f, dtype=F32) / n_f)
        ang = jnp.concatenate([row[:, None] * inv, col[:, None] * inv], axis=-1)
        cos, sin = jnp.cos(ang), jnp.sin(ang)
    else:
        cos = jnp.ones((seq_len, QK_ROPE // 2), F32)
        sin = jnp.zeros((seq_len, QK_ROPE // 2), F32)
    cosk = jnp.concatenate([cos, cos], axis=-1)
    sink = jnp.concatenate([sin, sin], axis=-1)
    ones = jnp.ones((seq_len, QK_NOPE), F32)
    zeros = jnp.zeros((seq_len, QK_NOPE), F32)
    tail = jnp.zeros((seq_len, HEAD_PAD - QK_NOPE - QK_ROPE), F32)
    cosq = jnp.tile(jnp.concatenate([ones, cosk, tail], axis=-1), (1, N_HEADS)) * scale
    sinq = jnp.tile(jnp.concatenate([zeros, sink, tail], axis=-1), (1, N_HEADS)) * scale
    return cosq, sinq, cosk, sink


def kernel(x, c, ctx, c_ctx, ada_w, ada_b, norm_g, w_in, b_in, sc_conv_w, sc_conv_b, sc_out, hy_conv_w,
           hy_conv_b, hy_w1, hy_b1, hy_w2, hy_b2, hy_w3, hy_freq, hy_skip, hy_out, mla_q_norm, mla_w_uq,
           mla_kv_norm, mla_w_ukv, mla_out, w_o, final_g):
    bsz, seq, d = x.shape
    lc = ctx.shape[1]
    depth = ada_w.shape[0]
    nh = seq // N2
    assert d == D_MODEL and bsz % 2 == 0 and nh % SUBLANE == 0 and bsz <= 7

    cc = jnp.zeros((8, d), F32).at[:bsz].set(c).at[bsz].set(c_ctx)
    mods = _ada_mods(cc, ada_w, ada_b)

    tabs = _dft_tables(nh)
    k_un, ssum = _filter_gen(seq, depth, True, hy_w1, hy_b1, hy_w2, hy_b2, hy_w3, hy_freq)
    kf = _filter_spectrum(tabs, nh, k_un, ssum)
    if depth > 1:
        kc_un, sc_sum = _filter_gen(lc, depth - 1, False, hy_w1, hy_b1, hy_w2, hy_b2, hy_w3, hy_freq)
        ctabs = _ctx_tables(lc)

    rope_l = _rope_tables(seq, True)
    rope_c = _rope_tables(lc, False)

    x_lat = x.reshape(bsz * seq, d)
    x_ctx = ctx.reshape(bsz * lc, d)
    for i in range(depth):
        last = i == depth - 1
        lw = _layer_weights(i, w_in, b_in, mla_q_norm, mla_w_uq, mla_kv_norm, mla_w_ukv, sc_out,
                            hy_out, mla_out, w_o)
        m = mods[i]
        split = lambda r: tuple(r[:, None, j * d:(j + 1) * d] for j in range(3))
        shift_l, scale_l, gate_l = split(m[:bsz])
        shift_c, scale_c, gate_c = split(m[bsz:bsz + 1])
        g = norm_g[i]
        scw, scb = sc_conv_w[i], sc_conv_b[i][None, :]
        hcw, hcb = hy_conv_w[i], hy_conv_b[i][None, :]

        hq = N_HEADS * HEAD_PAD
        v_h, x1_h, m2_h, q_l, k_all, v_all = _kp_call(
            x_lat, bsz, seq, seq + lc, 0, (scale_l, shift_l), g, lw, hcw, hcb, rope_l)
        v_hc, x1_hc, m2_hc, q_c, k_all, v_all = _kp_call(
            x_ctx, bsz, lc, seq + lc, seq, (scale_c, shift_c), g, lw, hcw, hcb, rope_c,
            kv_bufs=(k_all, v_all))
        att_l = _attention(q_l.reshape(bsz, seq, hq), k_all, v_all).reshape(bsz * seq, W_MLA)
        r3 = lambda t: t.reshape(bsz, seq, W_HYENA)
        gh_l = _hyena_long(tabs, nh, kf, i, r3(v_h), r3(x1_h), r3(m2_h), hy_skip[i])
        new_lat = _merge_call(x_lat, seq, (scale_l, shift_l, gate_l), g, lw, scw, scb,
                              gh_l.reshape(bsz * seq, W_HYENA), att_l, final_g, last)
        if not last:
            att_c = _attention(q_c.reshape(bsz, lc, hq), k_all, v_all, kv_row0=seq,
                               kv_len=lc).reshape(bsz * lc, W_MLA)
            rc = lambda t: t.reshape(bsz, lc, W_HYENA)
            gh_c = _ctx_hyena(ctabs, kc_un, sc_sum, i, rc(v_hc), rc(x1_hc), rc(m2_hc), hy_skip[i])
            x_ctx = _merge_call(x_ctx, lc, (scale_c, shift_c, gate_c), g, lw, scw, scb,
                                gh_c.reshape(bsz * lc, W_HYENA), att_c, final_g, False)
        x_lat = new_lat
    return x_lat.reshape(bsz, seq, d)
```

```python
import functools
import math

import jax
import jax.numpy as jnp
from jax import lax
from jax.experimental import pallas as pl
from jax.experimental.pallas import tpu as pltpu

F32 = jnp.float32
BF16 = jnp.bfloat16

D_MODEL = 1024
GRID_W = 64
W_CONV = 512
W_HYENA = 512
HYENA_ORDER = 2
HYENA_EMB = 33
HYENA_BANDS = (HYENA_EMB - 1) // 2
HYENA_HID = 64
HYENA_FAST_DECAY = 0.3
HYENA_SLOW_DECAY = 1.5
HYENA_TARGET = 1e-2
N_HEADS = 8
QK_NOPE = 64
QK_ROPE = 32
V_HEAD = 64
Q_LORA = 384
KV_LORA = 256
W_MLA = N_HEADS * V_HEAD
ROPE_BASE = 10000.0
N_BRANCH = 3
EPS = 1e-6

O_XIN, O_GB, O_GC, O_ZA = 0, 512, 1024, 1536
O_HPROJ, O_ZH = 2048, 3584
O_CQ, O_CKV, O_KR, O_ZM, O_GATES = 4096, 4480, 4736, 4768, 5280

LANE = 128
SUBLANE = 8
HEAD_PAD = 128
N2 = 128
VMEM_CAPACITY = 64 * 1024 * 1024
VMEM_LIMIT = VMEM_CAPACITY - 8 * 1024 * 1024


def _cparams(sem):
    return pltpu.CompilerParams(dimension_semantics=sem, vmem_limit_bytes=VMEM_LIMIT)


def _ada_kernel(c_ref, w_ref, b_ref, o_ref):
    c = c_ref[...]
    s = c * jax.nn.sigmoid(c)
    o_ref[...] = jnp.dot(s, w_ref[...], preferred_element_type=F32,
                         precision=lax.Precision.HIGHEST) + b_ref[...]


def _ada_mods(cc, ada_w, ada_b):
    depth = ada_w.shape[0]
    d = cc.shape[1]
    return pl.pallas_call(
        _ada_kernel,
        grid=(depth, 3),
        in_specs=[pl.BlockSpec((8, d), lambda l, j: (0, 0)),
                  pl.BlockSpec((None, d, d), lambda l, j: (l, 0, j)),
                  pl.BlockSpec((None, 1, d), lambda l, j: (l, 0, j))],
        out_specs=pl.BlockSpec((None, 8, d), lambda l, j: (l, 0, j)),
        out_shape=jax.ShapeDtypeStruct((depth, 8, 3 * d), F32),
        compiler_params=_cparams(("parallel", "parallel")),
        name="ada_mods",
    )(cc, ada_w, ada_b.reshape(depth, 1, 3 * d))


def _mod_norm(x, g, scale, shift):
    y = x * lax.rsqrt(jnp.mean(x * x, axis=-1, keepdims=True) + EPS)
    return (y * g) * (1.0 + scale) + shift


def _rms(x, g):
    return x * lax.rsqrt(jnp.mean(x * x, axis=-1, keepdims=True) + EPS) * g


def _silu(x):
    return x * jax.nn.sigmoid(x)


def _dwconv3_ext(u_ext, w, b, first, last, tm):
    n = tm + 2 * SUBLANE
    prev = pltpu.roll(u_ext, 1, 0)[SUBLANE:SUBLANE + tm]
    nxt = pltpu.roll(u_ext, n - 1, 0)[SUBLANE:SUBLANE + tm]
    row = lax.broadcasted_iota(jnp.int32, (tm, 1), 0)
    prev = jnp.where(jnp.logical_and(first, row == 0), 0.0, prev)
    nxt = jnp.where(jnp.logical_and(last, row == tm - 1), 0.0, nxt)
    return prev * w[0:1] + u_ext[SUBLANE:SUBLANE + tm] * w[1:2] + nxt * w[2:3] + b


def _kp_kernel(xm_ref, xp_ref, xn_ref, g_ref, sc_ref, sh_ref, w2_ref, b2_ref, cw_ref, cb_ref,
               w3_ref, b3_ref, qg_ref, kvg_ref, wq_ref, wqs_ref, wk_ref, wv_ref, vb_ref, e2_ref,
               cosq_ref, sinq_ref, cosk_ref, sink_ref, *rest, tm, tiles_per_seq):
    vh_ref, x1_ref, m2_ref, q_ref, k_ref, v_ref = rest[-6:]
    i = pl.program_id(0)
    first = (i % tiles_per_seq) == 0
    last = (i % tiles_per_seq) == tiles_per_seq - 1
    x_ext = jnp.concatenate([xp_ref[...], xm_ref[...], xn_ref[...]], axis=0)
    hb_ext = _mod_norm(x_ext, g_ref[...], sc_ref[...], sh_ref[...]).astype(BF16)

    p2 = jnp.dot(hb_ext, w2_ref[...], preferred_element_type=F32) + b2_ref[...]
    u = _dwconv3_ext(p2[:, 0:3 * W_HYENA], cw_ref[...], cb_ref[...], first, last, tm)
    zh = p2[SUBLANE:SUBLANE + tm, 3 * W_HYENA:4 * W_HYENA]
    vh_ref[...] = u[:, 0:W_HYENA].astype(vh_ref.dtype)
    x1_ref[...] = u[:, W_HYENA:2 * W_HYENA].astype(x1_ref.dtype)
    m2_ref[...] = (_silu(zh) * u[:, 2 * W_HYENA:3 * W_HYENA]).astype(m2_ref.dtype)

    p = jnp.dot(hb_ext[SUBLANE:SUBLANE + tm], w3_ref[...], preferred_element_type=F32) + b3_ref[...]
    cq = _rms(p[:, 0:Q_LORA], qg_ref[...]).astype(BF16)
    ckv = _rms(p[:, Q_LORA:Q_LORA + KV_LORA], kvg_ref[...]).astype(BF16)
    kr = p[:, 640:640 + QK_ROPE]
    krs = p[:, 768:768 + QK_ROPE]
    qa = jnp.dot(cq, wq_ref[...], preferred_element_type=F32)
    qb = jnp.dot(cq, wqs_ref[...], preferred_element_type=F32)
    cosq = jnp.tile(cosq_ref[...], (1, N_HEADS))
    sinq = jnp.tile(sinq_ref[...], (1, N_HEADS))
    q_ref[...] = (qa * cosq + qb * sinq).astype(q_ref.dtype)
    kr_rot = (kr * cosk_ref[...] + krs * sink_ref[...]).astype(BF16)
    kn = jnp.dot(ckv, wk_ref[...], preferred_element_type=F32)
    krp = jnp.dot(kr_rot, e2_ref[...], preferred_element_type=F32)
    k_ref[...] = (kn + krp).astype(k_ref.dtype)
    v_ref[...] = (jnp.dot(ckv, wv_ref[...], preferred_element_type=F32) + vb_ref[...]).astype(v_ref.dtype)


def _mod_spec(mod, tiles_per_seq):
    d = mod.shape[-1]
    if mod.shape[0] == 1:
        return pl.BlockSpec((None, 1, d), lambda i: (0, 0, 0))
    return pl.BlockSpec((None, 1, d), lambda i: (i // tiles_per_seq, 0, 0))


def _full_spec(a):
    nd = a.ndim
    return pl.BlockSpec(a.shape, lambda i: (0,) * nd)


def _halo_specs(n_rows, tm, d):
    r = tm // SUBLANE
    nb = n_rows // SUBLANE
    return [pl.BlockSpec((tm, d), lambda i: (i, 0)),
            pl.BlockSpec((SUBLANE, d), lambda i: (jnp.maximum(i * r - 1, 0), 0)),
            pl.BlockSpec((SUBLANE, d), lambda i: (jnp.minimum((i + 1) * r, nb - 1), 0))]


def _kp_call(x2d, bsz, seq_len, kv_len, kv_row0, mods, g, lw, cw, cb, tabs, kv_bufs=None):
    n_rows, d = x2d.shape
    tm = min(512, seq_len)
    tiles_per_seq = seq_len // tm
    blk0 = kv_row0 // tm
    assert kv_row0 % tm == 0
    scale, shift = mods
    g2 = g.reshape(1, 1, d)
    cosq, sinq, cosk, sink = tabs
    consts = [lw['w2'], lw['b2'], cw, cb, lw['w3'], lw['b3'], lw['qg'], lw['kvg'], lw['wq'], lw['wqs'],
              lw['wk'], lw['wv'], lw['vb'], lw['e2']]

    def tab_spec(t):
        return pl.BlockSpec((tm, t.shape[1]), lambda i: (i % tiles_per_seq, 0))

    in_specs = _halo_specs(n_rows, tm, d) + [
        _mod_spec(g2, tiles_per_seq), _mod_spec(scale, tiles_per_seq), _mod_spec(shift, tiles_per_seq)]
    in_specs += [_full_spec(a) for a in consts]
    in_specs += [tab_spec(t) for t in (cosq, sinq, cosk, sink)]
    args = [x2d, x2d, x2d, g2, scale, shift, *consts, cosq, sinq, cosk, sink]
    hq = N_HEADS * HEAD_PAD
    tok = lambda wd: pl.BlockSpec((tm, wd), lambda i: (i, 0))
    tok_shape = lambda wd: jax.ShapeDtypeStruct((n_rows, wd), BF16)
    kv_spec = pl.BlockSpec((None, tm, hq), lambda i: (i // tiles_per_seq, blk0 + i % tiles_per_seq, 0))
    kv_shape = jax.ShapeDtypeStruct((bsz, kv_len, hq), BF16)
    aliases = {}
    if kv_bufs is not None:
        in_specs += [pl.BlockSpec(memory_space=pl.ANY)] * 2
        aliases = {len(args): 4, len(args) + 1: 5}
        args += list(kv_bufs)
    return pl.pallas_call(
        functools.partial(_kp_kernel, tm=tm, tiles_per_seq=tiles_per_seq),
        grid=(n_rows // tm,),
        in_specs=in_specs,
        out_specs=[tok(W_HYENA), tok(W_HYENA), tok(W_HYENA), tok(hq), kv_spec, kv_spec],
        out_shape=[tok_shape(W_HYENA), tok_shape(W_HYENA), tok_shape(W_HYENA), tok_shape(hq),
                   kv_shape, kv_shape],
        input_output_aliases=aliases,
        compiler_params=_cparams(("parallel",)),
        name="kp_hyena_mla" if kv_bufs is None else "kp_hyena_mla_ctx",
    )(*args)


ATTN_CHAIN_ROWS = 512
ATTN_RING = 3


def _attn_kernel(q_ref, k_ref, v_ref, o_ref, s_ref, p_ref, m_ref, acc_ref):
    h = pl.program_id(2)
    ki = pl.program_id(3)

    @pl.when(ki == 0)
    def _():
        m_ref[...] = jnp.full(m_ref.shape, -jnp.inf, F32)
        acc_ref[...] = jnp.zeros(acc_ref.shape, F32)

    cr = s_ref.shape[1]
    n_chains = q_ref.shape[0] // cr
    chains = [pl.ds(c * cr, cr) for c in range(n_chains)]
    reps = s_ref.shape[2] // LANE
    alphas = {}
    for step in range(n_chains + 2):
        if step < n_chains:
            s_ref[step % ATTN_RING] = lax.dot_general(q_ref[chains[step], :], k_ref[...],
                                                      (((1,), (1,)), ((), ())),
                                                      preferred_element_type=F32)
        c = step - 1
        if 0 <= c < n_chains:
            s = s_ref.at[c % ATTN_RING]
            m_prev = m_ref[chains[c], :]
            m_new = jnp.maximum(m_prev, jnp.max(s[...], axis=-1, keepdims=True))
            alphas[c] = jnp.exp2(m_prev - m_new)
            m_ref[chains[c], :] = m_new
            p_ref[c % ATTN_RING] = jnp.exp2((s[...] - jnp.tile(m_new, (1, reps))).astype(BF16))
        c = step - 2
        if 0 <= c < n_chains:
            acc_ref[chains[c], :] = acc_ref[chains[c], :] * alphas.pop(c) + jnp.dot(
                p_ref[c % ATTN_RING], v_ref[...], preferred_element_type=F32)

    @pl.when(ki == pl.num_programs(3) - 1)
    def _():
        acc = acc_ref[...]
        o = (acc[:, 0:V_HEAD] / acc[:, V_HEAD:V_HEAD + 1]).astype(o_ref.dtype)

        @pl.when(h % 2 == 0)
        def _():
            o_ref[:, 0:V_HEAD] = o

        @pl.when(h % 2 == 1)
        def _():
            o_ref[:, V_HEAD:2 * V_HEAD] = o


MXU_DIM = 256
Q_TILE = 8192
KV_TILE_CAP = 2816


def _kv_tile(lk):
    for step in (MXU_DIM, LANE):
        cands = [t for t in range(step, min(lk, KV_TILE_CAP) + 1, step) if lk % t == 0]
        if cands:
            return cands[-1]
    return lk


def _attention(q, k, v, kv_row0=0, kv_len=None):
    b, lq, _ = q.shape
    lk = k.shape[1] if kv_len is None else kv_len
    tq = min(Q_TILE, lq)
    cr = min(ATTN_CHAIN_ROWS, tq)
    tk = _kv_tile(lk)
    assert kv_row0 % tk == 0
    kb0 = kv_row0 // tk
    grid = (b, lq // tq, N_HEADS, lk // tk)
    return pl.pallas_call(
        _attn_kernel,
        grid=grid,
        in_specs=[pl.BlockSpec((None, tq, HEAD_PAD), lambda bi, qi, h, ki: (bi, qi, h)),
                  pl.BlockSpec((None, tk, HEAD_PAD), lambda bi, qi, h, ki: (bi, kb0 + ki, h)),
                  pl.BlockSpec((None, tk, HEAD_PAD), lambda bi, qi, h, ki: (bi, kb0 + ki, h))],
        out_specs=pl.BlockSpec((None, tq, 2 * V_HEAD), lambda bi, qi, h, ki: (bi, qi, h // 2)),
        out_shape=jax.ShapeDtypeStruct((b, lq, W_MLA), BF16),
        scratch_shapes=[pltpu.VMEM((ATTN_RING, cr, tk), F32), pltpu.VMEM((ATTN_RING, cr, tk), BF16),
                        pltpu.VMEM((tq, LANE), F32), pltpu.VMEM((tq, HEAD_PAD), F32)],
        compiler_params=_cparams(("parallel", "parallel", "arbitrary", "arbitrary")),
        name="mla_attention",
    )(q, k, v)


COL_BWD = HYENA_EMB
COL_DROP = HYENA_EMB + 1


def _dot_bf16x3(a, b):
    a_hi = a.astype(BF16)
    b_hi = b.astype(BF16)
    a_lo = (a - a_hi.astype(F32)).astype(BF16)
    b_lo = (b - b_hi.astype(F32)).astype(BF16)
    dot = functools.partial(jnp.dot, preferred_element_type=F32)
    return dot(a_hi, b_hi) + (dot(a_lo, b_hi) + dot(a_hi, b_lo))


def _filt_kernel(z_ref, w1a_ref, w1b_ref, b1_ref, w2_ref, b2_ref, fr_ref, w3a_ref, w3b_ref, dl_ref,
                 k_ref, s_ref):
    i = pl.program_id(1)
    hp = lax.Precision.HIGHEST
    wc = HYENA_ORDER * W_HYENA
    z = z_ref[...]
    half = z.shape[0] // 2
    fr = fr_ref[...]
    pre = _dot_bf16x3(z[:half], w1a_ref[...]) + _dot_bf16x3(z[half:], w1b_ref[...])
    hid = jnp.sin(fr * (pre + b1_ref[...]))
    hid = jnp.sin(fr * (jnp.dot(hid, w2_ref[...], preferred_element_type=F32, precision=hp) + b2_ref[...]))
    hb = hid.astype(BF16)
    h2 = jnp.concatenate([jnp.dot(hb, w3a_ref[...], preferred_element_type=F32),
                          jnp.dot(hb, w3b_ref[...], preferred_element_type=F32)], axis=0)
    h = jnp.where(z[:, COL_BWD:COL_BWD + 1] > 0.5, h2[:, wc:], h2[:, :wc])
    t = z[:, 0:1]
    h = h * jnp.exp(-t * dl_ref[...])

    @pl.when(i == 0)
    def _():
        s_ref[...] = jnp.zeros(s_ref.shape, F32)

    s_ref[...] += jnp.sum(jnp.abs(h), axis=0, keepdims=True)
    k_ref[...] = jnp.where(z[:, COL_DROP:COL_DROP + 1] > 0.5, 0.0, h)


def _filter_tables(seq_len, permuted):
    t = jnp.linspace(0.0, 1.0, seq_len, dtype=F32)[:, None]
    w = 2.0 * math.pi * jnp.arange(seq_len, dtype=F32)[:, None] / seq_len
    f = jnp.linspace(1e-4, HYENA_BANDS - 1, HYENA_BANDS, dtype=F32)[None, :]
    z = jnp.concatenate([t, jnp.cos(f * w), -jnp.sin(f * w)], axis=-1)
    z2 = jnp.concatenate([z, z[0:1], jnp.flip(z[1:], axis=0)], axis=0)
    n = jnp.arange(2 * seq_len)
    flags = jnp.stack([n >= seq_len, n == seq_len], axis=-1).astype(F32)
    zz = jnp.concatenate([z2, flags], axis=-1)
    zz = jnp.pad(zz, ((0, 0), (0, LANE - zz.shape[1])))
    if permuted:
        zz = zz.reshape(2 * seq_len // N2, N2, LANE).transpose(1, 0, 2).reshape(2 * seq_len, LANE)
    return zz


def _hyena_deltas():
    max_decay = math.log(HYENA_TARGET) / HYENA_FAST_DECAY
    min_decay = math.log(HYENA_TARGET) / HYENA_SLOW_DECAY
    deltas = jnp.abs(jnp.linspace(min_decay, max_decay, W_HYENA, dtype=F32))
    return jnp.tile(deltas, HYENA_ORDER)[None, :]


def _filter_gen(seq_len, n_layers, permuted, hy_w1, hy_b1, hy_w2, hy_b2, hy_w3, hy_freq):
    zz = _filter_tables(seq_len, permuted)
    tmf = min(512, seq_len)
    tiles_half = seq_len // tmf
    wc = HYENA_ORDER * W_HYENA
    hh = HYENA_HID
    zpad = lambda a, lo, hi, ax: jnp.pad(a, [(lo, hi) if d == ax else (0, 0) for d in range(a.ndim)])
    w1p = jnp.pad(hy_w1[:n_layers], ((0, 0), (0, LANE - HYENA_EMB), (0, 0)))
    w1a, w1b = zpad(w1p, 0, hh, 2), zpad(w1p, hh, 0, 2)
    w2 = hy_w2[:n_layers]
    w2bd = jnp.concatenate([zpad(w2, 0, hh, 2), zpad(w2, hh, 0, 2)], axis=1)
    w3 = hy_w3[:n_layers].astype(BF16)
    w3a, w3b = zpad(w3, 0, hh, 1), zpad(w3, hh, 0, 1)
    r1 = lambda a: jnp.tile(a[:n_layers].reshape(n_layers, 1, hh), (1, 1, 2))
    per_layer = lambda r, c: pl.BlockSpec((None, r, c), lambda l, i: (l, 0, 0))
    return pl.pallas_call(
        _filt_kernel,
        grid=(n_layers, 2 * tiles_half),
        in_specs=[pl.BlockSpec((tmf, LANE), lambda l, i: (i, 0)),
                  per_layer(LANE, LANE), per_layer(LANE, LANE), per_layer(1, LANE),
                  per_layer(LANE, LANE), per_layer(1, LANE), per_layer(1, LANE),
                  per_layer(LANE, 2 * wc), per_layer(LANE, 2 * wc),
                  pl.BlockSpec((1, wc), lambda l, i: (0, 0))],
        out_specs=[pl.BlockSpec((tmf, wc), lambda l, i: (i, l)),
                   pl.BlockSpec((None, 1, wc), lambda l, i: (l, 0, 0))],
        out_shape=[jax.ShapeDtypeStruct((2 * seq_len, n_layers * wc), F32),
                   jax.ShapeDtypeStruct((n_layers, 1, wc), F32)],
        compiler_params=_cparams(("parallel", "arbitrary")),
        name="hyena_filter",
    )(zz, w1a, w1b, r1(hy_b1), w2bd, r1(hy_b2), r1(hy_freq), w3a, w3b, _hyena_deltas())


def _angles(a, b, n):
    m = (a * b) % n
    return m.astype(F32) * (2.0 * math.pi / n)


def _dft_tables(nh):
    n1 = 2 * nh
    n = n1 * N2
    k1 = jnp.arange(n1, dtype=jnp.int32)
    th = _angles(k1[:, None], jnp.arange(nh, dtype=jnp.int32)[None, :], n1)
    c, s = jnp.cos(th), jnp.sin(th)
    w1_pair = jnp.concatenate([jnp.concatenate([c, s], 1), jnp.concatenate([-s, c], 1)], 0)
    thf = _angles(k1[:, None], k1[None, :], n1)
    w1_real = jnp.concatenate([jnp.cos(thf), -jnp.sin(thf)], 0)
    g1 = jnp.concatenate([jnp.concatenate([c.T, -s.T], 1), jnp.concatenate([s.T, c.T], 1)], 0) / n
    n2 = jnp.arange(N2, dtype=jnp.int32)
    al = _angles(k1[:, None], n2[None, :], n)
    be = _angles(n2[:, None], n2[None, :], N2)
    ca, sa = jnp.cos(al)[:, None, :], jnp.sin(al)[:, None, :]
    cb, sb = jnp.cos(be)[None], jnp.sin(be)[None]
    cp, sp = ca * cb - sa * sb, sa * cb + ca * sb
    f2 = jnp.concatenate([jnp.concatenate([cp, sp], 2), jnp.concatenate([-sp, cp], 2)], 1)
    cpt, spt = cp.transpose(0, 2, 1), sp.transpose(0, 2, 1)
    g2 = jnp.concatenate([jnp.concatenate([cpt, -spt], 2), jnp.concatenate([spt, cpt], 2)], 1)
    g1 = g1.reshape(2 * nh, 2, n1).transpose(0, 2, 1).reshape(2 * nh, 2 * n1)
    f2 = f2.reshape(n1, 2 * N2, 2, N2).transpose(0, 1, 3, 2).reshape(n1, 2 * N2, 2 * N2)
    g2 = g2.reshape(n1, 2, N2, 2 * N2).transpose(0, 2, 1, 3).reshape(n1, 2 * N2, 2 * N2)
    w1_pair = w1_pair.reshape(2, n1, 2 * nh).transpose(1, 0, 2).reshape(2 * n1, 2 * nh)
    w1_real = w1_real.reshape(2, n1, n1).transpose(1, 0, 2).reshape(2 * n1, n1)
    return dict(w1_pair=w1_pair.astype(BF16), w1_real=w1_real.astype(BF16), g1=g1.astype(BF16),
                f2=f2.astype(BF16), g2=g2.astype(BF16))


def _strided_rows(ref, start, size):
    parts = [ref[cc, pl.ds(start, size, stride=SUBLANE), :] for cc in range(ref.shape[0])]
    return parts[0] if len(parts) == 1 else jnp.concatenate(parts, axis=1)


def _pack_pairs(x):
    return pltpu.bitcast(x.astype(BF16), jnp.uint32)


def _unpack_pairs(packed):
    return pltpu.bitcast(packed, BF16)


def _store_chunks(ref, row0, val):
    for cc in range(ref.shape[0]):
        ref[cc, pl.ds(row0, SUBLANE), :] = val[:, cc * LANE:(cc + 1) * LANE]


def _stage_a_kernel(*refs, n2t, n1, has_inv, has_fwd, has_div):
    it = iter(refs)
    y_ref = next(it) if has_inv else None
    g1_ref = next(it) if has_inv else None
    u_ref = next(it)
    mul_ref = next(it) if has_inv else None
    skip_ref = next(it) if has_inv else None
    div_ref = next(it) if has_div else None
    w1_ref = next(it) if has_fwd else None
    e_ref = next(it) if has_inv else None
    a_ref = next(it) if has_fwd else None
    yy_ref = next(it) if has_inv else None
    if has_inv:
        for j in range(n2t):
            nb, jl = j // SUBLANE, j % SUBLANE
            rhs = _unpack_pairs(_strided_rows(y_ref.at[nb], jl, n1))
            yy_ref[j] = jnp.dot(g1_ref[...], rhs, preferred_element_type=F32)
        for j in range(n2t):
            e = mul_ref[j].astype(F32) * (yy_ref[j] + u_ref[j].astype(F32) * skip_ref[...])
            e_ref[j] = e.astype(e_ref.dtype)
    if has_fwd:
        for j in range(n2t):
            if has_inv:
                src = e_ref[j]
            elif has_div:
                src = (u_ref[j] / div_ref[...]).astype(BF16)
            else:
                src = u_ref[j].astype(BF16)
            a = jnp.dot(w1_ref[...], src, preferred_element_type=F32)
            packed = _pack_pairs(a)
            for kt in range(n1 // SUBLANE):
                _store_chunks(a_ref.at[kt], j * SUBLANE, packed[kt * SUBLANE:(kt + 1) * SUBLANE])


def _stage_a(tabs, nh, u, *, y=None, mul=None, skip=None, div=None, w1=None, want_fwd=True,
             ct=256, n2t=32, e_dtype=BF16, c_off=0):
    p, _, rows, c = u.shape
    n1 = 2 * nh
    has_inv = y is not None
    has_div = div is not None
    ct = min(ct, c)
    nbk = n2t // SUBLANE
    grid = (p, c // ct, N2 // n2t)
    args, specs = [], []
    if has_inv:
        args += [y, tabs['g1']]
        specs += [pl.BlockSpec((None, nbk, ct // LANE, n1 * SUBLANE, LANE),
                               lambda pi, ci, ni: (pi, ni, ci, 0, 0)),
                  pl.BlockSpec(tabs['g1'].shape, lambda pi, ci, ni: (0, 0))]
    args.append(u)
    specs.append(pl.BlockSpec((None, n2t, rows, ct), lambda pi, ci, ni: (pi, ni, 0, ci)))
    if has_inv:
        args += [mul, skip]
        specs += [pl.BlockSpec((None, n2t, rows, ct), lambda pi, ci, ni: (pi, ni, 0, ci)),
                  pl.BlockSpec((1, ct), lambda pi, ci, ni: (0, ci + c_off // ct))]
    if has_div:
        args.append(div)
        specs.append(pl.BlockSpec((1, ct), lambda pi, ci, ni: (0, ci)))
    if want_fwd:
        args.append(w1)
        specs.append(pl.BlockSpec(w1.shape, lambda pi, ci, ni: (0, 0)))
    out_shape, out_specs = [], []
    if has_inv:
        out_shape.append(jax.ShapeDtypeStruct(u.shape, e_dtype))
        out_specs.append(pl.BlockSpec((None, n2t, rows, ct), lambda pi, ci, ni: (pi, ni, 0, ci)))
    if want_fwd:
        out_shape.append(jax.ShapeDtypeStruct((p, n1 // SUBLANE, c // LANE, N2 * SUBLANE, LANE), jnp.uint32))
        out_specs.append(pl.BlockSpec((None, n1 // SUBLANE, ct // LANE, n2t * SUBLANE, LANE),
                                      lambda pi, ci, ni: (pi, 0, ci, ni, 0)))
    outs = pl.pallas_call(
        functools.partial(_stage_a_kernel, n2t=n2t, n1=n1, has_inv=has_inv, has_fwd=want_fwd,
                          has_div=has_div),
        grid=grid, in_specs=specs, out_specs=out_specs, out_shape=out_shape,
        scratch_shapes=[pltpu.VMEM((n2t, rows, ct), F32)] if has_inv else [],
        compiler_params=_cparams(("parallel", "parallel", "parallel")),
        name="hyena_stage_a" + ("_inv" if has_inv else "") + ("_fwd" if want_fwd else ""),
    )(*args)
    return outs


def _stage_b_kernel(*refs, filt_only):
    if filt_only:
        a_ref, f2_ref, o_ref = refs
        for kl in range(SUBLANE):
            rhs = _unpack_pairs(_strided_rows(a_ref, kl, N2))
            o_ref[kl] = jnp.dot(f2_ref[kl], rhs, preferred_element_type=F32).astype(o_ref.dtype)
        return
    a_ref, f2_ref, g2_ref, kf_ref, o_ref, t_ref, z_ref = refs
    for kl in range(SUBLANE):
        rhs = _unpack_pairs(_strided_rows(a_ref, kl, N2))
        t_ref[kl] = jnp.dot(f2_ref[kl], rhs, preferred_element_type=F32)
    for kl in range(SUBLANE):
        kf = kf_ref[kl].astype(F32)
        tre, tim = t_ref[kl, :N2], t_ref[kl, N2:]
        kre, kim = kf[:N2], kf[N2:]
        z_ref[kl, :N2] = (tre * kre - tim * kim).astype(BF16)
        z_ref[kl, N2:] = (tre * kim + tim * kre).astype(BF16)
    for kl in range(SUBLANE):
        y = jnp.dot(g2_ref[kl], z_ref[kl], preferred_element_type=F32)
        packed = _pack_pairs(y)
        for nt in range(N2 // SUBLANE):
            _store_chunks(o_ref.at[nt], kl * SUBLANE, packed[nt * SUBLANE:(nt + 1) * SUBLANE])


def _stage_b(tabs, a, kf=None, *, c_off=0, ct=512):
    p, nkt, ncc, _, _ = a.shape
    c = ncc * LANE
    n1 = nkt * SUBLANE
    ct = min(ct, c)
    filt_only = kf is None
    grid = (nkt, c // ct, p)
    a_spec = pl.BlockSpec((None, None, ct // LANE, N2 * SUBLANE, LANE),
                          lambda kt, ci, pi: (pi, kt, ci, 0, 0))
    tab_spec = pl.BlockSpec((SUBLANE, 2 * N2, 2 * N2), lambda kt, ci, pi: (kt, 0, 0))
    if filt_only:
        return pl.pallas_call(
            functools.partial(_stage_b_kernel, filt_only=True),
            grid=grid, in_specs=[a_spec, tab_spec],
            out_specs=pl.BlockSpec((SUBLANE, 2 * N2, ct), lambda kt, ci, pi: (kt, 0, ci)),
            out_shape=jax.ShapeDtypeStruct((n1, 2 * N2, c), BF16),
            compiler_params=_cparams(("parallel", "parallel", "parallel")),
            name="hyena_stage_b_filter",
        )(a, tabs['f2'])
    return pl.pallas_call(
        functools.partial(_stage_b_kernel, filt_only=False),
        grid=grid,
        in_specs=[a_spec, tab_spec, tab_spec,
                  pl.BlockSpec((SUBLANE, 2 * N2, ct), lambda kt, ci, pi: (kt, 0, ci + c_off // ct))],
        out_specs=pl.BlockSpec((None, N2 // SUBLANE, ct // LANE, SUBLANE * SUBLANE, LANE),
                               lambda kt, ci, pi: (pi, 0, ci, kt, 0)),
        out_shape=jax.ShapeDtypeStruct((p, N2 // SUBLANE, c // LANE, n1 * SUBLANE, LANE), jnp.uint32),
        scratch_shapes=[pltpu.VMEM((SUBLANE, 2 * N2, ct), F32), pltpu.VMEM((SUBLANE, 2 * N2, ct), BF16)],
        compiler_params=_cparams(("parallel", "parallel", "parallel")),
        name="hyena_stage_b",
    )(a, tabs['f2'], tabs['g2'], kf)


def _permute_seq(a, nh):
    b, _, c = a.shape
    return a.reshape(b // 2, 2, nh, N2, c).transpose(0, 3, 1, 2, 4).reshape(b // 2, N2, 2 * nh, c)


def _unpermute_seq(a, nh):
    p, _, _, c = a.shape
    return a.reshape(p, N2, 2, nh, c).transpose(0, 2, 3, 1, 4).reshape(2 * p, nh * N2, c)


def _filter_spectrum(tabs, nh, k_un, ssum):
    n, call = k_un.shape
    kp = k_un.reshape(1, N2, 2 * nh, call)
    (a,) = _stage_a(tabs, nh, kp, div=ssum.reshape(1, call), w1=tabs['w1_real'], ct=512, n2t=16)
    return _stage_b(tabs, a)


def _hyena_long(tabs, nh, kf, layer, v, x1, m2, skip):
    vp, x1p, m2p = (_permute_seq(t, nh) for t in (v, x1, m2))
    c0 = layer * HYENA_ORDER * W_HYENA
    skip2 = skip.reshape(1, HYENA_ORDER * W_HYENA)
    (a1,) = _stage_a(tabs, nh, vp, w1=tabs['w1_pair'])
    y1 = _stage_b(tabs, a1, kf, c_off=c0)
    z, a2 = _stage_a(tabs, nh, vp, y=y1, mul=x1p, skip=skip2, w1=tabs['w1_pair'], c_off=0)
    y2 = _stage_b(tabs, a2, kf, c_off=c0 + W_HYENA)
    (gp,) = _stage_a(tabs, nh, z, y=y2, mul=m2p, skip=skip2, want_fwd=False, c_off=W_HYENA)
    return _unpermute_seq(gp, nh)


def _ctx_conv_kernel(v_ref, x1_ref, m2_ref, k_ref, s_ref, skip_ref, ff_ref, fk_ref, gi_ref, o_ref,
                     *, n):
    def conv(u, o):
        kfull = k_ref[:, o * W_HYENA:(o + 1) * W_HYENA] / s_ref[:, o * W_HYENA:(o + 1) * W_HYENA]
        kf = jnp.dot(fk_ref[...], kfull.astype(BF16), preferred_element_type=F32)
        uf = jnp.dot(ff_ref[...], u.astype(BF16), preferred_element_type=F32)
        ure, uim, kre, kim = uf[:n], uf[n:], kf[:n], kf[n:]
        z = jnp.concatenate([ure * kre - uim * kim, ure * kim + uim * kre], axis=0).astype(BF16)
        y = jnp.dot(gi_ref[...], z, preferred_element_type=F32)
        return y + u * skip_ref[o:o + 1, :]

    v = v_ref[...].astype(F32)
    z1 = x1_ref[...].astype(F32) * conv(v, 0)
    o_ref[...] = (m2_ref[...].astype(F32) * conv(z1, 1)).astype(o_ref.dtype)


def _ctx_tables(lc):
    n = 2 * lc
    k = jnp.arange(n, dtype=jnp.int32)
    ph = _angles(k[:, None], k[None, :], n)
    c, s = jnp.cos(ph), jnp.sin(ph)
    fk = jnp.concatenate([c, -s], axis=0)
    ff = fk[:, :lc]
    gi = jnp.concatenate([c[:lc], -s[:lc]], axis=1) / n
    return ff.astype(BF16), fk.astype(BF16), gi.astype(BF16)


def _ctx_hyena(ctabs, k_un, ssum, layer, v, x1, m2, skip):
    b, lc, c = v.shape
    n = 2 * lc
    ff, fk, gi = ctabs
    wc = HYENA_ORDER * W_HYENA
    tok = pl.BlockSpec((None, lc, c), lambda bi: (bi, 0, 0))
    return pl.pallas_call(
        functools.partial(_ctx_conv_kernel, n=n),
        grid=(b,),
        in_specs=[tok, tok, tok,
                  pl.BlockSpec((n, wc), lambda bi: (0, layer)),
                  pl.BlockSpec((None, 1, wc), lambda bi: (layer, 0, 0)),
                  pl.BlockSpec((HYENA_ORDER, c), lambda bi: (0, 0)),
                  _full_spec(ff), _full_spec(fk), _full_spec(gi)],
        out_specs=tok,
        out_shape=jax.ShapeDtypeStruct((b, lc, c), BF16),
        compiler_params=_cparams(("parallel",)),
        name="ctx_hyena",
    )(v, x1, m2, k_un, ssum, skip, ff, fk, gi)


def _merge_kernel(xm_ref, xp_ref, xn_ref, gh_ref, att_ref, g_ref, sc_ref, sh_ref, gt_ref,
                  w1_ref, b1_ref, cw_ref, cb_ref, wg_ref, bg_ref, sco_ref, hyo_ref, mlo_ref, wo_ref,
                  fg_ref, o_ref, *, final, tm, tiles_per_seq):
    i = pl.program_id(0)
    first = (i % tiles_per_seq) == 0
    last = (i % tiles_per_seq) == tiles_per_seq - 1
    x = xm_ref[...]
    d = x.shape[-1]
    x_ext = jnp.concatenate([xp_ref[...], x, xn_ref[...]], axis=0)
    hb_ext = _mod_norm(x_ext, g_ref[...], sc_ref[...], sh_ref[...]).astype(BF16)
    p = jnp.dot(hb_ext, w1_ref[...], preferred_element_type=F32) + b1_ref[...]
    prod = p[:, 2 * W_CONV:3 * W_CONV] * p[:, 0:W_CONV]
    conv = _dwconv3_ext(prod, cw_ref[...], cb_ref[...], first, last, tm)
    pm = p[SUBLANE:SUBLANE + tm]
    a = (_silu(pm[:, 3 * W_CONV:4 * W_CONV]) * (pm[:, W_CONV:2 * W_CONV] * conv)).astype(BF16)
    sm = _silu(pm[:, 4 * W_CONV:4 * W_CONV + W_MLA])
    hb = hb_ext[SUBLANE:SUBLANE + tm]
    gates = jax.nn.sigmoid(jnp.dot(hb, wg_ref[...], preferred_element_type=F32) + bg_ref[...])
    ya = jnp.dot(a, sco_ref[...], preferred_element_type=F32)
    yh = jnp.dot(gh_ref[...], hyo_ref[...], preferred_element_type=F32)
    am = (sm * att_ref[...].astype(F32)).astype(BF16)
    ym = jnp.dot(am, mlo_ref[...], preferred_element_type=F32)
    y = gates[:, 0:d] * ya + gates[:, d:2 * d] * yh + gates[:, 2 * d:3 * d] * ym
    o = jnp.dot(y.astype(BF16), wo_ref[...], preferred_element_type=F32)
    xn = x + gt_ref[...] * o
    if final:
        xn = _rms(xn, fg_ref[...])
    o_ref[...] = xn


def _merge_call(x2d, seq_len, mods, g, lw, cw, cb, gh, att, final_g, final):
    n_rows, d = x2d.shape
    tm = min(512, seq_len)
    tiles_per_seq = seq_len // tm
    scale, shift, gate = mods
    g2 = g.reshape(1, 1, d)
    fg = final_g.reshape(1, d)
    tok = lambda wd: pl.BlockSpec((tm, wd), lambda i: (i, 0))
    consts = [lw['w1'], lw['b1'], cw, cb, lw['wg'], lw['bg'], lw['sc_out'], lw['hy_out'], lw['mla_out'],
              lw['w_o'], fg]
    in_specs = _halo_specs(n_rows, tm, d) + [
        tok(W_HYENA), tok(W_MLA),
        _mod_spec(g2, tiles_per_seq), _mod_spec(scale, tiles_per_seq),
        _mod_spec(shift, tiles_per_seq), _mod_spec(gate, tiles_per_seq)]
    in_specs += [_full_spec(c) for c in consts]
    return pl.pallas_call(
        functools.partial(_merge_kernel, final=final, tm=tm, tiles_per_seq=tiles_per_seq),
        grid=(n_rows // tm,),
        in_specs=in_specs,
        out_specs=tok(d),
        out_shape=jax.ShapeDtypeStruct((n_rows, d), F32),
        compiler_params=_cparams(("parallel",)),
        name="merge_final" if final else "merge",
    )(x2d, x2d, x2d, gh, att, g2, scale, shift, gate, *consts)


def _rope_swap_cols(w):
    half = QK_ROPE // 2
    return jnp.concatenate([-w[..., half:], w[..., :half]], axis=-1)


def _layer_weights(i, w_in, b_in, mla_q_norm, mla_w_uq, mla_kv_norm, mla_w_ukv, sc_out, hy_out,
                   mla_out, w_o):
    wi, bi = w_in[i], b_in[i]
    d = wi.shape[0]
    lw = {}
    lw['w1'] = jnp.concatenate([wi[:, O_XIN:O_HPROJ], wi[:, O_ZM:O_GATES]], axis=1).astype(BF16)
    lw['b1'] = jnp.concatenate([bi[O_XIN:O_HPROJ], bi[O_ZM:O_GATES]])[None, :]
    lw['w2'] = wi[:, O_HPROJ:O_CQ].astype(BF16)
    lw['b2'] = bi[O_HPROJ:O_CQ][None, :]
    wkr, bkr = wi[:, O_KR:O_ZM], bi[O_KR:O_ZM]
    zw = lambda n: jnp.zeros((d, n), F32)
    zb = lambda n: jnp.zeros((n,), F32)
    lw['w3'] = jnp.concatenate([wi[:, O_CQ:O_KR], wkr, zw(96), _rope_swap_cols(wkr), zw(96)], axis=1).astype(BF16)
    lw['b3'] = jnp.concatenate([bi[O_CQ:O_KR], bkr, zb(96), _rope_swap_cols(bkr), zb(96)])[None, :]
    lw['qg'] = mla_q_norm[i][None, :]
    lw['kvg'] = mla_kv_norm[i][None, :]
    wuq = mla_w_uq[i].reshape(Q_LORA, N_HEADS, QK_NOPE + QK_ROPE)
    pad = jnp.zeros((Q_LORA, N_HEADS, HEAD_PAD - QK_NOPE - QK_ROPE), F32)
    lw['wq'] = jnp.concatenate([wuq, pad], axis=-1).reshape(Q_LORA, N_HEADS * HEAD_PAD).astype(BF16)
    zn = jnp.zeros((Q_LORA, N_HEADS, QK_NOPE), F32)
    lw['wqs'] = jnp.concatenate([zn, _rope_swap_cols(wuq[..., QK_NOPE:]), pad], axis=-1).reshape(
        Q_LORA, N_HEADS * HEAD_PAD).astype(BF16)
    wukv = mla_w_ukv[i].reshape(KV_LORA, N_HEADS, QK_NOPE + V_HEAD)
    padk = jnp.zeros((KV_LORA, N_HEADS, HEAD_PAD - QK_NOPE), F32)
    lw['wk'] = jnp.concatenate([wukv[..., :QK_NOPE], padk], axis=-1).reshape(
        KV_LORA, N_HEADS * HEAD_PAD).astype(BF16)
    padv = jnp.zeros((KV_LORA, N_HEADS, HEAD_PAD - V_HEAD), F32)
    lw['wv'] = jnp.concatenate([wukv[..., QK_NOPE:], padv], axis=-1).reshape(
        KV_LORA, N_HEADS * HEAD_PAD).astype(BF16)
    lw['vb'] = jnp.zeros((N_HEADS, HEAD_PAD), F32).at[:, V_HEAD].set(1.0).reshape(1, N_HEADS * HEAD_PAD)
    e2 = jnp.zeros((QK_ROPE, N_HEADS, HEAD_PAD), F32).at[:, :, QK_NOPE:QK_NOPE + QK_ROPE].set(
        jnp.eye(QK_ROPE, dtype=F32)[:, None, :])
    lw['e2'] = e2.reshape(QK_ROPE, N_HEADS * HEAD_PAD).astype(BF16)
    lw['wg'] = wi[:, O_GATES:].astype(BF16)
    lw['bg'] = bi[O_GATES:][None, :]
    lw['sc_out'] = sc_out[i].astype(BF16)
    lw['hy_out'] = hy_out[i].astype(BF16)
    lw['mla_out'] = mla_out[i].astype(BF16)
    lw['w_o'] = w_o[i].astype(BF16)
    return lw


def _rope_tables(seq_len, use_rope):
    scale = (QK_NOPE + QK_ROPE) ** -0.5 * math.log2(math.e)
    if use_rope:
        rows = seq_len // GRID_W
        row = jnp.broadcast_to(jnp.arange(rows, dtype=F32)[:, None], (rows, GRID_W)).reshape(seq_len)
        col = jnp.broadcast_to(jnp.arange(GRID_W, dtype=F32)[None, :], (rows, GRID_W)).reshape(seq_len)
        n_f = QK_ROPE // 4
        inv = ROPE_BASE ** (-jnp.arange(n_f, dtype=F32) / n_f)
        ang = jnp.concatenate([row[:, None] * inv, col[:, None] * inv], axis=-1)
        cos, sin = jnp.cos(ang), jnp.sin(ang)
    else:
        cos = jnp.ones((seq_len, QK_ROPE // 2), F32)
        sin = jnp.zeros((seq_len, QK_ROPE // 2), F32)
    cosk = jnp.concatenate([cos, cos], axis=-1)
    sink = jnp.concatenate([sin, sin], axis=-1)
    ones = jnp.ones((seq_len, QK_NOPE), F32)
    zeros = jnp.zeros((seq_len, QK_NOPE), F32)
    tail = jnp.zeros((seq_len, HEAD_PAD - QK_NOPE - QK_ROPE), F32)
    cosq = jnp.concatenate([ones, cosk, tail], axis=-1) * scale
    sinq = jnp.concatenate([zeros, sink, tail], axis=-1) * scale
    return cosq, sinq, cosk, sink


def kernel(x, c, ctx, c_ctx, ada_w, ada_b, norm_g, w_in, b_in, sc_conv_w, sc_conv_b, sc_out, hy_conv_w,
           hy_conv_b, hy_w1, hy_b1, hy_w2, hy_b2, hy_w3, hy_freq, hy_skip, hy_out, mla_q_norm, mla_w_uq,
           mla_kv_norm, mla_w_ukv, mla_out, w_o, final_g):
    bsz, seq, d = x.shape
    lc = ctx.shape[1]
    depth = ada_w.shape[0]
    nh = seq // N2
    assert d == D_MODEL and bsz % 2 == 0 and nh % SUBLANE == 0 and bsz <= 7

    cc = jnp.zeros((8, d), F32).at[:bsz].set(c).at[bsz].set(c_ctx)
    mods = _ada_mods(cc, ada_w, ada_b)

    tabs = _dft_tables(nh)
    k_un, ssum = _filter_gen(seq, depth, True, hy_w1, hy_b1, hy_w2, hy_b2, hy_w3, hy_freq)
    kf = _filter_spectrum(tabs, nh, k_un, ssum)
    if depth > 1:
        kc_un, sc_sum = _filter_gen(lc, depth - 1, False, hy_w1, hy_b1, hy_w2, hy_b2, hy_w3, hy_freq)
        ctabs = _ctx_tables(lc)

    rope_l = _rope_tables(seq, True)
    rope_c = _rope_tables(lc, False)

    x_lat = x.reshape(bsz * seq, d)
    x_ctx = ctx.reshape(bsz * lc, d)
    for i in range(depth):
        last = i == depth - 1
        lw = _layer_weights(i, w_in, b_in, mla_q_norm, mla_w_uq, mla_kv_norm, mla_w_ukv, sc_out,
                            hy_out, mla_out, w_o)
        m = mods[i]
        split = lambda r: tuple(r[:, None, j * d:(j + 1) * d] for j in range(3))
        shift_l, scale_l, gate_l = split(m[:bsz])
        shift_c, scale_c, gate_c = split(m[bsz:bsz + 1])
        g = norm_g[i]
        scw, scb = sc_conv_w[i], sc_conv_b[i][None, :]
        hcw, hcb = hy_conv_w[i], hy_conv_b[i][None, :]

        hq = N_HEADS * HEAD_PAD
        v_h, x1_h, m2_h, q_l, k_all, v_all = _kp_call(
            x_lat, bsz, seq, seq + lc, 0, (scale_l, shift_l), g, lw, hcw, hcb, rope_l)
        v_hc, x1_hc, m2_hc, q_c, k_all, v_all = _kp_call(
            x_ctx, bsz, lc, seq + lc, seq, (scale_c, shift_c), g, lw, hcw, hcb, rope_c,
            kv_bufs=(k_all, v_all))
        att_l = _attention(q_l.reshape(bsz, seq, hq), k_all, v_all).reshape(bsz * seq, W_MLA)
        r3 = lambda t: t.reshape(bsz, seq, W_HYENA)
        gh_l = _hyena_long(tabs, nh, kf, i, r3(v_h), r3(x1_h), r3(m2_h), hy_skip[i])
        new_lat = _merge_call(x_lat, seq, (scale_l, shift_l, gate_l), g, lw, scw, scb,
                              gh_l.reshape(bsz * seq, W_HYENA), att_l, final_g, last)
        if not last:
            att_c = _attention(q_c.reshape(bsz, lc, hq), k_all, v_all, kv_row0=seq,
                               kv_len=lc).reshape(bsz * lc, W_MLA)
            rc = lambda t: t.reshape(bsz, lc, W_HYENA)
            gh_c = _ctx_hyena(ctabs, kc_un, sc_sum, i, rc(v_hc), rc(x1_hc), rc(m2_hc), hy_skip[i])
            x_ctx = _merge_call(x_ctx, lc, (scale_c, shift_c, gate_c), g, lw, scw, scb,
                                gh_c.reshape(bsz * lc, W_HYENA), att_c, final_g, False)
        x_lat = new_lat
    return x_lat.reshape(bsz, seq, d)
```

```python
import functools
import math

import jax
import jax.numpy as jnp
from jax import lax
from jax.experimental import pallas as pl
from jax.experimental.pallas import tpu as pltpu

F32 = jnp.float32
BF16 = jnp.bfloat16

D_MODEL = 1024
GRID_W = 64
W_CONV = 512
W_HYENA = 512
HYENA_ORDER = 2
HYENA_EMB = 33
HYENA_BANDS = (HYENA_EMB - 1) // 2
HYENA_HID = 64
HYENA_FAST_DECAY = 0.3
HYENA_SLOW_DECAY = 1.5
HYENA_TARGET = 1e-2
N_HEADS = 8
QK_NOPE = 64
QK_ROPE = 32
V_HEAD = 64
Q_LORA = 384
KV_LORA = 256
W_MLA = N_HEADS * V_HEAD
ROPE_BASE = 10000.0
N_BRANCH = 3
EPS = 1e-6

O_XIN, O_GB, O_GC, O_ZA = 0, 512, 1024, 1536
O_HPROJ, O_ZH = 2048, 3584
O_CQ, O_CKV, O_KR, O_ZM, O_GATES = 4096, 4480, 4736, 4768, 5280

LANE = 128
SUBLANE = 8
HEAD_PAD = 128
N2 = 128
VMEM_CAPACITY = 64 * 1024 * 1024
VMEM_LIMIT = VMEM_CAPACITY - 8 * 1024 * 1024


def _cparams(sem):
    return pltpu.CompilerParams(dimension_semantics=sem, vmem_limit_bytes=VMEM_LIMIT)


def _ada_kernel(c_ref, w_ref, b_ref, o_ref):
    c = c_ref[...]
    s = c * jax.nn.sigmoid(c)
    o_ref[...] = jnp.dot(s, w_ref[...], preferred_element_type=F32,
                         precision=lax.Precision.HIGHEST) + b_ref[...]


def _ada_mods(cc, ada_w, ada_b):
    depth = ada_w.shape[0]
    d = cc.shape[1]
    return pl.pallas_call(
        _ada_kernel,
        grid=(depth, 3),
        in_specs=[pl.BlockSpec((8, d), lambda l, j: (0, 0)),
                  pl.BlockSpec((None, d, d), lambda l, j: (l, 0, j)),
                  pl.BlockSpec((None, 1, d), lambda l, j: (l, 0, j))],
        out_specs=pl.BlockSpec((None, 8, d), lambda l, j: (l, 0, j)),
        out_shape=jax.ShapeDtypeStruct((depth, 8, 3 * d), F32),
        compiler_params=_cparams(("parallel", "parallel")),
        name="ada_mods",
    )(cc, ada_w, ada_b.reshape(depth, 1, 3 * d))


def _mod_norm(x, g, scale, shift):
    y = x * lax.rsqrt(jnp.mean(x * x, axis=-1, keepdims=True) + EPS)
    return (y * g) * (1.0 + scale) + shift


def _rms(x, g):
    return x * lax.rsqrt(jnp.mean(x * x, axis=-1, keepdims=True) + EPS) * g


def _silu(x):
    return x * jax.nn.sigmoid(x)


def _dwconv3_ext(u_ext, w, b, first, last, tm):
    n = tm + 2 * SUBLANE
    prev = pltpu.roll(u_ext, 1, 0)[SUBLANE:SUBLANE + tm]
    nxt = pltpu.roll(u_ext, n - 1, 0)[SUBLANE:SUBLANE + tm]
    row = lax.broadcasted_iota(jnp.int32, (tm, 1), 0)
    prev = jnp.where(jnp.logical_and(first, row == 0), 0.0, prev)
    nxt = jnp.where(jnp.logical_and(last, row == tm - 1), 0.0, nxt)
    return prev * w[0:1] + u_ext[SUBLANE:SUBLANE + tm] * w[1:2] + nxt * w[2:3] + b


def _kp_kernel(xm_ref, xp_ref, xn_ref, g_ref, sc_ref, sh_ref, w2_ref, b2_ref, cw_ref, cb_ref,
               w3_ref, b3_ref, qg_ref, kvg_ref, wq_ref, wqs_ref, wk_ref, wv_ref, vb_ref, e2_ref,
               cosq_ref, sinq_ref, cosk_ref, sink_ref, *rest, tm, tiles_per_seq):
    vh_ref, x1_ref, m2_ref, q_ref, k_ref, v_ref = rest[-6:]
    i = pl.program_id(0)
    first = (i % tiles_per_seq) == 0
    last = (i % tiles_per_seq) == tiles_per_seq - 1
    x_ext = jnp.concatenate([xp_ref[...], xm_ref[...], xn_ref[...]], axis=0)
    hb_ext = _mod_norm(x_ext, g_ref[...], sc_ref[...], sh_ref[...]).astype(BF16)

    p2 = jnp.dot(hb_ext, w2_ref[...], preferred_element_type=F32) + b2_ref[...]
    u = _dwconv3_ext(p2[:, 0:3 * W_HYENA], cw_ref[...], cb_ref[...], first, last, tm)
    zh = p2[SUBLANE:SUBLANE + tm, 3 * W_HYENA:4 * W_HYENA]
    vh_ref[...] = u[:, 0:W_HYENA].astype(vh_ref.dtype)
    x1_ref[...] = u[:, W_HYENA:2 * W_HYENA].astype(x1_ref.dtype)
    m2_ref[...] = (_silu(zh) * u[:, 2 * W_HYENA:3 * W_HYENA]).astype(m2_ref.dtype)

    p = jnp.dot(hb_ext[SUBLANE:SUBLANE + tm], w3_ref[...], preferred_element_type=F32) + b3_ref[...]
    cq = _rms(p[:, 0:Q_LORA], qg_ref[...]).astype(BF16)
    ckv = _rms(p[:, Q_LORA:Q_LORA + KV_LORA], kvg_ref[...]).astype(BF16)
    kr = p[:, 640:640 + QK_ROPE]
    krs = p[:, 768:768 + QK_ROPE]
    qa = jnp.dot(cq, wq_ref[...], preferred_element_type=F32)
    qb = jnp.dot(cq, wqs_ref[...], preferred_element_type=F32)
    cosq = jnp.tile(cosq_ref[...], (1, N_HEADS))
    sinq = jnp.tile(sinq_ref[...], (1, N_HEADS))
    q_ref[...] = (qa * cosq + qb * sinq).astype(q_ref.dtype)
    kr_rot = (kr * cosk_ref[...] + krs * sink_ref[...]).astype(BF16)
    kn = jnp.dot(ckv, wk_ref[...], preferred_element_type=F32)
    krp = jnp.dot(kr_rot, e2_ref[...], preferred_element_type=F32)
    k_ref[...] = (kn + krp).astype(k_ref.dtype)
    v_ref[...] = (jnp.dot(ckv, wv_ref[...], preferred_element_type=F32) + vb_ref[...]).astype(v_ref.dtype)


def _mod_spec(mod, tiles_per_seq):
    d = mod.shape[-1]
    if mod.shape[0] == 1:
        return pl.BlockSpec((None, 1, d), lambda i: (0, 0, 0))
    return pl.BlockSpec((None, 1, d), lambda i: (i // tiles_per_seq, 0, 0))


def _full_spec(a):
    nd = a.ndim
    return pl.BlockSpec(a.shape, lambda i: (0,) * nd)


def _halo_specs(n_rows, tm, d):
    r = tm // SUBLANE
    nb = n_rows // SUBLANE
    return [pl.BlockSpec((tm, d), lambda i: (i, 0)),
            pl.BlockSpec((SUBLANE, d), lambda i: (jnp.maximum(i * r - 1, 0), 0)),
            pl.BlockSpec((SUBLANE, d), lambda i: (jnp.minimum((i + 1) * r, nb - 1), 0))]


def _kp_call(x2d, bsz, seq_len, kv_row0, mods, g, lw, cw, cb, tabs, kv_bufs):
    n_rows, d = x2d.shape
    tm = min(512, seq_len)
    tiles_per_seq = seq_len // tm
    blk0 = kv_row0 // tm
    assert kv_row0 % tm == 0
    scale, shift = mods
    g2 = g.reshape(1, 1, d)
    cosq, sinq, cosk, sink = tabs
    consts = [lw['w2'], lw['b2'], cw, cb, lw['w3'], lw['b3'], lw['qg'], lw['kvg'], lw['wq'], lw['wqs'],
              lw['wk'], lw['wv'], lw['vb'], lw['e2']]

    def tab_spec(t):
        return pl.BlockSpec((tm, t.shape[1]), lambda i: (i % tiles_per_seq, 0))

    in_specs = _halo_specs(n_rows, tm, d) + [
        _mod_spec(g2, tiles_per_seq), _mod_spec(scale, tiles_per_seq), _mod_spec(shift, tiles_per_seq)]
    in_specs += [_full_spec(a) for a in consts]
    in_specs += [tab_spec(t) for t in (cosq, sinq, cosk, sink)]
    args = [x2d, x2d, x2d, g2, scale, shift, *consts, cosq, sinq, cosk, sink]
    hq = N_HEADS * HEAD_PAD
    tok = lambda wd: pl.BlockSpec((tm, wd), lambda i: (i, 0))
    tok_shape = lambda wd: jax.ShapeDtypeStruct((n_rows, wd), BF16)
    kv_spec = pl.BlockSpec((None, tm, hq), lambda i: (i // tiles_per_seq, blk0 + i % tiles_per_seq, 0))
    kv_shape = jax.ShapeDtypeStruct(kv_bufs[0].shape, BF16)
    assert kv_bufs[0].shape[0] == bsz and kv_bufs[0].shape == kv_bufs[1].shape
    in_specs += [pl.BlockSpec(memory_space=pl.ANY)] * 2
    aliases = {len(args): 4, len(args) + 1: 5}
    args += list(kv_bufs)
    return pl.pallas_call(
        functools.partial(_kp_kernel, tm=tm, tiles_per_seq=tiles_per_seq),
        grid=(n_rows // tm,),
        in_specs=in_specs,
        out_specs=[tok(W_HYENA), tok(W_HYENA), tok(W_HYENA), tok(hq), kv_spec, kv_spec],
        out_shape=[tok_shape(W_HYENA), tok_shape(W_HYENA), tok_shape(W_HYENA), tok_shape(hq),
                   kv_shape, kv_shape],
        input_output_aliases=aliases,
        compiler_params=_cparams(("parallel",)),
        name="kp_hyena_mla",
    )(*args)


ATTN_CHAIN_ROWS = 512
ATTN_RING = 3


def _attn_kernel(q_ref, k_ref, v_ref, o_ref, s_ref, p_ref, m_ref, acc_ref):
    h = pl.program_id(2)
    ki = pl.program_id(3)

    @pl.when(ki == 0)
    def _():
        m_ref[...] = jnp.full(m_ref.shape, -jnp.inf, F32)
        acc_ref[...] = jnp.zeros(acc_ref.shape, F32)

    cr = s_ref.shape[1]
    n_chains = q_ref.shape[0] // cr
    chains = [pl.ds(c * cr, cr) for c in range(n_chains)]
    reps = s_ref.shape[2] // LANE
    alphas = {}
    for step in range(n_chains + 2):
        if step < n_chains:
            s_ref[step % ATTN_RING] = lax.dot_general(q_ref[chains[step], :], k_ref[...],
                                                      (((1,), (1,)), ((), ())),
                                                      preferred_element_type=F32)
        c = step - 1
        if 0 <= c < n_chains:
            s = s_ref.at[c % ATTN_RING]
            m_prev = m_ref[chains[c], :]
            m_new = jnp.maximum(m_prev, jnp.max(s[...], axis=-1, keepdims=True))
            alphas[c] = jnp.exp2(m_prev - m_new)
            m_ref[chains[c], :] = m_new
            p_ref[c % ATTN_RING] = jnp.exp2((s[...] - jnp.tile(m_new, (1, reps))).astype(BF16))
        c = step - 2
        if 0 <= c < n_chains:
            acc_ref[chains[c], :] = acc_ref[chains[c], :] * alphas.pop(c) + jnp.dot(
                p_ref[c % ATTN_RING], v_ref[...], preferred_element_type=F32)

    @pl.when(ki == pl.num_programs(3) - 1)
    def _():
        acc = acc_ref[...]
        o = (acc[:, 0:V_HEAD] / acc[:, V_HEAD:V_HEAD + 1]).astype(o_ref.dtype)

        @pl.when(h % 2 == 0)
        def _():
            o_ref[:, 0:V_HEAD] = o

        @pl.when(h % 2 == 1)
        def _():
            o_ref[:, V_HEAD:2 * V_HEAD] = o


MXU_DIM = 256
Q_TILE = 8192
KV_TILE_CAP = 2816


def _kv_tile(lk):
    for step in (MXU_DIM, LANE):
        cands = [t for t in range(step, min(lk, KV_TILE_CAP) + 1, step) if lk % t == 0]
        if cands:
            return cands[-1]
    return lk


def _attention(q, k, v, kv_row0=0, kv_len=None):
    b, lq, _ = q.shape
    lk = k.shape[1] if kv_len is None else kv_len
    tq = min(Q_TILE, lq)
    cr = min(ATTN_CHAIN_ROWS, tq)
    tk = _kv_tile(lk)
    assert kv_row0 % tk == 0
    kb0 = kv_row0 // tk
    grid = (b, lq // tq, N_HEADS, lk // tk)
    return pl.pallas_call(
        _attn_kernel,
        grid=grid,
        in_specs=[pl.BlockSpec((None, tq, HEAD_PAD), lambda bi, qi, h, ki: (bi, qi, h)),
                  pl.BlockSpec((None, tk, HEAD_PAD), lambda bi, qi, h, ki: (bi, kb0 + ki, h)),
                  pl.BlockSpec((None, tk, HEAD_PAD), lambda bi, qi, h, ki: (bi, kb0 + ki, h))],
        out_specs=pl.BlockSpec((None, tq, 2 * V_HEAD), lambda bi, qi, h, ki: (bi, qi, h // 2)),
        out_shape=jax.ShapeDtypeStruct((b, lq, W_MLA), BF16),
        scratch_shapes=[pltpu.VMEM((ATTN_RING, cr, tk), F32), pltpu.VMEM((ATTN_RING, cr, tk), BF16),
                        pltpu.VMEM((tq, LANE), F32), pltpu.VMEM((tq, HEAD_PAD), F32)],
        compiler_params=_cparams(("parallel", "parallel", "arbitrary", "arbitrary")),
        name="mla_attention",
    )(q, k, v)


COL_BWD = HYENA_EMB
COL_DROP = HYENA_EMB + 1


def _dot_bf16x3(a, b):
    a_hi = a.astype(BF16)
    b_hi = b.astype(BF16)
    a_lo = (a - a_hi.astype(F32)).astype(BF16)
    b_lo = (b - b_hi.astype(F32)).astype(BF16)
    dot = functools.partial(jnp.dot, preferred_element_type=F32)
    return dot(a_hi, b_hi) + (dot(a_lo, b_hi) + dot(a_hi, b_lo))


def _filt_kernel(z_ref, w1a_ref, w1b_ref, b1_ref, w2_ref, b2_ref, fr_ref, w3a_ref, w3b_ref, dl_ref,
                 k_ref, s_ref):
    i = pl.program_id(1)
    hp = lax.Precision.HIGHEST
    wc = HYENA_ORDER * W_HYENA
    z = z_ref[...]
    half = z.shape[0] // 2
    fr = fr_ref[...]
    pre = _dot_bf16x3(z[:half], w1a_ref[...]) + _dot_bf16x3(z[half:], w1b_ref[...])
    hid = jnp.sin(fr * (pre + b1_ref[...]))
    hid = jnp.sin(fr * (jnp.dot(hid, w2_ref[...], preferred_element_type=F32, precision=hp) + b2_ref[...]))
    hb = hid.astype(BF16)
    h2 = jnp.concatenate([jnp.dot(hb, w3a_ref[...], preferred_element_type=F32),
                          jnp.dot(hb, w3b_ref[...], preferred_element_type=F32)], axis=0)
    h = jnp.where(z[:, COL_BWD:COL_BWD + 1] > 0.5, h2[:, wc:], h2[:, :wc])
    t = z[:, 0:1]
    h = h * jnp.exp(-t * dl_ref[...])

    @pl.when(i == 0)
    def _():
        s_ref[...] = jnp.zeros(s_ref.shape, F32)

    s_ref[...] += jnp.sum(jnp.abs(h), axis=0, keepdims=True)
    k_ref[...] = jnp.where(z[:, COL_DROP:COL_DROP + 1] > 0.5, 0.0, h)


def _filter_tables(seq_len, permuted):
    t = jnp.linspace(0.0, 1.0, seq_len, dtype=F32)[:, None]
    w = 2.0 * math.pi * jnp.arange(seq_len, dtype=F32)[:, None] / seq_len
    f = jnp.linspace(1e-4, HYENA_BANDS - 1, HYENA_BANDS, dtype=F32)[None, :]
    z = jnp.concatenate([t, jnp.cos(f * w), -jnp.sin(f * w)], axis=-1)
    z2 = jnp.concatenate([z, z[0:1], jnp.flip(z[1:], axis=0)], axis=0)
    n = jnp.arange(2 * seq_len)
    flags = jnp.stack([n >= seq_len, n == seq_len], axis=-1).astype(F32)
    zz = jnp.concatenate([z2, flags], axis=-1)
    zz = jnp.pad(zz, ((0, 0), (0, LANE - zz.shape[1])))
    if permuted:
        zz = zz.reshape(2 * seq_len // N2, N2, LANE).transpose(1, 0, 2).reshape(2 * seq_len, LANE)
    return zz


def _hyena_deltas():
    max_decay = math.log(HYENA_TARGET) / HYENA_FAST_DECAY
    min_decay = math.log(HYENA_TARGET) / HYENA_SLOW_DECAY
    deltas = jnp.abs(jnp.linspace(min_decay, max_decay, W_HYENA, dtype=F32))
    return jnp.tile(deltas, HYENA_ORDER)[None, :]


def _filter_gen(seq_len, n_layers, permuted, hy_w1, hy_b1, hy_w2, hy_b2, hy_w3, hy_freq):
    zz = _filter_tables(seq_len, permuted)
    tmf = min(512, seq_len)
    tiles_half = seq_len // tmf
    wc = HYENA_ORDER * W_HYENA
    hh = HYENA_HID
    zpad = lambda a, lo, hi, ax: jnp.pad(a, [(lo, hi) if d == ax else (0, 0) for d in range(a.ndim)])
    w1p = jnp.pad(hy_w1[:n_layers], ((0, 0), (0, LANE - HYENA_EMB), (0, 0)))
    w1a, w1b = zpad(w1p, 0, hh, 2), zpad(w1p, hh, 0, 2)
    w2 = hy_w2[:n_layers]
    w2bd = jnp.concatenate([zpad(w2, 0, hh, 2), zpad(w2, hh, 0, 2)], axis=1)
    w3 = hy_w3[:n_layers].astype(BF16)
    w3a, w3b = zpad(w3, 0, hh, 1), zpad(w3, hh, 0, 1)
    r1 = lambda a: jnp.tile(a[:n_layers].reshape(n_layers, 1, hh), (1, 1, 2))
    per_layer = lambda r, c: pl.BlockSpec((None, r, c), lambda l, i: (l, 0, 0))
    return pl.pallas_call(
        _filt_kernel,
        grid=(n_layers, 2 * tiles_half),
        in_specs=[pl.BlockSpec((tmf, LANE), lambda l, i: (i, 0)),
                  per_layer(LANE, LANE), per_layer(LANE, LANE), per_layer(1, LANE),
                  per_layer(LANE, LANE), per_layer(1, LANE), per_layer(1, LANE),
                  per_layer(LANE, 2 * wc), per_layer(LANE, 2 * wc),
                  pl.BlockSpec((1, wc), lambda l, i: (0, 0))],
        out_specs=[pl.BlockSpec((tmf, wc), lambda l, i: (i, l)),
                   pl.BlockSpec((None, 1, wc), lambda l, i: (l, 0, 0))],
        out_shape=[jax.ShapeDtypeStruct((2 * seq_len, n_layers * wc), F32),
                   jax.ShapeDtypeStruct((n_layers, 1, wc), F32)],
        compiler_params=_cparams(("parallel", "arbitrary")),
        name="hyena_filter",
    )(zz, w1a, w1b, r1(hy_b1), w2bd, r1(hy_b2), r1(hy_freq), w3a, w3b, _hyena_deltas())


def _angles(a, b, n):
    m = (a * b) % n
    return m.astype(F32) * (2.0 * math.pi / n)


def _dft_tables(nh):
    n1 = 2 * nh
    n = n1 * N2
    k1 = jnp.arange(n1, dtype=jnp.int32)
    th = _angles(k1[:, None], jnp.arange(nh, dtype=jnp.int32)[None, :], n1)
    c, s = jnp.cos(th), jnp.sin(th)
    w1_pair = jnp.concatenate([jnp.concatenate([c, s], 1), jnp.concatenate([-s, c], 1)], 0)
    thf = _angles(k1[:, None], k1[None, :], n1)
    w1_real = jnp.concatenate([jnp.cos(thf), -jnp.sin(thf)], 0)
    g1 = jnp.concatenate([jnp.concatenate([c.T, -s.T], 1), jnp.concatenate([s.T, c.T], 1)], 0) / n
    n2 = jnp.arange(N2, dtype=jnp.int32)
    al = _angles(k1[:, None], n2[None, :], n)
    be = _angles(n2[:, None], n2[None, :], N2)
    ca, sa = jnp.cos(al)[:, None, :], jnp.sin(al)[:, None, :]
    cb, sb = jnp.cos(be)[None], jnp.sin(be)[None]
    cp, sp = ca * cb - sa * sb, sa * cb + ca * sb
    f2 = jnp.concatenate([jnp.concatenate([cp, sp], 2), jnp.concatenate([-sp, cp], 2)], 1)
    cpt, spt = cp.transpose(0, 2, 1), sp.transpose(0, 2, 1)
    g2 = jnp.concatenate([jnp.concatenate([cpt, -spt], 2), jnp.concatenate([spt, cpt], 2)], 1)
    g1 = g1.reshape(2 * nh, 2, n1).transpose(0, 2, 1).reshape(2 * nh, 2 * n1)
    f2 = f2.reshape(n1, 2 * N2, 2, N2).transpose(0, 1, 3, 2).reshape(n1, 2 * N2, 2 * N2)
    g2 = g2.reshape(n1, 2, N2, 2 * N2).transpose(0, 2, 1, 3).reshape(n1, 2 * N2, 2 * N2)
    w1_pair = w1_pair.reshape(2, n1, 2 * nh).transpose(1, 0, 2).reshape(2 * n1, 2 * nh)
    w1_real = w1_real.reshape(2, n1, n1).transpose(1, 0, 2).reshape(2 * n1, n1)
    return dict(w1_pair=w1_pair.astype(BF16), w1_real=w1_real.astype(BF16), g1=g1.astype(BF16),
                f2=f2.astype(BF16), g2=g2.astype(BF16))


def _strided_rows(ref, start, size):
    parts = [ref[cc, pl.ds(start, size, stride=SUBLANE), :] for cc in range(ref.shape[0])]
    return parts[0] if len(parts) == 1 else jnp.concatenate(parts, axis=1)


def _pack_pairs(x):
    return pltpu.bitcast(x.astype(BF16), jnp.uint32)


def _unpack_pairs(packed):
    return pltpu.bitcast(packed, BF16)


def _store_chunks(ref, row0, val):
    for cc in range(ref.shape[0]):
        ref[cc, pl.ds(row0, SUBLANE), :] = val[:, cc * LANE:(cc + 1) * LANE]


def _stage_a_kernel(*refs, n2t, n1, has_inv, has_fwd, has_div):
    it = iter(refs)
    y_ref = next(it) if has_inv else None
    g1_ref = next(it) if has_inv else None
    u_ref = next(it)
    mul_ref = next(it) if has_inv else None
    skip_ref = next(it) if has_inv else None
    div_ref = next(it) if has_div else None
    w1_ref = next(it) if has_fwd else None
    e_ref = next(it) if has_inv else None
    a_ref = next(it) if has_fwd else None
    yy_ref = next(it) if has_inv else None
    if has_inv:
        for j in range(n2t):
            nb, jl = j // SUBLANE, j % SUBLANE
            rhs = _unpack_pairs(_strided_rows(y_ref.at[nb], jl, n1))
            yy_ref[j] = jnp.dot(g1_ref[...], rhs, preferred_element_type=F32)
        for j in range(n2t):
            e = mul_ref[j].astype(F32) * (yy_ref[j] + u_ref[j].astype(F32) * skip_ref[...])
            e_ref[j] = e.astype(e_ref.dtype)
    if has_fwd:
        for j in range(n2t):
            if has_inv:
                src = e_ref[j]
            elif has_div:
                src = (u_ref[j] / div_ref[...]).astype(BF16)
            else:
                src = u_ref[j].astype(BF16)
            a = jnp.dot(w1_ref[...], src, preferred_element_type=F32)
            packed = _pack_pairs(a)
            for kt in range(n1 // SUBLANE):
                _store_chunks(a_ref.at[kt], j * SUBLANE, packed[kt * SUBLANE:(kt + 1) * SUBLANE])


def _stage_a(tabs, nh, u, *, y=None, mul=None, skip=None, div=None, w1=None, want_fwd=True,
             ct=256, n2t=32, e_dtype=BF16, c_off=0):
    p, _, rows, c = u.shape
    n1 = 2 * nh
    has_inv = y is not None
    has_div = div is not None
    ct = min(ct, c)
    nbk = n2t // SUBLANE
    grid = (p, c // ct, N2 // n2t)
    args, specs = [], []
    if has_inv:
        args += [y, tabs['g1']]
        specs += [pl.BlockSpec((None, nbk, ct // LANE, n1 * SUBLANE, LANE),
                               lambda pi, ci, ni: (pi, ni, ci, 0, 0)),
                  pl.BlockSpec(tabs['g1'].shape, lambda pi, ci, ni: (0, 0))]
    args.append(u)
    specs.append(pl.BlockSpec((None, n2t, rows, ct), lambda pi, ci, ni: (pi, ni, 0, ci)))
    if has_inv:
        args += [mul, skip]
        specs += [pl.BlockSpec((None, n2t, rows, ct), lambda pi, ci, ni: (pi, ni, 0, ci)),
                  pl.BlockSpec((1, ct), lambda pi, ci, ni: (0, ci + c_off // ct))]
    if has_div:
        args.append(div)
        specs.append(pl.BlockSpec((1, ct), lambda pi, ci, ni: (0, ci)))
    if want_fwd:
        args.append(w1)
        specs.append(pl.BlockSpec(w1.shape, lambda pi, ci, ni: (0, 0)))
    out_shape, out_specs = [], []
    if has_inv:
        out_shape.append(jax.ShapeDtypeStruct(u.shape, e_dtype))
        out_specs.append(pl.BlockSpec((None, n2t, rows, ct), lambda pi, ci, ni: (pi, ni, 0, ci)))
    if want_fwd:
        out_shape.append(jax.ShapeDtypeStruct((p, n1 // SUBLANE, c // LANE, N2 * SUBLANE, LANE), jnp.uint32))
        out_specs.append(pl.BlockSpec((None, n1 // SUBLANE, ct // LANE, n2t * SUBLANE, LANE),
                                      lambda pi, ci, ni: (pi, 0, ci, ni, 0)))
    outs = pl.pallas_call(
        functools.partial(_stage_a_kernel, n2t=n2t, n1=n1, has_inv=has_inv, has_fwd=want_fwd,
                          has_div=has_div),
        grid=grid, in_specs=specs, out_specs=out_specs, out_shape=out_shape,
        scratch_shapes=[pltpu.VMEM((n2t, rows, ct), F32)] if has_inv else [],
        compiler_params=_cparams(("parallel", "parallel", "parallel")),
        name="hyena_stage_a" + ("_inv" if has_inv else "") + ("_fwd" if want_fwd else ""),
    )(*args)
    return outs


def _stage_b_kernel(*refs, filt_only):
    if filt_only:
        a_ref, f2_ref, o_ref = refs
        for kl in range(SUBLANE):
            rhs = _unpack_pairs(_strided_rows(a_ref, kl, N2))
            o_ref[kl] = jnp.dot(f2_ref[kl], rhs, preferred_element_type=F32).astype(o_ref.dtype)
        return
    a_ref, f2_ref, g2_ref, kf_ref, o_ref, t_ref, z_ref = refs
    for kl in range(SUBLANE):
        rhs = _unpack_pairs(_strided_rows(a_ref, kl, N2))
        t_ref[kl] = jnp.dot(f2_ref[kl], rhs, preferred_element_type=F32)
    for kl in range(SUBLANE):
        kf = kf_ref[kl].astype(F32)
        tre, tim = t_ref[kl, :N2], t_ref[kl, N2:]
        kre, kim = kf[:N2], kf[N2:]
        z_ref[kl, :N2] = (tre * kre - tim * kim).astype(BF16)
        z_ref[kl, N2:] = (tre * kim + tim * kre).astype(BF16)
    for kl in range(SUBLANE):
        y = jnp.dot(g2_ref[kl], z_ref[kl], preferred_element_type=F32)
        packed = _pack_pairs(y)
        for nt in range(N2 // SUBLANE):
            _store_chunks(o_ref.at[nt], kl * SUBLANE, packed[nt * SUBLANE:(nt + 1) * SUBLANE])


def _stage_b(tabs, a, kf=None, *, c_off=0, ct=512):
    p, nkt, ncc, _, _ = a.shape
    c = ncc * LANE
    n1 = nkt * SUBLANE
    ct = min(ct, c)
    filt_only = kf is None
    grid = (nkt, c // ct, p)
    a_spec = pl.BlockSpec((None, None, ct // LANE, N2 * SUBLANE, LANE),
                          lambda kt, ci, pi: (pi, kt, ci, 0, 0))
    tab_spec = pl.BlockSpec((SUBLANE, 2 * N2, 2 * N2), lambda kt, ci, pi: (kt, 0, 0))
    if filt_only:
        return pl.pallas_call(
            functools.partial(_stage_b_kernel, filt_only=True),
            grid=grid, in_specs=[a_spec, tab_spec],
            out_specs=pl.BlockSpec((SUBLANE, 2 * N2, ct), lambda kt, ci, pi: (kt, 0, ci)),
            out_shape=jax.ShapeDtypeStruct((n1, 2 * N2, c), BF16),
            compiler_params=_cparams(("parallel", "parallel", "parallel")),
            name="hyena_stage_b_filter",
        )(a, tabs['f2'])
    return pl.pallas_call(
        functools.partial(_stage_b_kernel, filt_only=False),
        grid=grid,
        in_specs=[a_spec, tab_spec, tab_spec,
                  pl.BlockSpec((SUBLANE, 2 * N2, ct), lambda kt, ci, pi: (kt, 0, ci + c_off // ct))],
        out_specs=pl.BlockSpec((None, N2 // SUBLANE, ct // LANE, SUBLANE * SUBLANE, LANE),
                               lambda kt, ci, pi: (pi, 0, ci, kt, 0)),
        out_shape=jax.ShapeDtypeStruct((p, N2 // SUBLANE, c // LANE, n1 * SUBLANE, LANE), jnp.uint32),
        scratch_shapes=[pltpu.VMEM((SUBLANE, 2 * N2, ct), F32), pltpu.VMEM((SUBLANE, 2 * N2, ct), BF16)],
        compiler_params=_cparams(("parallel", "parallel", "parallel")),
        name="hyena_stage_b",
    )(a, tabs['f2'], tabs['g2'], kf)


def _permute_seq(a, nh):
    b, _, c = a.shape
    return a.reshape(b // 2, 2, nh, N2, c).transpose(0, 3, 1, 2, 4).reshape(b // 2, N2, 2 * nh, c)


def _unpermute_seq(a, nh):
    p, _, _, c = a.shape
    return a.reshape(p, N2, 2, nh, c).transpose(0, 2, 3, 1, 4).reshape(2 * p, nh * N2, c)


def _filter_spectrum(tabs, nh, k_un, ssum):
    n, call = k_un.shape
    kp = k_un.reshape(1, N2, 2 * nh, call)
    (a,) = _stage_a(tabs, nh, kp, div=ssum.reshape(1, call), w1=tabs['w1_real'], ct=512, n2t=16)
    return _stage_b(tabs, a)


def _hyena_long(tabs, nh, kf, layer, v, x1, m2, skip):
    vp, x1p, m2p = (_permute_seq(t, nh) for t in (v, x1, m2))
    c0 = layer * HYENA_ORDER * W_HYENA
    skip2 = skip.reshape(1, HYENA_ORDER * W_HYENA)
    (a1,) = _stage_a(tabs, nh, vp, w1=tabs['w1_pair'])
    y1 = _stage_b(tabs, a1, kf, c_off=c0)
    z, a2 = _stage_a(tabs, nh, vp, y=y1, mul=x1p, skip=skip2, w1=tabs['w1_pair'], c_off=0)
    y2 = _stage_b(tabs, a2, kf, c_off=c0 + W_HYENA)
    (gp,) = _stage_a(tabs, nh, z, y=y2, mul=m2p, skip=skip2, want_fwd=False, c_off=W_HYENA)
    return _unpermute_seq(gp, nh)


def _ctx_conv_kernel(v_ref, x1_ref, m2_ref, k_ref, s_ref, skip_ref, ff_ref, fk_ref, gi_ref, o_ref,
                     *, n):
    def conv(u, o):
        kfull = k_ref[:, o * W_HYENA:(o + 1) * W_HYENA] / s_ref[:, o * W_HYENA:(o + 1) * W_HYENA]
        kf = jnp.dot(fk_ref[...], kfull.astype(BF16), preferred_element_type=F32)
        uf = jnp.dot(ff_ref[...], u.astype(BF16), preferred_element_type=F32)
        ure, uim, kre, kim = uf[:n], uf[n:], kf[:n], kf[n:]
        z = jnp.concatenate([ure * kre - uim * kim, ure * kim + uim * kre], axis=0).astype(BF16)
        y = jnp.dot(gi_ref[...], z, preferred_element_type=F32)
        return y + u * skip_ref[o:o + 1, :]

    v = v_ref[...].astype(F32)
    z1 = x1_ref[...].astype(F32) * conv(v, 0)
    o_ref[...] = (m2_ref[...].astype(F32) * conv(z1, 1)).astype(o_ref.dtype)


def _ctx_tables(lc):
    n = 2 * lc
    k = jnp.arange(n, dtype=jnp.int32)
    ph = _angles(k[:, None], k[None, :], n)
    c, s = jnp.cos(ph), jnp.sin(ph)
    fk = jnp.concatenate([c, -s], axis=0)
    ff = fk[:, :lc]
    gi = jnp.concatenate([c[:lc], -s[:lc]], axis=1) / n
    return ff.astype(BF16), fk.astype(BF16), gi.astype(BF16)


def _ctx_hyena(ctabs, k_un, ssum, layer, v, x1, m2, skip):
    b, lc, c = v.shape
    n = 2 * lc
    ff, fk, gi = ctabs
    wc = HYENA_ORDER * W_HYENA
    tok = pl.BlockSpec((None, lc, c), lambda bi: (bi, 0, 0))
    return pl.pallas_call(
        functools.partial(_ctx_conv_kernel, n=n),
        grid=(b,),
        in_specs=[tok, tok, tok,
                  pl.BlockSpec((n, wc), lambda bi: (0, layer)),
                  pl.BlockSpec((None, 1, wc), lambda bi: (layer, 0, 0)),
                  pl.BlockSpec((HYENA_ORDER, c), lambda bi: (0, 0)),
                  _full_spec(ff), _full_spec(fk), _full_spec(gi)],
        out_specs=tok,
        out_shape=jax.ShapeDtypeStruct((b, lc, c), BF16),
        compiler_params=_cparams(("parallel",)),
        name="ctx_hyena",
    )(v, x1, m2, k_un, ssum, skip, ff, fk, gi)


def _merge_kernel(xm_ref, xp_ref, xn_ref, gh_ref, att_ref, g_ref, sc_ref, sh_ref, gt_ref,
                  w1_ref, b1_ref, cw_ref, cb_ref, wg_ref, bg_ref, sco_ref, hyo_ref, mlo_ref, wo_ref,
                  fg_ref, o_ref, *, final, tm, tiles_per_seq):
    i = pl.program_id(0)
    first = (i % tiles_per_seq) == 0
    last = (i % tiles_per_seq) == tiles_per_seq - 1
    x = xm_ref[...]
    d = x.shape[-1]
    x_ext = jnp.concatenate([xp_ref[...], x, xn_ref[...]], axis=0)
    hb_ext = _mod_norm(x_ext, g_ref[...], sc_ref[...], sh_ref[...]).astype(BF16)
    p = jnp.dot(hb_ext, w1_ref[...], preferred_element_type=F32) + b1_ref[...]
    prod = p[:, 2 * W_CONV:3 * W_CONV] * p[:, 0:W_CONV]
    conv = _dwconv3_ext(prod, cw_ref[...], cb_ref[...], first, last, tm)
    pm = p[SUBLANE:SUBLANE + tm]
    a = (_silu(pm[:, 3 * W_CONV:4 * W_CONV]) * (pm[:, W_CONV:2 * W_CONV] * conv)).astype(BF16)
    sm = _silu(pm[:, 4 * W_CONV:4 * W_CONV + W_MLA])
    hb = hb_ext[SUBLANE:SUBLANE + tm]
    gates = jax.nn.sigmoid(jnp.dot(hb, wg_ref[...], preferred_element_type=F32) + bg_ref[...])
    ya = jnp.dot(a, sco_ref[...], preferred_element_type=F32)
    yh = jnp.dot(gh_ref[...], hyo_ref[...], preferred_element_type=F32)
    am = (sm * att_ref[...].astype(F32)).astype(BF16)
    ym = jnp.dot(am, mlo_ref[...], preferred_element_type=F32)
    y = gates[:, 0:d] * ya + gates[:, d:2 * d] * yh + gates[:, 2 * d:3 * d] * ym
    o = jnp.dot(y.astype(BF16), wo_ref[...], preferred_element_type=F32)
    xn = x + gt_ref[...] * o
    if final:
        xn = _rms(xn, fg_ref[...])
    o_ref[...] = xn


def _merge_call(x2d, seq_len, mods, g, lw, cw, cb, gh, att, final_g, final):
    n_rows, d = x2d.shape
    tm = min(512, seq_len)
    tiles_per_seq = seq_len // tm
    scale, shift, gate = mods
    g2 = g.reshape(1, 1, d)
    fg = final_g.reshape(1, d)
    tok = lambda wd: pl.BlockSpec((tm, wd), lambda i: (i, 0))
    consts = [lw['w1'], lw['b1'], cw, cb, lw['wg'], lw['bg'], lw['sc_out'], lw['hy_out'], lw['mla_out'],
              lw['w_o'], fg]
    in_specs = _halo_specs(n_rows, tm, d) + [
        tok(W_HYENA), tok(W_MLA),
        _mod_spec(g2, tiles_per_seq), _mod_spec(scale, tiles_per_seq),
        _mod_spec(shift, tiles_per_seq), _mod_spec(gate, tiles_per_seq)]
    in_specs += [_full_spec(c) for c in consts]
    return pl.pallas_call(
        functools.partial(_merge_kernel, final=final, tm=tm, tiles_per_seq=tiles_per_seq),
        grid=(n_rows // tm,),
        in_specs=in_specs,
        out_specs=tok(d),
        out_shape=jax.ShapeDtypeStruct((n_rows, d), F32),
        compiler_params=_cparams(("parallel",)),
        name="merge_final" if final else "merge",
    )(x2d, x2d, x2d, gh, att, g2, scale, shift, gate, *consts)


def _rope_swap_cols(w):
    half = QK_ROPE // 2
    return jnp.concatenate([-w[..., half:], w[..., :half]], axis=-1)


def _layer_weights(i, w_in, b_in, mla_q_norm, mla_w_uq, mla_kv_norm, mla_w_ukv, sc_out, hy_out,
                   mla_out, w_o):
    wi, bi = w_in[i], b_in[i]
    d = wi.shape[0]
    lw = {}
    lw['w1'] = jnp.concatenate([wi[:, O_XIN:O_HPROJ], wi[:, O_ZM:O_GATES]], axis=1).astype(BF16)
    lw['b1'] = jnp.concatenate([bi[O_XIN:O_HPROJ], bi[O_ZM:O_GATES]])[None, :]
    lw['w2'] = wi[:, O_HPROJ:O_CQ].astype(BF16)
    lw['b2'] = bi[O_HPROJ:O_CQ][None, :]
    wkr, bkr = wi[:, O_KR:O_ZM], bi[O_KR:O_ZM]
    zw = lambda n: jnp.zeros((d, n), F32)
    zb = lambda n: jnp.zeros((n,), F32)
    lw['w3'] = jnp.concatenate([wi[:, O_CQ:O_KR], wkr, zw(96), _rope_swap_cols(wkr), zw(96)], axis=1).astype(BF16)
    lw['b3'] = jnp.concatenate([bi[O_CQ:O_KR], bkr, zb(96), _rope_swap_cols(bkr), zb(96)])[None, :]
    lw['qg'] = mla_q_norm[i][None, :]
    lw['kvg'] = mla_kv_norm[i][None, :]
    wuq = mla_w_uq[i].reshape(Q_LORA, N_HEADS, QK_NOPE + QK_ROPE)
    pad = jnp.zeros((Q_LORA, N_HEADS, HEAD_PAD - QK_NOPE - QK_ROPE), F32)
    lw['wq'] = jnp.concatenate([wuq, pad], axis=-1).reshape(Q_LORA, N_HEADS * HEAD_PAD).astype(BF16)
    zn = jnp.zeros((Q_LORA, N_HEADS, QK_NOPE), F32)
    lw['wqs'] = jnp.concatenate([zn, _rope_swap_cols(wuq[..., QK_NOPE:]), pad], axis=-1).reshape(
        Q_LORA, N_HEADS * HEAD_PAD).astype(BF16)
    wukv = mla_w_ukv[i].reshape(KV_LORA, N_HEADS, QK_NOPE + V_HEAD)
    padk = jnp.zeros((KV_LORA, N_HEADS, HEAD_PAD - QK_NOPE), F32)
    lw['wk'] = jnp.concatenate([wukv[..., :QK_NOPE], padk], axis=-1).reshape(
        KV_LORA, N_HEADS * HEAD_PAD).astype(BF16)
    padv = jnp.zeros((KV_LORA, N_HEADS, HEAD_PAD - V_HEAD), F32)
    lw['wv'] = jnp.concatenate([wukv[..., QK_NOPE:], padv], axis=-1).reshape(
        KV_LORA, N_HEADS * HEAD_PAD).astype(BF16)
    lw['vb'] = jnp.zeros((N_HEADS, HEAD_PAD), F32).at[:, V_HEAD].set(1.0).reshape(1, N_HEADS * HEAD_PAD)
    e2 = jnp.zeros((QK_ROPE, N_HEADS, HEAD_PAD), F32).at[:, :, QK_NOPE:QK_NOPE + QK_ROPE].set(
        jnp.eye(QK_ROPE, dtype=F32)[:, None, :])
    lw['e2'] = e2.reshape(QK_ROPE, N_HEADS * HEAD_PAD).astype(BF16)
    lw['wg'] = wi[:, O_GATES:].astype(BF16)
    lw['bg'] = bi[O_GATES:][None, :]
    lw['sc_out'] = sc_out[i].astype(BF16)
    lw['hy_out'] = hy_out[i].astype(BF16)
    lw['mla_out'] = mla_out[i].astype(BF16)
    lw['w_o'] = w_o[i].astype(BF16)
    return lw


def _rope_tables(seq_len, use_rope):
    scale = (QK_NOPE + QK_ROPE) ** -0.5 * math.log2(math.e)
    if use_rope:
        rows = seq_len // GRID_W
        row = jnp.broadcast_to(jnp.arange(rows, dtype=F32)[:, None], (rows, GRID_W)).reshape(seq_len)
        col = jnp.broadcast_to(jnp.arange(GRID_W, dtype=F32)[None, :], (rows, GRID_W)).reshape(seq_len)
        n_f = QK_ROPE // 4
        inv = ROPE_BASE ** (-jnp.arange(n_f, dtype=F32) / n_f)
        ang = jnp.concatenate([row[:, None] * inv, col[:, None] * inv], axis=-1)
        cos, sin = jnp.cos(ang), jnp.sin(ang)
    else:
        cos = jnp.ones((seq_len, QK_ROPE // 2), F32)
        sin = jnp.zeros((seq_len, QK_ROPE // 2), F32)
    cosk = jnp.concatenate([cos, cos], axis=-1)
    sink = jnp.concatenate([sin, sin], axis=-1)
    ones = jnp.ones((seq_len, QK_NOPE), F32)
    zeros = jnp.zeros((seq_len, QK_NOPE), F32)
    tail = jnp.zeros((seq_len, HEAD_PAD - QK_NOPE - QK_ROPE), F32)
    cosq = jnp.concatenate([ones, cosk, tail], axis=-1) * scale
    sinq = jnp.concatenate([zeros, sink, tail], axis=-1) * scale
    return cosq, sinq, cosk, sink


def kernel(x, c, ctx, c_ctx, ada_w, ada_b, norm_g, w_in, b_in, sc_conv_w, sc_conv_b, sc_out, hy_conv_w,
           hy_conv_b, hy_w1, hy_b1, hy_w2, hy_b2, hy_w3, hy_freq, hy_skip, hy_out, mla_q_norm, mla_w_uq,
           mla_kv_norm, mla_w_ukv, mla_out, w_o, final_g):
    bsz, seq, d = x.shape
    lc = ctx.shape[1]
    depth = ada_w.shape[0]
    nh = seq // N2
    assert d == D_MODEL and bsz % 2 == 0 and nh % SUBLANE == 0 and bsz <= 7

    cc = jnp.zeros((8, d), F32).at[:bsz].set(c).at[bsz].set(c_ctx)
    mods = _ada_mods(cc, ada_w, ada_b)

    tabs = _dft_tables(nh)
    k_un, ssum = _filter_gen(seq, depth, True, hy_w1, hy_b1, hy_w2, hy_b2, hy_w3, hy_freq)
    kf = _filter_spectrum(tabs, nh, k_un, ssum)
    if depth > 1:
        kc_un, sc_sum = _filter_gen(lc, depth - 1, False, hy_w1, hy_b1, hy_w2, hy_b2, hy_w3, hy_freq)
        ctabs = _ctx_tables(lc)

    rope_l = _rope_tables(seq, True)
    rope_c = _rope_tables(lc, False)

    x_lat = x.reshape(bsz * seq, d)
    x_ctx = ctx.reshape(bsz * lc, d)
    hq = N_HEADS * HEAD_PAD
    k_all = jnp.zeros((bsz, seq + lc, hq), BF16)
    v_all = jnp.zeros((bsz, seq + lc, hq), BF16)
    for i in range(depth):
        last = i == depth - 1
        lw = _layer_weights(i, w_in, b_in, mla_q_norm, mla_w_uq, mla_kv_norm, mla_w_ukv, sc_out,
                            hy_out, mla_out, w_o)
        m = mods[i]
        split = lambda r: tuple(r[:, None, j * d:(j + 1) * d] for j in range(3))
        shift_l, scale_l, gate_l = split(m[:bsz])
        shift_c, scale_c, gate_c = split(m[bsz:bsz + 1])
        g = norm_g[i]
        scw, scb = sc_conv_w[i], sc_conv_b[i][None, :]
        hcw, hcb = hy_conv_w[i], hy_conv_b[i][None, :]

        v_h, x1_h, m2_h, q_l, k_all, v_all = _kp_call(
            x_lat, bsz, seq, 0, (scale_l, shift_l), g, lw, hcw, hcb, rope_l, (k_all, v_all))
        v_hc, x1_hc, m2_hc, q_c, k_all, v_all = _kp_call(
            x_ctx, bsz, lc, seq, (scale_c, shift_c), g, lw, hcw, hcb, rope_c, (k_all, v_all))
        att_l = _attention(q_l.reshape(bsz, seq, hq), k_all, v_all).reshape(bsz * seq, W_MLA)
        r3 = lambda t: t.reshape(bsz, seq, W_HYENA)
        gh_l = _hyena_long(tabs, nh, kf, i, r3(v_h), r3(x1_h), r3(m2_h), hy_skip[i])
        new_lat = _merge_call(x_lat, seq, (scale_l, shift_l, gate_l), g, lw, scw, scb,
                              gh_l.reshape(bsz * seq, W_HYENA), att_l, final_g, last)
        if not last:
            att_c = _attention(q_c.reshape(bsz, lc, hq), k_all, v_all, kv_row0=seq,
                               kv_len=lc).reshape(bsz * lc, W_MLA)
            rc = lambda t: t.reshape(bsz, lc, W_HYENA)
            gh_c = _ctx_hyena(ctabs, kc_un, sc_sum, i, rc(v_hc), rc(x1_hc), rc(m2_hc), hy_skip[i])
            x_ctx = _merge_call(x_ctx, lc, (scale_c, shift_c, gate_c), g, lw, scw, scb,
                                gh_c.reshape(bsz * lc, W_HYENA), att_c, final_g, False)
        x_lat = new_lat
    return x_lat.reshape(bsz, seq, d)
```
